```python
import math
import jax, jax.numpy as jnp
from jax import lax
import numpy as np

D_MODEL = 1024
BATCH = 8
SEQ = 2048
DEPTH = 2

CTX_LEN = 256
GRID_W = 64
MIX_WIDTH = D_MODEL
ATTN_WIDTH = MIX_WIDTH // 2
FOURIER_WIDTH = MIX_WIDTH - ATTN_WIDTH
HEAD_DIM = 64
N_HEADS = ATTN_WIDTH // (2 * HEAD_DIM)
V_HEAD_DIM = 2 * HEAD_DIM
N_FGROUPS = 4
FGROUP_DIM = FOURIER_WIDTH // N_FGROUPS
IN_COLS = 2 * ATTN_WIDTH + N_HEADS * V_HEAD_DIM + FOURIER_WIDTH
D_FF = -(-8 * D_MODEL // (3 * 256)) * 256
ROPE_AXIS_DIM = HEAD_DIM // 2
ROPE_THETA = 10000.0
Q_BLOCK = 128
EPS = 1e-6

kernel_name = "hybrid_diffattn_fnet_dit_block"


def rms_norm(x, gain):
    xf = x.astype(jnp.float32)
    y = xf * lax.rsqrt(jnp.mean(xf * xf, axis=-1, keepdims=True) + EPS)
    return (y * gain.astype(jnp.float32)).astype(x.dtype)


def modulate(h, shift, scale):
    return h * (1 + scale) + shift


def axial_rope_tables(n_tokens):
    rows = n_tokens // GRID_W
    row = jnp.broadcast_to(jnp.arange(rows, dtype=jnp.float32)[:, None], (rows, GRID_W)).reshape(-1)
    col = jnp.broadcast_to(jnp.arange(GRID_W, dtype=jnp.float32)[None, :], (rows, GRID_W)).reshape(-1)
    inv = ROPE_THETA ** (-jnp.arange(0, ROPE_AXIS_DIM, 2, dtype=jnp.float32) / ROPE_AXIS_DIM)
    ang = jnp.concatenate([row[:, None] * inv, col[:, None] * inv], axis=-1)
    return jnp.cos(ang), jnp.sin(ang)


def apply_axial_rope(t, cos, sin):
    tf = t.astype(jnp.float32)
    c = cos[:, None, None, :]
    s = sin[:, None, None, :]
    half = ROPE_AXIS_DIM // 2

    def rot(ta, ca, sa):
        t1, t2 = ta[..., :half], ta[..., half:]
        return jnp.concatenate([t1 * ca - t2 * sa, t2 * ca + t1 * sa], axis=-1)

    out = jnp.concatenate([
        rot(tf[..., :ROPE_AXIS_DIM], c[..., :half], s[..., :half]),
        rot(tf[..., ROPE_AXIS_DIM:], c[..., half:], s[..., half:]),
    ], axis=-1)
    return out.astype(t.dtype)


def split_proj(h, w_in, q_gain, k_gain):
    B, L, _ = h.shape
    z = h @ w_in
    q, k, v, f = jnp.split(z, [ATTN_WIDTH, 2 * ATTN_WIDTH, 2 * ATTN_WIDTH + N_HEADS * V_HEAD_DIM], axis=-1)
    q = rms_norm(q.reshape(B, L, N_HEADS, 2, HEAD_DIM), q_gain)
    k = rms_norm(k.reshape(B, L, N_HEADS, 2, HEAD_DIM), k_gain)
    v = v.reshape(B, L, N_HEADS, V_HEAD_DIM)
    return q, k, v, f


def to_heads_qk(t):
    return t.transpose(0, 2, 3, 1, 4)


def to_heads_v(t):
    return t.transpose(0, 2, 1, 3)


def diff_attend(q, k, v, lam):
    s = jnp.einsum('bhiqd,bhikd->bhiqk', q.astype(jnp.float32), k.astype(jnp.float32)) * (HEAD_DIM ** -0.5)
    p = jax.nn.softmax(s, axis=-1)
    a = p[:, :, 0] - lam * p[:, :, 1]
    return jnp.einsum('bhqk,bhkv->bhqv', a, v.astype(jnp.float32)).astype(v.dtype)


def blocked_diff_attend(q, k, v, lam):
    B, H, _, L, d = q.shape
    nb = L // Q_BLOCK
    qb = q.reshape(B, H, 2, nb, Q_BLOCK, d).transpose(3, 0, 1, 2, 4, 5)
    out = lax.map(lambda qblk: diff_attend(qblk, k, v, lam), qb)
    return out.transpose(1, 2, 0, 3, 4).reshape(B, H, L, V_HEAD_DIM)


def heads_out(o, subln_gain, lambda_init):
    o = rms_norm(o, subln_gain) * (1.0 - lambda_init)
    B, H, L, Dv = o.shape
    return o.transpose(0, 2, 1, 3).reshape(B, L, H * Dv)


def fourier_mix(f, w_f):
    B, L, _ = f.shape
    fg = f.reshape(B, L, N_FGROUPS, FGROUP_DIM).astype(jnp.float32)
    spec = jnp.fft.fft2(fg, axes=(1, 3), norm='ortho').real.astype(f.dtype)
    return jnp.einsum('blgc,gcd->blgd', spec, w_f).reshape(B, L, FOURIER_WIDTH)


def swiglu(h, w_gate, w_up, w_down):
    return (jax.nn.silu(h @ w_gate) * (h @ w_up)) @ w_down


def setup_inputs(seed: int = 0) -> dict:
    key = jax.random.key(seed)
    ks = jax.random.split(key, 24)
    f32 = jnp.float32
    nrm = lambda k, shape, s: jax.random.normal(k, shape, f32) * s
    D = D_MODEL
    return {
        'x': nrm(ks[0], (BATCH, SEQ, D), 1.0),
        'c': nrm(ks[1], (BATCH, D), 1.0),
        'ctx': nrm(ks[2], (BATCH, CTX_LEN, D), 1.0),
        'c_ctx': nrm(ks[3], (D,), 1.0),
        'w_ada': nrm(ks[4], (DEPTH, D, 6 * D), 0.5 * D ** -0.5),
        'b_ada': nrm(ks[5], (DEPTH, 6 * D), 0.01),
        'norm1_g': 1.0 + nrm(ks[6], (DEPTH, D), 0.02),
        'norm2_g': 1.0 + nrm(ks[7], (DEPTH, D), 0.02),
        'w_in': nrm(ks[8], (DEPTH, D, IN_COLS), D ** -0.5),
        'q_norm_g': 1.0 + nrm(ks[9], (DEPTH, HEAD_DIM), 0.02),
        'k_norm_g': 1.0 + nrm(ks[10], (DEPTH, HEAD_DIM), 0.02),
        'lambda_q1': nrm(ks[11], (DEPTH, HEAD_DIM), 0.1),
        'lambda_k1': nrm(ks[12], (DEPTH, HEAD_DIM), 0.1),
        'lambda_q2': nrm(ks[13], (DEPTH, HEAD_DIM), 0.1),
        'lambda_k2': nrm(ks[14], (DEPTH, HEAD_DIM), 0.1),
        'subln_g': 1.0 + nrm(ks[15], (DEPTH, V_HEAD_DIM), 0.02),
        'w_fourier': nrm(ks[16], (DEPTH, N_FGROUPS, FGROUP_DIM, FGROUP_DIM), FGROUP_DIM ** -0.5),
        'w_out': nrm(ks[17], (DEPTH, MIX_WIDTH, D), MIX_WIDTH ** -0.5),
        'w_gate': nrm(ks[18], (DEPTH, D, D_FF), D ** -0.5),
        'w_up': nrm(ks[19], (DEPTH, D, D_FF), D ** -0.5),
        'w_down': nrm(ks[20], (DEPTH, D_FF, D), D_FF ** -0.5),
    }


def reference(x, c, ctx, c_ctx, w_ada, b_ada, norm1_g, norm2_g, w_in, q_norm_g, k_norm_g,
              lambda_q1, lambda_k1, lambda_q2, lambda_k2, subln_g, w_fourier, w_out,
              w_gate, w_up, w_down):
    L = x.shape[1]
    cos, sin = axial_rope_tables(L)
    for i in range(DEPTH):
        last = i == DEPTH - 1
        lambda_init = 0.8 - 0.6 * math.exp(-0.3 * i)
        lam = (jnp.exp(jnp.sum(lambda_q1[i].astype(jnp.float32) * lambda_k1[i].astype(jnp.float32)))
               - jnp.exp(jnp.sum(lambda_q2[i].astype(jnp.float32) * lambda_k2[i].astype(jnp.float32)))
               + lambda_init)

        mod_x = jax.nn.silu(c) @ w_ada[i] + b_ada[i]
        mod_c = jax.nn.silu(c_ctx) @ w_ada[i] + b_ada[i]
        sh_a, sc_a, g_a, sh_f, sc_f, g_f = [m[:, None, :] for m in jnp.split(mod_x, 6, axis=-1)]
        csh_a, csc_a, cg_a, csh_f, csc_f, cg_f = jnp.split(mod_c, 6, axis=-1)

        hc = modulate(rms_norm(ctx, norm1_g[i]), csh_a, csc_a)
        qc, kc, vc, fc = split_proj(hc, w_in[i], q_norm_g[i], k_norm_g[i])
        kc_h, vc_h = to_heads_qk(kc), to_heads_v(vc)

        hx = modulate(rms_norm(x, norm1_g[i]), sh_a, sc_a)
        qx, kx, vx, fx = split_proj(hx, w_in[i], q_norm_g[i], k_norm_g[i])
        qx = apply_axial_rope(qx, cos, sin)
        kx = apply_axial_rope(kx, cos, sin)
        k_all = jnp.concatenate([kc_h, to_heads_qk(kx)], axis=3)
        v_all = jnp.concatenate([vc_h, to_heads_v(vx)], axis=2)
        attn_x = heads_out(blocked_diff_attend(to_heads_qk(qx), k_all, v_all, lam), subln_g[i], lambda_init)
        four_x = fourier_mix(fx, w_fourier[i])
        mix_x = jnp.concatenate([attn_x, four_x], axis=-1) @ w_out[i]
        x_new = x + g_a * mix_x
        hx2 = modulate(rms_norm(x_new, norm2_g[i]), sh_f, sc_f)
        x_new = x_new + g_f * swiglu(hx2, w_gate[i], w_up[i], w_down[i])

        if not last:
            attn_c = heads_out(diff_attend(to_heads_qk(qc), kc_h, vc_h, lam), subln_g[i], lambda_init)
            four_c = fourier_mix(fc, w_fourier[i])
            mix_c = jnp.concatenate([attn_c, four_c], axis=-1) @ w_out[i]
            ctx = ctx + cg_a * mix_c
            hc2 = modulate(rms_norm(ctx, norm2_g[i]), csh_f, csc_f)
            ctx = ctx + cg_f * swiglu(hc2, w_gate[i], w_up[i], w_down[i])
        x = x_new
    return x
```

```python
import functools
import math

import numpy as np
import jax
import jax.numpy as jnp
from jax import lax
from jax.experimental import pallas as pl
from jax.experimental.pallas import tpu as pltpu

D_MODEL = 1024
GRID_W = 64
ATTN_WIDTH = 512
FOURIER_WIDTH = 512
HEAD_DIM = 64
N_HEADS = 4
V_HEAD_DIM = 128
N_FGROUPS = 4
FGROUP_DIM = 128
IN_COLS = 2048
ROPE_AXIS_DIM = 32
ROPE_THETA = 10000.0
EPS = 1e-6

LANES = 128
MXU_DIM = 256
MOD_ROWS = 16
VMEM_LIMIT_BYTES = 56 * 1024 * 1024

F32 = jnp.float32
BF16 = jnp.bfloat16
_NT = (((1,), (1,)), ((), ()))


def _params(*sem):
    return pltpu.CompilerParams(dimension_semantics=sem, vmem_limit_bytes=VMEM_LIMIT_BYTES)


def _resident(shape):
    zeros = (0,) * len(shape)
    return pl.BlockSpec(shape, lambda *_: zeros, pipeline_mode=pl.Buffered(1))


def _rope_tables(n_tokens):
    rows = n_tokens // GRID_W
    row = np.repeat(np.arange(rows, dtype=np.float64), GRID_W)
    col = np.tile(np.arange(GRID_W, dtype=np.float64), rows)
    inv = np.float32(ROPE_THETA) ** (-np.arange(0, ROPE_AXIS_DIM, 2, dtype=np.float32) / np.float32(ROPE_AXIS_DIM))
    inv = inv.astype(np.float64)
    half = ROPE_AXIS_DIM // 2
    d = np.arange(HEAD_DIM)
    pos = np.where(d[None, :] < ROPE_AXIS_DIM, row[:, None], col[:, None])
    ang = pos * inv[d % half][None, :]
    low = (d % ROPE_AXIS_DIM) < half
    cos = np.cos(ang)
    sin = np.sin(ang)
    sin_up = np.where(low[None, :], -sin, 0.0)
    sin_dn = np.where(low[None, :], 0.0, sin)
    tile2 = lambda a: jnp.asarray(np.tile(a, (1, LANES // HEAD_DIM)), F32)
    return tile2(cos), tile2(sin_up), tile2(sin_dn)


def _channel_dft():
    n = FGROUP_DIM
    k = (np.arange(n)[:, None] * np.arange(n)[None, :]) % n
    ang = 2.0 * np.pi * k / n
    m = np.concatenate([np.cos(ang), np.sin(ang)], axis=1) / np.sqrt(n)
    return jnp.asarray(m, F32).astype(BF16)


def _position_dft(n_tokens):
    half = n_tokens // 2
    lo = np.arange(half)[:, None]
    m = np.arange(half)[None, :]
    mats = []
    for s in (0, 1):
        k = ((2 * m + s) * lo) % n_tokens
        ang = 2.0 * np.pi * k / n_tokens
        mats.append(jnp.asarray(np.cos(ang) / np.sqrt(n_tokens), F32).astype(BF16))
        mats.append(jnp.asarray(-np.sin(ang) / np.sqrt(n_tokens), F32).astype(BF16))
    return mats


def _segment_ones():
    seg = np.arange(MXU_DIM) // HEAD_DIM
    return jnp.asarray(seg[:, None] == seg[None, :], BF16)


def _mod_kernel(c_ref, w_ref, b_ref, o_ref):
    cv = c_ref[...]
    s = cv / (1.0 + jnp.exp(-cv))
    o_ref[0] = jnp.dot(s, w_ref[0], precision=lax.Precision.HIGHEST,
                       preferred_element_type=F32) + b_ref[0]


def _adaln_mod(cvec, w_ada, b_ada):
    depth, d, n = w_ada.shape
    tn = 1536
    return pl.pallas_call(
        _mod_kernel,
        grid=(depth, n // tn),
        in_specs=[pl.BlockSpec((MOD_ROWS, d), lambda i, j: (0, 0)),
                  pl.BlockSpec((1, d, tn), lambda i, j: (i, 0, j)),
                  pl.BlockSpec((1, 1, tn), lambda i, j: (i, 0, j))],
        out_specs=pl.BlockSpec((1, MOD_ROWS, tn), lambda i, j: (i, 0, j)),
        out_shape=jax.ShapeDtypeStruct((depth, MOD_ROWS, n), F32),
        compiler_params=_params("arbitrary", "arbitrary"),
        name="adaln_mod",
    )(cvec, w_ada, b_ada.reshape(depth, 1, n))


def _proj_kernel(*refs, rope):
    if rope:
        (x_ref, mod_ref, g1_ref, w_ref, qg_ref, kg_ref, ones_ref, cdft_ref,
         cos_ref, sup_ref, sdn_ref, q_ref, k_ref, v_ref, y_ref) = refs
    else:
        (x_ref, mod_ref, g1_ref, w_ref, qg_ref, kg_ref, ones_ref, cdft_ref,
         q_ref, k_ref, v_ref, y_ref) = refs
    x = x_ref[0]
    ms = jnp.mean(x * x, axis=-1, keepdims=True)
    h = x * lax.rsqrt(ms + EPS) * g1_ref[...]
    h = h * (1.0 + mod_ref[0, 1:2, :]) + mod_ref[0, 0:1, :]
    z = jnp.dot(h.astype(BF16), w_ref[...], preferred_element_type=F32)

    ones = ones_ref[...]
    if rope:
        cos, sup, sdn = cos_ref[...], sup_ref[...], sdn_ref[...]

    def norm_rope(t, gain, scale, out_ref):
        for cb in range(ATTN_WIDTH // MXU_DIM):
            tc = t[:, MXU_DIM * cb:MXU_DIM * (cb + 1)]
            ss = jnp.dot((tc * tc).astype(BF16), ones, preferred_element_type=F32)
            tn = tc * lax.rsqrt(ss * (1.0 / HEAD_DIM) + EPS) * gain
            for hb in range(MXU_DIM // LANES):
                u = tn[:, LANES * hb:LANES * (hb + 1)]
                if rope:
                    u = (u * cos + pltpu.roll(u, LANES - ROPE_AXIS_DIM // 2, 1) * sup
                         + pltpu.roll(u, ROPE_AXIS_DIM // 2, 1) * sdn)
                c0 = MXU_DIM * cb + LANES * hb
                out_ref[0, :, c0:c0 + LANES] = (u * scale).astype(BF16)

    norm_rope(z[:, 0:ATTN_WIDTH], qg_ref[...], HEAD_DIM ** -0.5, q_ref)
    norm_rope(z[:, ATTN_WIDTH:2 * ATTN_WIDTH], kg_ref[...], 1.0, k_ref)
    v_ref[0] = z[:, 2 * ATTN_WIDTH:3 * ATTN_WIDTH].astype(BF16)
    cdft = cdft_ref[...]
    f0 = 3 * ATTN_WIDTH
    for g in range(N_FGROUPS):
        fg = z[:, f0 + FGROUP_DIM * g:f0 + FGROUP_DIM * (g + 1)].astype(BF16)
        yg = jnp.dot(fg, cdft, preferred_element_type=F32)
        y_ref[0, :, FGROUP_DIM * g:FGROUP_DIM * (g + 1)] = yg[:, :FGROUP_DIM].astype(BF16)
        y_ref[0, :, FOURIER_WIDTH + FGROUP_DIM * g:FOURIER_WIDTH + FGROUP_DIM * (g + 1)] = (
            yg[:, FGROUP_DIM:].astype(BF16))


def _project(x, mod, g1, w_in, qg, kg, ones, cdft, rope_tabs, *, is_ctx, tm):
    b, l, d = x.shape
    rope = rope_tabs is not None
    mod_map = (lambda i, t: (b, 0, 0)) if is_ctx else (lambda i, t: (i, 0, 0))
    tok = lambda width: pl.BlockSpec((1, tm, width), lambda i, t: (i, t, 0))
    in_specs = [tok(d),
                pl.BlockSpec((1, 6, d), mod_map),
                _resident((1, d)),
                _resident((d, IN_COLS)),
                _resident((1, MXU_DIM)),
                _resident((1, MXU_DIM)),
                _resident((MXU_DIM, MXU_DIM)),
                _resident((FGROUP_DIM, 2 * FGROUP_DIM))]
    args = [x, mod, g1, w_in, qg, kg, ones, cdft]
    if rope:
        in_specs += [pl.BlockSpec((tm, LANES), lambda i, t: (t, 0))] * 3
        args += list(rope_tabs)
    out = lambda width: jax.ShapeDtypeStruct((b, l, width), BF16)
    return pl.pallas_call(
        functools.partial(_proj_kernel, rope=rope),
        grid=(b, l // tm),
        in_specs=in_specs,
        out_specs=[tok(ATTN_WIDTH), tok(ATTN_WIDTH), tok(ATTN_WIDTH), tok(2 * FOURIER_WIDTH)],
        out_shape=[out(ATTN_WIDTH), out(ATTN_WIDTH), out(ATTN_WIDTH), out(2 * FOURIER_WIDTH)],
        compiler_params=_params("arbitrary", "arbitrary"),
        name="proj_ctx" if is_ctx else "proj_lat",
    )(*args)


def _attn_kernel(*refs, n_seg, lambda_init):
    q_ref = refs[0]
    k_refs = refs[1:1 + n_seg]
    v_refs = refs[1 + n_seg:1 + 2 * n_seg]
    lam_ref, sg_ref, o_ref = refs[1 + 2 * n_seg:]
    lv = lam_ref[...]
    lam = (jnp.exp(jnp.sum(lv[0:1] * lv[1:2], axis=-1, keepdims=True))
           - jnp.exp(jnp.sum(lv[2:3] * lv[3:4], axis=-1, keepdims=True)) + lambda_init)

    q = q_ref[0]
    lane = lax.broadcasted_iota(jnp.int32, (1, LANES), 1)
    first = (lane < HEAD_DIM).astype(BF16)
    q1 = q * first
    q2 = q * (1.0 - first).astype(BF16)

    def softmax_parts(qm):
        ss = [lax.dot_general(qm, k[0], _NT, preferred_element_type=F32) for k in k_refs]
        m = functools.reduce(jnp.maximum, [jnp.max(s, axis=-1, keepdims=True) for s in ss])
        ps = [jnp.exp(s - m) for s in ss]
        l = functools.reduce(lambda a, c: a + c, [jnp.sum(p, axis=-1, keepdims=True) for p in ps])
        return ps, 1.0 / l

    p1, r1 = softmax_parts(q1)
    p2, r2 = softmax_parts(q2)
    r2 = r2 * lam
    o = None
    for pa, pb, v in zip(p1, p2, v_refs):
        a = (pa * r1 - pb * r2).astype(BF16)
        part = jnp.dot(a, v[0], preferred_element_type=F32)
        o = part if o is None else o + part
    ms = jnp.mean(o * o, axis=-1, keepdims=True)
    o = o * lax.rsqrt(ms + EPS) * sg_ref[...] * (1.0 - lambda_init)
    o_ref[0] = o.astype(BF16)


def _attention(q, ks, vs, lam_rows, subln_g, *, lambda_init, tq, name):
    b, lq, _ = q.shape
    n_seg = len(ks)
    qo_spec = pl.BlockSpec((1, tq, LANES), lambda i, h, t: (i, t, h))
    kv_specs = [pl.BlockSpec((1, k.shape[1], LANES), lambda i, h, t: (i, 0, h)) for k in ks]
    return pl.pallas_call(
        functools.partial(_attn_kernel, n_seg=n_seg, lambda_init=lambda_init),
        grid=(b, N_HEADS, lq // tq),
        in_specs=[qo_spec] + kv_specs + kv_specs + [_resident((4, HEAD_DIM)), _resident((1, V_HEAD_DIM))],
        out_specs=qo_spec,
        out_shape=jax.ShapeDtypeStruct((b, lq, ATTN_WIDTH), BF16),
        compiler_params=_params("arbitrary", "arbitrary", "arbitrary"),
        name=name,
    )(q, *ks, *vs, lam_rows, subln_g)


def _fourier_kernel(y_ref, ce_ref, se_ref, co_ref, so_ref, wf_ref, o_ref):
    w = FOURIER_WIDTH
    e = (jnp.dot(ce_ref[...], y_ref[0, :, 0:w], preferred_element_type=F32)
         + jnp.dot(se_ref[...], y_ref[0, :, w:2 * w], preferred_element_type=F32))
    o = (jnp.dot(co_ref[...], y_ref[0, :, 2 * w:3 * w], preferred_element_type=F32)
         + jnp.dot(so_ref[...], y_ref[0, :, 3 * w:4 * w], preferred_element_type=F32))
    top = (e + o).astype(BF16)
    bot = (e - o).astype(BF16)
    for g in range(N_FGROUPS):
        c0 = FGROUP_DIM * g
        wg = wf_ref[g]
        o_ref[0, 0, :, c0:c0 + FGROUP_DIM] = jnp.dot(
            top[:, c0:c0 + FGROUP_DIM], wg, preferred_element_type=F32).astype(BF16)
        o_ref[0, 1, :, c0:c0 + FGROUP_DIM] = jnp.dot(
            bot[:, c0:c0 + FGROUP_DIM], wg, preferred_element_type=F32).astype(BF16)


def _fourier(y, pos_mats, w_f, *, name):
    b, l, _ = y.shape
    half = l // 2
    y2 = y.reshape(b, half, 4 * FOURIER_WIDTH)
    out = pl.pallas_call(
        _fourier_kernel,
        grid=(b,),
        in_specs=[pl.BlockSpec((1, half, 4 * FOURIER_WIDTH), lambda i: (i, 0, 0))]
                 + [_resident((half, half))] * 4
                 + [_resident((N_FGROUPS, FGROUP_DIM, FGROUP_DIM))],
        out_specs=pl.BlockSpec((1, 2, half, FOURIER_WIDTH), lambda i: (i, 0, 0, 0)),
        out_shape=jax.ShapeDtypeStruct((b, 2, half, FOURIER_WIDTH), BF16),
        compiler_params=_params("arbitrary"),
        name=name,
    )(y2, *pos_mats, w_f)
    return out.reshape(b, l, FOURIER_WIDTH)


def _ffn_kernel(x_ref, a_ref, f_ref, mod_ref, g2_ref, wo_ref, wg_ref, wu_ref, wd_ref, o_ref):
    x = x_ref[0]
    mix = (jnp.dot(a_ref[0], wo_ref[0:ATTN_WIDTH, :], preferred_element_type=F32)
           + jnp.dot(f_ref[0], wo_ref[ATTN_WIDTH:, :], preferred_element_type=F32))
    x1 = x + mod_ref[0, 2:3, :] * mix
    ms = jnp.mean(x1 * x1, axis=-1, keepdims=True)
    h = x1 * lax.rsqrt(ms + EPS) * g2_ref[...]
    h = (h * (1.0 + mod_ref[0, 4:5, :]) + mod_ref[0, 3:4, :]).astype(BF16)
    gate = jnp.dot(h, wg_ref[...], preferred_element_type=F32)
    up = jnp.dot(h, wu_ref[...], preferred_element_type=F32)
    act = (gate / (1.0 + jnp.exp(-gate)) * up).astype(BF16)
    y = jnp.dot(act, wd_ref[...], preferred_element_type=F32)
    o_ref[0] = x1 + mod_ref[0, 5:6, :] * y


def _out_ffn(x, attn, four, mod, g2, w_out, w_gate, w_up, w_down, *, is_ctx, tm):
    b, l, d = x.shape
    d_ff = w_gate.shape[1]
    mod_map = (lambda i, t: (b, 0, 0)) if is_ctx else (lambda i, t: (i, 0, 0))
    tok = lambda width: pl.BlockSpec((1, tm, width), lambda i, t: (i, t, 0))
    return pl.pallas_call(
        _ffn_kernel,
        grid=(b, l // tm),
        in_specs=[tok(d), tok(ATTN_WIDTH), tok(FOURIER_WIDTH),
                  pl.BlockSpec((1, 6, d), mod_map),
                  _resident((1, d)),
                  _resident((d, d)),
                  _resident((d, d_ff)),
                  _resident((d, d_ff)),
                  _resident((d_ff, d))],
        out_specs=tok(d),
        out_shape=jax.ShapeDtypeStruct((b, l, d), F32),
        compiler_params=_params("arbitrary", "arbitrary"),
        name="out_ffn_ctx" if is_ctx else "out_ffn_lat",
    )(x, attn, four, mod, g2, w_out, w_gate, w_up, w_down)


def kernel(x, c, ctx, c_ctx, w_ada, b_ada, norm1_g, norm2_g, w_in, q_norm_g, k_norm_g, lambda_q1, lambda_k1, lambda_q2, lambda_k2, subln_g, w_fourier, w_out, w_gate, w_up, w_down):
    depth = w_ada.shape[0]
    b, l, d = x.shape
    lc = ctx.shape[1]
    assert b + 1 <= MOD_ROWS

    rope_tabs = _rope_tables(l)
    cdft = _channel_dft()
    pos_lat = _position_dft(l)
    pos_ctx = _position_dft(lc)
    ones = _segment_ones()

    cvec = jnp.zeros((MOD_ROWS, d), F32).at[:b].set(c).at[b].set(c_ctx)
    mod_all = _adaln_mod(cvec, w_ada, b_ada).reshape(depth, MOD_ROWS, 6, d)

    w_in_b, w_out_b = w_in.astype(BF16), w_out.astype(BF16)
    w_gate_b, w_up_b, w_down_b = w_gate.astype(BF16), w_up.astype(BF16), w_down.astype(BF16)
    w_f_b = w_fourier.astype(BF16)
    tile_gain = lambda g: jnp.tile(g, MXU_DIM // HEAD_DIM).reshape(1, MXU_DIM)

    for i in range(depth):
        last = i == depth - 1
        lambda_init = 0.8 - 0.6 * math.exp(-0.3 * i)
        mod = mod_all[i]
        g1, g2 = norm1_g[i].reshape(1, d), norm2_g[i].reshape(1, d)
        qg, kg = tile_gain(q_norm_g[i]), tile_gain(k_norm_g[i])
        lam_rows = jnp.stack([lambda_q1[i], lambda_k1[i], lambda_q2[i], lambda_k2[i]])
        sg = subln_g[i].reshape(1, V_HEAD_DIM)

        qc, kc, vc, yc = _project(ctx, mod, g1, w_in_b[i], qg, kg, ones, cdft, None, is_ctx=True, tm=lc)
        qx, kx, vx, yx = _project(x, mod, g1, w_in_b[i], qg, kg, ones, cdft, rope_tabs, is_ctx=False, tm=512)

        attn_x = _attention(qx, [kc, kx], [vc, vx], lam_rows, sg, lambda_init=lambda_init, tq=256, name="attn_lat")
        four_x = _fourier(yx, pos_lat, w_f_b[i], name="fourier_lat")
        x_new = _out_ffn(x, attn_x, four_x, mod, g2, w_out_b[i], w_gate_b[i], w_up_b[i], w_down_b[i],
                         is_ctx=False, tm=512)
        if not last:
            attn_c = _attention(qc, [kc], [vc], lam_rows, sg, lambda_init=lambda_init, tq=lc, name="attn_ctx")
            four_c = _fourier(yc, pos_ctx, w_f_b[i], name="fourier_ctx")
            ctx = _out_ffn(ctx, attn_c, four_c, mod, g2, w_out_b[i], w_gate_b[i], w_up_b[i], w_down_b[i],
                           is_ctx=True, tm=lc)
        x = x_new
    return x
```

```python
import functools
import math

import numpy as np
import jax
import jax.numpy as jnp
from jax import lax
from jax.experimental import pallas as pl
from jax.experimental.pallas import tpu as pltpu

D_MODEL = 1024
GRID_W = 64
ATTN_WIDTH = 512
FOURIER_WIDTH = 512
HEAD_DIM = 64
N_HEADS = 4
V_HEAD_DIM = 128
N_FGROUPS = 4
FGROUP_DIM = 128
IN_COLS = 2048
ROPE_AXIS_DIM = 32
ROPE_THETA = 10000.0
EPS = 1e-6

LANES = 128
MXU_DIM = 256
MOD_ROWS = 16
VMEM_LIMIT_BYTES = 56 * 1024 * 1024
SAFE_SCORE_BOUND = 40.0

F32 = jnp.float32
BF16 = jnp.bfloat16
_NT = (((1,), (1,)), ((), ()))


def _params(*sem):
    return pltpu.CompilerParams(dimension_semantics=sem, vmem_limit_bytes=VMEM_LIMIT_BYTES)


def _resident(shape, layer=None):
    zeros = (0,) * len(shape)
    if layer is None:
        return pl.BlockSpec(shape, lambda *_: zeros, pipeline_mode=pl.Buffered(1))
    return pl.BlockSpec((None,) + tuple(shape), lambda *_: (layer,) + zeros, pipeline_mode=pl.Buffered(1))


def _mod_spec(layer, row_of):
    return pl.BlockSpec((None, None, 6, D_MODEL), lambda *g: (layer, row_of(*g), 0, 0))


def _rope_tables(n_tokens):
    rows = n_tokens // GRID_W
    row = np.repeat(np.arange(rows, dtype=np.float64), GRID_W)
    col = np.tile(np.arange(GRID_W, dtype=np.float64), rows)
    inv = np.float32(ROPE_THETA) ** (-np.arange(0, ROPE_AXIS_DIM, 2, dtype=np.float32) / np.float32(ROPE_AXIS_DIM))
    inv = inv.astype(np.float64)
    half = ROPE_AXIS_DIM // 2
    d = np.arange(HEAD_DIM)
    pos = np.where(d[None, :] < ROPE_AXIS_DIM, row[:, None], col[:, None])
    ang = pos * inv[d % half][None, :]
    low = (d % ROPE_AXIS_DIM) < half
    cos = np.cos(ang)
    sin = np.sin(ang)
    sin_up = np.where(low[None, :], -sin, 0.0)
    sin_dn = np.where(low[None, :], 0.0, sin)
    tile2 = lambda a: jnp.asarray(np.tile(a, (1, LANES // HEAD_DIM)), F32)
    return tile2(cos), tile2(sin_up), tile2(sin_dn)


def _channel_dft():
    n = FGROUP_DIM
    k = (np.arange(n)[:, None] * np.arange(n)[None, :]) % n
    ang = 2.0 * np.pi * k / n
    m = np.concatenate([np.cos(ang), np.sin(ang)], axis=1) / np.sqrt(n)
    return jnp.asarray(m, F32).astype(BF16)


def _position_dft(n_tokens):
    half = n_tokens // 2
    lo = np.arange(half)[:, None]
    m = np.arange(half)[None, :]
    mats = []
    for s in (0, 1):
        k = ((2 * m + s) * lo) % n_tokens
        ang = 2.0 * np.pi * k / n_tokens
        mats.append(jnp.asarray(np.cos(ang) / np.sqrt(n_tokens), F32).astype(BF16))
        mats.append(jnp.asarray(-np.sin(ang) / np.sqrt(n_tokens), F32).astype(BF16))
    return mats


def _segment_ones():
    seg = np.arange(MXU_DIM) // HEAD_DIM
    return jnp.asarray(seg[:, None] == seg[None, :], BF16)


def _mod_kernel(c_ref, w_ref, b_ref, o_ref):
    cv = c_ref[...]
    s = cv / (1.0 + jnp.exp(-cv))
    o_ref[0] = jnp.dot(s, w_ref[0], precision=lax.Precision.HIGHEST,
                       preferred_element_type=F32) + b_ref[0]


def _adaln_mod(cvec, w_ada, b_ada):
    depth, d, n = w_ada.shape
    tn = 1536
    return pl.pallas_call(
        _mod_kernel,
        grid=(depth, n // tn),
        in_specs=[pl.BlockSpec((MOD_ROWS, d), lambda i, j: (0, 0)),
                  pl.BlockSpec((1, d, tn), lambda i, j: (i, 0, j)),
                  pl.BlockSpec((1, 1, tn), lambda i, j: (i, 0, j))],
        out_specs=pl.BlockSpec((1, MOD_ROWS, tn), lambda i, j: (i, 0, j)),
        out_shape=jax.ShapeDtypeStruct((depth, MOD_ROWS, n), F32),
        compiler_params=_params("arbitrary", "arbitrary"),
        name="adaln_mod",
    )(cvec, w_ada, b_ada.reshape(depth, 1, n))


def _proj_kernel(*refs, rope):
    if rope:
        (x_ref, mod_ref, g1_ref, w_ref, qg_ref, kg_ref, ones_ref, cdft_ref,
         cos_ref, sup_ref, sdn_ref, q_ref, k_ref, v_ref, y_ref, f_scr) = refs
    else:
        (x_ref, mod_ref, g1_ref, w_ref, qg_ref, kg_ref, ones_ref, cdft_ref,
         q_ref, k_ref, v_ref, y_ref, f_scr) = refs
    x = x_ref[0]
    tm = x.shape[0]
    ms = jnp.mean(x * x, axis=-1, keepdims=True)
    h = x * lax.rsqrt(ms + EPS) * g1_ref[...]
    h = h * (1.0 + mod_ref[1:2, :]) + mod_ref[0:1, :]
    z = jnp.dot(h.astype(BF16), w_ref[...], preferred_element_type=F32)

    ones = ones_ref[...]
    if rope:
        cos, sup, sdn = cos_ref[...], sup_ref[...], sdn_ref[...]

    def norm_rope(t, gain, scale, out_ref):
        for cb in range(ATTN_WIDTH // MXU_DIM):
            tc = t[:, MXU_DIM * cb:MXU_DIM * (cb + 1)]
            ss = jnp.dot((tc * tc).astype(BF16), ones, preferred_element_type=F32)
            tn = tc * lax.rsqrt(ss * (1.0 / HEAD_DIM) + EPS) * gain
            for hb in range(MXU_DIM // LANES):
                u = tn[:, LANES * hb:LANES * (hb + 1)]
                if rope:
                    u = (u * cos + pltpu.roll(u, LANES - ROPE_AXIS_DIM // 2, 1) * sup
                         + pltpu.roll(u, ROPE_AXIS_DIM // 2, 1) * sdn)
                c0 = MXU_DIM * cb + LANES * hb
                out_ref[0, :, c0:c0 + LANES] = (u * scale).astype(BF16)

    norm_rope(z[:, 0:ATTN_WIDTH], qg_ref[...], HEAD_DIM ** -0.5, q_ref)
    norm_rope(z[:, ATTN_WIDTH:2 * ATTN_WIDTH], kg_ref[...], 1.0, k_ref)
    v_ref[0] = z[:, 2 * ATTN_WIDTH:3 * ATTN_WIDTH].astype(BF16)

    cdft = cdft_ref[...]
    w = FOURIER_WIDTH
    for g in range(N_FGROUPS):
        c0 = FGROUP_DIM * g
        f_scr[g] = z[:, 3 * ATTN_WIDTH + c0:3 * ATTN_WIDTH + c0 + FGROUP_DIM]
        for parity in range(2):
            fp = f_scr[g, pl.ds(parity, tm // 2, stride=2), :].astype(BF16)
            yg = jnp.dot(fp, cdft, preferred_element_type=F32)
            base = 2 * w * parity
            y_ref[0, :, base + c0:base + c0 + FGROUP_DIM] = yg[:, :FGROUP_DIM].astype(BF16)
            y_ref[0, :, base + w + c0:base + w + c0 + FGROUP_DIM] = yg[:, FGROUP_DIM:].astype(BF16)


def _project(x, mod_all, g1, w_in, qg, kg, ones, cdft, rope_tabs, *, layer, is_ctx, tm):
    b, l, d = x.shape
    rope = rope_tabs is not None
    row_of = (lambda i, t: b) if is_ctx else (lambda i, t: i)
    tok = lambda width: pl.BlockSpec((1, tm, width), lambda i, t: (i, t, 0))
    in_specs = [tok(d),
                _mod_spec(layer, row_of),
                _resident((1, d), layer),
                _resident((d, IN_COLS), layer),
                _resident((1, MXU_DIM), layer),
                _resident((1, MXU_DIM), layer),
                _resident((MXU_DIM, MXU_DIM)),
                _resident((FGROUP_DIM, 2 * FGROUP_DIM))]
    args = [x, mod_all, g1, w_in, qg, kg, ones, cdft]
    if rope:
        in_specs += [pl.BlockSpec((tm, LANES), lambda i, t: (t, 0))] * 3
        args += list(rope_tabs)
    out = lambda width: jax.ShapeDtypeStruct((b, l, width), BF16)
    return pl.pallas_call(
        functools.partial(_proj_kernel, rope=rope),
        grid=(b, l // tm),
        in_specs=in_specs,
        out_specs=[tok(ATTN_WIDTH), tok(ATTN_WIDTH), tok(ATTN_WIDTH),
                   pl.BlockSpec((1, tm // 2, 4 * FOURIER_WIDTH), lambda i, t: (i, t, 0))],
        out_shape=[out(ATTN_WIDTH), out(ATTN_WIDTH), out(ATTN_WIDTH),
                   jax.ShapeDtypeStruct((b, l // 2, 4 * FOURIER_WIDTH), BF16)],
        scratch_shapes=[pltpu.VMEM((N_FGROUPS, tm, FGROUP_DIM), F32)],
        compiler_params=_params("arbitrary", "arbitrary"),
        name="proj_ctx" if is_ctx else "proj_lat",
    )(*args)


def _attn_kernel(*refs, n_seg, lambda_init):
    bound_ref, q_ref = refs[0], refs[1]
    k_refs = refs[2:2 + n_seg]
    v_refs = refs[2 + n_seg:2 + 2 * n_seg]
    lam_ref, sg_ref, o_ref = refs[2 + 2 * n_seg:]

    def attend(use_bound):
        lv = lam_ref[...]
        lam = (jnp.exp(jnp.sum(lv[0:1] * lv[1:2], axis=-1, keepdims=True))
               - jnp.exp(jnp.sum(lv[2:3] * lv[3:4], axis=-1, keepdims=True)) + lambda_init)
        q = q_ref[0]
        lane = lax.broadcasted_iota(jnp.int32, (1, LANES), 1)
        first = (lane < HEAD_DIM).astype(BF16)

        def one_map(qm):
            ss = [lax.dot_general(qm, k[0], _NT, preferred_element_type=F32) for k in k_refs]
            if use_bound:
                m = bound_ref[0, 0]
            else:
                m = functools.reduce(jnp.maximum, [jnp.max(s, axis=-1, keepdims=True) for s in ss])
            ps = [jnp.exp(s - m) for s in ss]
            l = functools.reduce(lambda a, c: a + c, [jnp.sum(p, axis=-1, keepdims=True) for p in ps])
            o = functools.reduce(lambda a, c: a + c, [
                jnp.dot(p.astype(BF16), v[0], preferred_element_type=F32) for p, v in zip(ps, v_refs)])
            return o, 1.0 / l

        o1, r1 = one_map(q * first)
        o2, r2 = one_map(q * (1.0 - first).astype(BF16))
        o = o1 * r1 - o2 * (r2 * lam)
        ms = jnp.mean(o * o, axis=-1, keepdims=True)
        o = o * lax.rsqrt(ms + EPS) * sg_ref[...] * (1.0 - lambda_init)
        o_ref[0] = o.astype(BF16)

    safe = bound_ref[0, 0] <= SAFE_SCORE_BOUND
    pl.when(safe)(lambda: attend(True))
    pl.when(jnp.logical_not(safe))(lambda: attend(False))


def _attention(bound, q, ks, vs, lam_rows, subln_g, *, layer, lambda_init, tq, name):
    b, lq, _ = q.shape
    n_seg = len(ks)
    qo_spec = pl.BlockSpec((1, tq, LANES), lambda i, h, t: (i, t, h))
    kv_specs = [pl.BlockSpec((1, k.shape[1], LANES), lambda i, h, t: (i, 0, h)) for k in ks]
    return pl.pallas_call(
        functools.partial(_attn_kernel, n_seg=n_seg, lambda_init=lambda_init),
        grid=(b, N_HEADS, lq // tq),
        in_specs=[pl.BlockSpec(memory_space=pltpu.SMEM), qo_spec] + kv_specs + kv_specs
                 + [_resident((4, HEAD_DIM), layer), _resident((1, V_HEAD_DIM), layer)],
        out_specs=qo_spec,
        out_shape=jax.ShapeDtypeStruct((b, lq, ATTN_WIDTH), BF16),
        compiler_params=_params("arbitrary", "arbitrary", "arbitrary"),
        name=name,
    )(bound, q, *ks, *vs, lam_rows, subln_g)


def _fourier_kernel(y_ref, ce_ref, se_ref, co_ref, so_ref, wf_ref, o_ref):
    w = FOURIER_WIDTH
    e = (jnp.dot(ce_ref[...], y_ref[0, :, 0:w], preferred_element_type=F32)
         + jnp.dot(se_ref[...], y_ref[0, :, w:2 * w], preferred_element_type=F32))
    o = (jnp.dot(co_ref[...], y_ref[0, :, 2 * w:3 * w], preferred_element_type=F32)
         + jnp.dot(so_ref[...], y_ref[0, :, 3 * w:4 * w], preferred_element_type=F32))
    top = (e + o).astype(BF16)
    bot = (e - o).astype(BF16)
    for g in range(N_FGROUPS):
        c0 = FGROUP_DIM * g
        wg = wf_ref[g]
        o_ref[0, 0, :, c0:c0 + FGROUP_DIM] = jnp.dot(
            top[:, c0:c0 + FGROUP_DIM], wg, preferred_element_type=F32).astype(BF16)
        o_ref[0, 1, :, c0:c0 + FGROUP_DIM] = jnp.dot(
            bot[:, c0:c0 + FGROUP_DIM], wg, preferred_element_type=F32).astype(BF16)


def _fourier(y2, pos_mats, w_f, *, layer, name):
    b, half, _ = y2.shape
    out = pl.pallas_call(
        _fourier_kernel,
        grid=(b,),
        in_specs=[pl.BlockSpec((1, half, 4 * FOURIER_WIDTH), lambda i: (i, 0, 0))]
                 + [_resident((half, half))] * 4
                 + [_resident((N_FGROUPS, FGROUP_DIM, FGROUP_DIM), layer)],
        out_specs=pl.BlockSpec((1, 2, half, FOURIER_WIDTH), lambda i: (i, 0, 0, 0)),
        out_shape=jax.ShapeDtypeStruct((b, 2, half, FOURIER_WIDTH), BF16),
        compiler_params=_params("arbitrary"),
        name=name,
    )(y2, *pos_mats, w_f)
    return out.reshape(b, 2 * half, FOURIER_WIDTH)


def _ffn_kernel(x_ref, a_ref, f_ref, mod_ref, g2_ref, wo_ref, wg_ref, wu_ref, wd_ref, o_ref):
    x = x_ref[0]
    mix = (jnp.dot(a_ref[0], wo_ref[0:ATTN_WIDTH, :], preferred_element_type=F32)
           + jnp.dot(f_ref[0], wo_ref[ATTN_WIDTH:, :], preferred_element_type=F32))
    x1 = x + mod_ref[2:3, :] * mix
    ms = jnp.mean(x1 * x1, axis=-1, keepdims=True)
    h = x1 * lax.rsqrt(ms + EPS) * g2_ref[...]
    h = (h * (1.0 + mod_ref[4:5, :]) + mod_ref[3:4, :]).astype(BF16)
    gate = jnp.dot(h, wg_ref[...], preferred_element_type=F32)
    up = jnp.dot(h, wu_ref[...], preferred_element_type=F32)
    act = (gate / (1.0 + jnp.exp(-gate)) * up).astype(BF16)
    y = jnp.dot(act, wd_ref[...], preferred_element_type=F32)
    o_ref[0] = x1 + mod_ref[5:6, :] * y


def _out_ffn(x, attn, four, mod_all, g2, w_out, w_gate, w_up, w_down, *, layer, is_ctx, tm):
    b, l, d = x.shape
    d_ff = w_gate.shape[-1]
    row_of = (lambda i, t: b) if is_ctx else (lambda i, t: i)
    tok = lambda width: pl.BlockSpec((1, tm, width), lambda i, t: (i, t, 0))
    return pl.pallas_call(
        _ffn_kernel,
        grid=(b, l // tm),
        in_specs=[tok(d), tok(ATTN_WIDTH), tok(FOURIER_WIDTH),
                  _mod_spec(layer, row_of),
                  _resident((1, d), layer),
                  _resident((d, d), layer),
                  _resident((d, d_ff), layer),
                  _resident((d, d_ff), layer),
                  _resident((d_ff, d), layer)],
        out_specs=tok(d),
        out_shape=jax.ShapeDtypeStruct((b, l, d), F32),
        compiler_params=_params("arbitrary", "arbitrary"),
        name="out_ffn_ctx" if is_ctx else "out_ffn_lat",
    )(x, attn, four, mod_all, g2, w_out, w_gate, w_up, w_down)


def kernel(x, c, ctx, c_ctx, w_ada, b_ada, norm1_g, norm2_g, w_in, q_norm_g, k_norm_g, lambda_q1, lambda_k1, lambda_q2, lambda_k2, subln_g, w_fourier, w_out, w_gate, w_up, w_down):
    depth = w_ada.shape[0]
    b, l, d = x.shape
    lc = ctx.shape[1]
    assert b + 1 <= MOD_ROWS

    rope_tabs = _rope_tables(l)
    cdft = _channel_dft()
    pos_lat = _position_dft(l)
    pos_ctx = _position_dft(lc)
    ones = _segment_ones()

    cvec = jnp.zeros((MOD_ROWS, d), F32).at[:b].set(c).at[b].set(c_ctx)
    mod_all = _adaln_mod(cvec, w_ada, b_ada).reshape(depth, MOD_ROWS, 6, d)

    w_in_b, w_out_b = w_in.astype(BF16), w_out.astype(BF16)
    w_gate_b, w_up_b, w_down_b = w_gate.astype(BF16), w_up.astype(BF16), w_down.astype(BF16)
    w_f_b = w_fourier.astype(BF16)
    g1, g2 = norm1_g.reshape(depth, 1, d), norm2_g.reshape(depth, 1, d)
    tile_gain = lambda g: jnp.tile(g, (1, MXU_DIM // HEAD_DIM)).reshape(depth, 1, MXU_DIM)
    qg, kg = tile_gain(q_norm_g), tile_gain(k_norm_g)
    lam_rows = jnp.stack([lambda_q1, lambda_k1, lambda_q2, lambda_k2], axis=1)
    sg = subln_g.reshape(depth, 1, V_HEAD_DIM)
    score_bound = (HEAD_DIM ** 0.5) * jnp.max(jnp.abs(q_norm_g), axis=1) * jnp.max(jnp.abs(k_norm_g), axis=1)

    for i in range(depth):
        last = i == depth - 1
        lambda_init = 0.8 - 0.6 * math.exp(-0.3 * i)
        bound = score_bound[i].reshape(1, 1)
        proj = functools.partial(_project, mod_all=mod_all, g1=g1, w_in=w_in_b, qg=qg, kg=kg,
                                 ones=ones, cdft=cdft, layer=i)
        attend = functools.partial(_attention, bound, lam_rows=lam_rows, subln_g=sg, layer=i,
                                   lambda_init=lambda_init)
        ffn = functools.partial(_out_ffn, mod_all=mod_all, g2=g2, w_out=w_out_b, w_gate=w_gate_b,
                                w_up=w_up_b, w_down=w_down_b, layer=i)

        qc, kc, vc, yc = proj(ctx, rope_tabs=None, is_ctx=True, tm=lc)
        qx, kx, vx, yx = proj(x, rope_tabs=rope_tabs, is_ctx=False, tm=512)

        attn_x = attend(qx, [kc, kx], [vc, vx], tq=256, name="attn_lat")
        four_x = _fourier(yx, pos_lat, w_f_b, layer=i, name="fourier_lat")
        x_new = ffn(x, attn_x, four_x, is_ctx=False, tm=512)
        if not last:
            attn_c = attend(qc, [kc], [vc], tq=lc, name="attn_ctx")
            four_c = _fourier(yc, pos_ctx, w_f_b, layer=i, name="fourier_ctx")
            ctx = ffn(ctx, attn_c, four_c, is_ctx=True, tm=lc)
        x = x_new
    return x
```

```python
import functools
import math

import numpy as np
import jax
import jax.numpy as jnp
from jax import lax
from jax.experimental import pallas as pl
from jax.experimental.pallas import tpu as pltpu

D_MODEL = 1024
GRID_W = 64
ATTN_WIDTH = 512
FOURIER_WIDTH = 512
HEAD_DIM = 64
N_HEADS = 4
V_HEAD_DIM = 128
N_FGROUPS = 4
FGROUP_DIM = 128
IN_COLS = 2048
ROPE_AXIS_DIM = 32
ROPE_THETA = 10000.0
EPS = 1e-6

LANES = 128
MXU_DIM = 256
MOD_ROWS = 16
VMEM_LIMIT_BYTES = 56 * 1024 * 1024
SAFE_SCORE_BOUND = 40.0
LOG2_E = 1.4426950408889634

F32 = jnp.float32
BF16 = jnp.bfloat16
_NT = (((1,), (1,)), ((), ()))


def _params(*sem):
    return pltpu.CompilerParams(dimension_semantics=sem, vmem_limit_bytes=VMEM_LIMIT_BYTES)


def _resident(shape, layer=None):
    zeros = (0,) * len(shape)
    if layer is None:
        return pl.BlockSpec(shape, lambda *_: zeros, pipeline_mode=pl.Buffered(1))
    return pl.BlockSpec((None,) + tuple(shape), lambda *_: (layer,) + zeros, pipeline_mode=pl.Buffered(1))


def _mod_spec(layer, row_of):
    return pl.BlockSpec((None, None, 6, D_MODEL), lambda *g: (layer, row_of(*g), 0, 0))


def _rope_tables(n_tokens):
    rows = n_tokens // GRID_W
    row = np.repeat(np.arange(rows, dtype=np.float64), GRID_W)
    col = np.tile(np.arange(GRID_W, dtype=np.float64), rows)
    inv = np.float32(ROPE_THETA) ** (-np.arange(0, ROPE_AXIS_DIM, 2, dtype=np.float32) / np.float32(ROPE_AXIS_DIM))
    inv = inv.astype(np.float64)
    half = ROPE_AXIS_DIM // 2
    d = np.arange(HEAD_DIM)
    pos = np.where(d[None, :] < ROPE_AXIS_DIM, row[:, None], col[:, None])
    ang = pos * inv[d % half][None, :]
    low = (d % ROPE_AXIS_DIM) < half
    cos = np.cos(ang)
    sin = np.sin(ang)
    sin_up = np.where(low[None, :], -sin, 0.0)
    sin_dn = np.where(low[None, :], 0.0, sin)
    tile2 = lambda a: jnp.asarray(np.tile(a, (1, LANES // HEAD_DIM)), F32)
    return tile2(cos), tile2(sin_up), tile2(sin_dn)


def _channel_dft():
    n = FGROUP_DIM
    k = (np.arange(n)[:, None] * np.arange(n)[None, :]) % n
    ang = 2.0 * np.pi * k / n
    m = np.concatenate([np.cos(ang), np.sin(ang)], axis=1) / np.sqrt(n)
    return jnp.asarray(m, F32).astype(BF16)


def _position_dft(n_tokens):
    half = n_tokens // 2
    lo = np.arange(half)[:, None]
    m = np.arange(half)[None, :]
    mats = []
    for s in (0, 1):
        k = ((2 * m + s) * lo) % n_tokens
        ang = 2.0 * np.pi * k / n_tokens
        mats.append(jnp.asarray(np.cos(ang) / np.sqrt(n_tokens), F32).astype(BF16))
        mats.append(jnp.asarray(-np.sin(ang) / np.sqrt(n_tokens), F32).astype(BF16))
    return mats


def _segment_ones():
    seg = np.arange(MXU_DIM) // HEAD_DIM
    return jnp.asarray(seg[:, None] == seg[None, :], BF16)


def _mod_kernel(c_ref, w_ref, b_ref, o_ref):
    cv = c_ref[...]
    s = cv / (1.0 + jnp.exp(-cv))
    o_ref[0] = jnp.dot(s, w_ref[0], precision=lax.Precision.HIGHEST,
                       preferred_element_type=F32) + b_ref[0]


def _adaln_mod(cvec, w_ada, b_ada):
    depth, d, n = w_ada.shape
    tn = 1536
    return pl.pallas_call(
        _mod_kernel,
        grid=(depth, n // tn),
        in_specs=[pl.BlockSpec((MOD_ROWS, d), lambda i, j: (0, 0)),
                  pl.BlockSpec((1, d, tn), lambda i, j: (i, 0, j)),
                  pl.BlockSpec((1, 1, tn), lambda i, j: (i, 0, j))],
        out_specs=pl.BlockSpec((1, MOD_ROWS, tn), lambda i, j: (i, 0, j)),
        out_shape=jax.ShapeDtypeStruct((depth, MOD_ROWS, n), F32),
        compiler_params=_params("arbitrary", "arbitrary"),
        name="adaln_mod",
    )(cvec, w_ada, b_ada.reshape(depth, 1, n))


def _proj_kernel(*refs, rope):
    if rope:
        (x_ref, mod_ref, g1_ref, w_ref, qg_ref, kg_ref, ones_ref, cdft_ref,
         cos_ref, sup_ref, sdn_ref, q_ref, k_ref, v_ref, y_ref, f_scr) = refs
    else:
        (x_ref, mod_ref, g1_ref, w_ref, qg_ref, kg_ref, ones_ref, cdft_ref,
         q_ref, k_ref, v_ref, y_ref, f_scr) = refs
    x = x_ref[0]
    tm = x.shape[0]
    ms = jnp.mean(x * x, axis=-1, keepdims=True)
    h = x * lax.rsqrt(ms + EPS) * g1_ref[...]
    h = h * (1.0 + mod_ref[1:2, :]) + mod_ref[0:1, :]
    z = jnp.dot(h.astype(BF16), w_ref[...], preferred_element_type=F32)

    ones = ones_ref[...]
    if rope:
        cos, sup, sdn = cos_ref[...], sup_ref[...], sdn_ref[...]

    def norm_rope(t, gain, scale, out_ref):
        for cb in range(ATTN_WIDTH // MXU_DIM):
            tc = t[:, MXU_DIM * cb:MXU_DIM * (cb + 1)]
            ss = jnp.dot((tc * tc).astype(BF16), ones, preferred_element_type=F32)
            tn = tc * lax.rsqrt(ss * (1.0 / HEAD_DIM) + EPS) * gain
            for hb in range(MXU_DIM // LANES):
                u = tn[:, LANES * hb:LANES * (hb + 1)]
                if rope:
                    u = (u * cos + pltpu.roll(u, LANES - ROPE_AXIS_DIM // 2, 1) * sup
                         + pltpu.roll(u, ROPE_AXIS_DIM // 2, 1) * sdn)
                c0 = MXU_DIM * cb + LANES * hb
                out_ref[0, :, c0:c0 + LANES] = (u * scale).astype(BF16)

    norm_rope(z[:, 0:ATTN_WIDTH], qg_ref[...], HEAD_DIM ** -0.5 * LOG2_E, q_ref)
    norm_rope(z[:, ATTN_WIDTH:2 * ATTN_WIDTH], kg_ref[...], 1.0, k_ref)
    v_ref[0] = z[:, 2 * ATTN_WIDTH:3 * ATTN_WIDTH].astype(BF16)

    cdft = cdft_ref[...]
    w = FOURIER_WIDTH
    for g in range(N_FGROUPS):
        c0 = FGROUP_DIM * g
        f_scr[g] = z[:, 3 * ATTN_WIDTH + c0:3 * ATTN_WIDTH + c0 + FGROUP_DIM]
        for parity in range(2):
            fp = f_scr[g, pl.ds(parity, tm // 2, stride=2), :].astype(BF16)
            yg = jnp.dot(fp, cdft, preferred_element_type=F32)
            base = 2 * w * parity
            y_ref[0, :, base + c0:base + c0 + FGROUP_DIM] = yg[:, :FGROUP_DIM].astype(BF16)
            y_ref[0, :, base + w + c0:base + w + c0 + FGROUP_DIM] = yg[:, FGROUP_DIM:].astype(BF16)


def _project(x, mod_all, g1, w_in, qg, kg, ones, cdft, rope_tabs, *, layer, is_ctx, tm):
    b, l, d = x.shape
    rope = rope_tabs is not None
    row_of = (lambda i, t: b) if is_ctx else (lambda i, t: i)
    tok = lambda width: pl.BlockSpec((1, tm, width), lambda i, t: (i, t, 0))
    in_specs = [tok(d),
                _mod_spec(layer, row_of),
                _resident((1, d), layer),
                _resident((d, IN_COLS), layer),
                _resident((1, MXU_DIM), layer),
                _resident((1, MXU_DIM), layer),
                _resident((MXU_DIM, MXU_DIM)),
                _resident((FGROUP_DIM, 2 * FGROUP_DIM))]
    args = [x, mod_all, g1, w_in, qg, kg, ones, cdft]
    if rope:
        in_specs += [pl.BlockSpec((tm, LANES), lambda i, t: (t, 0))] * 3
        args += list(rope_tabs)
    out = lambda width: jax.ShapeDtypeStruct((b, l, width), BF16)
    return pl.pallas_call(
        functools.partial(_proj_kernel, rope=rope),
        grid=(b, l // tm),
        in_specs=in_specs,
        out_specs=[tok(ATTN_WIDTH), tok(ATTN_WIDTH), tok(ATTN_WIDTH),
                   pl.BlockSpec((1, tm // 2, 4 * FOURIER_WIDTH), lambda i, t: (i, t, 0))],
        out_shape=[out(ATTN_WIDTH), out(ATTN_WIDTH), out(ATTN_WIDTH),
                   jax.ShapeDtypeStruct((b, l // 2, 4 * FOURIER_WIDTH), BF16)],
        scratch_shapes=[pltpu.VMEM((N_FGROUPS, tm, FGROUP_DIM), F32)],
        compiler_params=_params("arbitrary", "arbitrary"),
        name="proj_ctx" if is_ctx else "proj_lat",
    )(*args)


def _attn_kernel(*refs, n_seg, lambda_init, sub_rows):
    bound_ref, q_ref = refs[0], refs[1]
    k_refs = refs[2:2 + n_seg]
    v_refs = refs[2 + n_seg:2 + 2 * n_seg]
    lam_ref, sg_ref, o_ref = refs[2 + 2 * n_seg:]

    def attend(use_bound):
        lv = lam_ref[...]
        lam = (jnp.exp(jnp.sum(lv[0:1] * lv[1:2], axis=-1, keepdims=True))
               - jnp.exp(jnp.sum(lv[2:3] * lv[3:4], axis=-1, keepdims=True)) + lambda_init)
        lane = lax.broadcasted_iota(jnp.int32, (1, LANES), 1)
        first = (lane < HEAD_DIM).astype(BF16)
        tiles = [pl.ds(r0, sub_rows) for r0 in range(0, q_ref.shape[1], sub_rows)]
        probs = scores(tiles[0], first, use_bound)
        for t, rows in enumerate(tiles):
            nxt = scores(tiles[t + 1], first, use_bound) if t + 1 < len(tiles) else None
            finish(rows, probs, lam)
            probs = nxt

    def scores(rows, first, use_bound):
        q = q_ref[0, rows, :]

        def one_map(qm):
            ss = [lax.dot_general(qm, k[0], _NT, preferred_element_type=F32) for k in k_refs]
            if use_bound:
                m = bound_ref[0, 0]
            else:
                m = functools.reduce(jnp.maximum, [jnp.max(s, axis=-1, keepdims=True) for s in ss])
            ps = [jnp.exp2(s - m) for s in ss]
            l = functools.reduce(lambda a, c: a + c, [jnp.sum(p, axis=-1, keepdims=True) for p in ps])
            return [p.astype(BF16) for p in ps], l

        return one_map(q * first), one_map(q * (1.0 - first).astype(BF16))

    def finish(rows, probs, lam):
        (p1, l1), (p2, l2) = probs
        c = (lam * l1 / l2).astype(BF16)
        o = functools.reduce(lambda a, b: a + b, [
            jnp.dot(pa - c * pb, v[0], preferred_element_type=F32) for pa, pb, v in zip(p1, p2, v_refs)])
        o = o * (1.0 / l1)
        ms = jnp.mean(o * o, axis=-1, keepdims=True)
        o = o * lax.rsqrt(ms + EPS) * sg_ref[...] * (1.0 - lambda_init)
        o_ref[0, rows, :] = o.astype(BF16)

    safe = bound_ref[0, 0] <= SAFE_SCORE_BOUND
    pl.when(safe)(lambda: attend(True))
    pl.when(jnp.logical_not(safe))(lambda: attend(False))


def _attention(bound, q, ks, vs, lam_rows, subln_g, *, layer, lambda_init, tq, name):
    b, lq, _ = q.shape
    n_seg = len(ks)
    qo_spec = pl.BlockSpec((1, tq, LANES), lambda i, h, t: (i, t, h))
    kv_specs = [pl.BlockSpec((1, k.shape[1], LANES), lambda i, h, t: (i, 0, h)) for k in ks]
    return pl.pallas_call(
        functools.partial(_attn_kernel, n_seg=n_seg, lambda_init=lambda_init, sub_rows=min(tq, MXU_DIM)),
        grid=(b, N_HEADS, lq // tq),
        in_specs=[pl.BlockSpec(memory_space=pltpu.SMEM), qo_spec] + kv_specs + kv_specs
                 + [_resident((4, HEAD_DIM), layer), _resident((1, V_HEAD_DIM), layer)],
        out_specs=qo_spec,
        out_shape=jax.ShapeDtypeStruct((b, lq, ATTN_WIDTH), BF16),
        compiler_params=_params("arbitrary", "arbitrary", "arbitrary"),
        name=name,
    )(bound, q, *ks, *vs, lam_rows, subln_g)


def _fourier_kernel(y_ref, ce_ref, se_ref, co_ref, so_ref, wf_ref, o_ref):
    w = FOURIER_WIDTH
    e = (jnp.dot(ce_ref[...], y_ref[0, :, 0:w], preferred_element_type=F32)
         + jnp.dot(se_ref[...], y_ref[0, :, w:2 * w], preferred_element_type=F32))
    o = (jnp.dot(co_ref[...], y_ref[0, :, 2 * w:3 * w], preferred_element_type=F32)
         + jnp.dot(so_ref[...], y_ref[0, :, 3 * w:4 * w], preferred_element_type=F32))
    top = (e + o).astype(BF16)
    bot = (e - o).astype(BF16)
    for g in range(N_FGROUPS):
        c0 = FGROUP_DIM * g
        wg = wf_ref[g]
        o_ref[0, 0, :, c0:c0 + FGROUP_DIM] = jnp.dot(
            top[:, c0:c0 + FGROUP_DIM], wg, preferred_element_type=F32).astype(BF16)
        o_ref[0, 1, :, c0:c0 + FGROUP_DIM] = jnp.dot(
            bot[:, c0:c0 + FGROUP_DIM], wg, preferred_element_type=F32).astype(BF16)


def _fourier(y2, pos_mats, w_f, *, layer, name):
    b, half, _ = y2.shape
    out = pl.pallas_call(
        _fourier_kernel,
        grid=(b,),
        in_specs=[pl.BlockSpec((1, half, 4 * FOURIER_WIDTH), lambda i: (i, 0, 0))]
                 + [_resident((half, half))] * 4
                 + [_resident((N_FGROUPS, FGROUP_DIM, FGROUP_DIM), layer)],
        out_specs=pl.BlockSpec((1, 2, half, FOURIER_WIDTH), lambda i: (i, 0, 0, 0)),
        out_shape=jax.ShapeDtypeStruct((b, 2, half, FOURIER_WIDTH), BF16),
        compiler_params=_params("arbitrary"),
        name=name,
    )(y2, *pos_mats, w_f)
    return out.reshape(b, 2 * half, FOURIER_WIDTH)


def _ffn_kernel(x_ref, a_ref, f_ref, mod_ref, g2_ref, wo_ref, wg_ref, wu_ref, wd_ref, o_ref):
    x = x_ref[0]
    mix = (jnp.dot(a_ref[0], wo_ref[0:ATTN_WIDTH, :], preferred_element_type=F32)
           + jnp.dot(f_ref[0], wo_ref[ATTN_WIDTH:, :], preferred_element_type=F32))
    x1 = x + mod_ref[2:3, :] * mix
    ms = jnp.mean(x1 * x1, axis=-1, keepdims=True)
    h = x1 * lax.rsqrt(ms + EPS) * g2_ref[...]
    h = (h * (1.0 + mod_ref[4:5, :]) + mod_ref[3:4, :]).astype(BF16)
    gate = jnp.dot(h, wg_ref[...], preferred_element_type=F32)
    up = jnp.dot(h, wu_ref[...], preferred_element_type=F32)
    act = (gate / (1.0 + jnp.exp(-gate)) * up).astype(BF16)
    y = jnp.dot(act, wd_ref[...], preferred_element_type=F32)
    o_ref[0] = x1 + mod_ref[5:6, :] * y


def _out_ffn(x, attn, four, mod_all, g2, w_out, w_gate, w_up, w_down, *, layer, is_ctx, tm):
    b, l, d = x.shape
    d_ff = w_gate.shape[-1]
    row_of = (lambda i, t: b) if is_ctx else (lambda i, t: i)
    tok = lambda width: pl.BlockSpec((1, tm, width), lambda i, t: (i, t, 0))
    return pl.pallas_call(
        _ffn_kernel,
        grid=(b, l // tm),
        in_specs=[tok(d), tok(ATTN_WIDTH), tok(FOURIER_WIDTH),
                  _mod_spec(layer, row_of),
                  _resident((1, d), layer),
                  _resident((d, d), layer),
                  _resident((d, d_ff), layer),
                  _resident((d, d_ff), layer),
                  _resident((d_ff, d), layer)],
        out_specs=tok(d),
        out_shape=jax.ShapeDtypeStruct((b, l, d), F32),
        compiler_params=_params("arbitrary", "arbitrary"),
        name="out_ffn_ctx" if is_ctx else "out_ffn_lat",
    )(x, attn, four, mod_all, g2, w_out, w_gate, w_up, w_down)


def kernel(x, c, ctx, c_ctx, w_ada, b_ada, norm1_g, norm2_g, w_in, q_norm_g, k_norm_g, lambda_q1, lambda_k1, lambda_q2, lambda_k2, subln_g, w_fourier, w_out, w_gate, w_up, w_down):
    depth = w_ada.shape[0]
    b, l, d = x.shape
    lc = ctx.shape[1]
    assert b + 1 <= MOD_ROWS

    rope_tabs = _rope_tables(l)
    cdft = _channel_dft()
    pos_lat = _position_dft(l)
    pos_ctx = _position_dft(lc)
    ones = _segment_ones()

    cvec = jnp.zeros((MOD_ROWS, d), F32).at[:b].set(c).at[b].set(c_ctx)
    mod_all = _adaln_mod(cvec, w_ada, b_ada).reshape(depth, MOD_ROWS, 6, d)

    w_in_b, w_out_b = w_in.astype(BF16), w_out.astype(BF16)
    w_gate_b, w_up_b, w_down_b = w_gate.astype(BF16), w_up.astype(BF16), w_down.astype(BF16)
    w_f_b = w_fourier.astype(BF16)
    g1, g2 = norm1_g.reshape(depth, 1, d), norm2_g.reshape(depth, 1, d)
    tile_gain = lambda g: jnp.tile(g, (1, MXU_DIM // HEAD_DIM)).reshape(depth, 1, MXU_DIM)
    qg, kg = tile_gain(q_norm_g), tile_gain(k_norm_g)
    lam_rows = jnp.stack([lambda_q1, lambda_k1, lambda_q2, lambda_k2], axis=1)
    sg = subln_g.reshape(depth, 1, V_HEAD_DIM)
    score_bound = (HEAD_DIM ** 0.5 * LOG2_E) * jnp.max(jnp.abs(q_norm_g), axis=1) * jnp.max(jnp.abs(k_norm_g), axis=1)

    for i in range(depth):
        last = i == depth - 1
        lambda_init = 0.8 - 0.6 * math.exp(-0.3 * i)
        bound = score_bound[i].reshape(1, 1)
        proj = functools.partial(_project, mod_all=mod_all, g1=g1, w_in=w_in_b, qg=qg, kg=kg,
                                 ones=ones, cdft=cdft, layer=i)
        attend = functools.partial(_attention, bound, lam_rows=lam_rows, subln_g=sg, layer=i,
                                   lambda_init=lambda_init)
        ffn = functools.partial(_out_ffn, mod_all=mod_all, g2=g2, w_out=w_out_b, w_gate=w_gate_b,
                                w_up=w_up_b, w_down=w_down_b, layer=i)

        qc, kc, vc, yc = proj(ctx, rope_tabs=None, is_ctx=True, tm=lc)
        qx, kx, vx, yx = proj(x, rope_tabs=rope_tabs, is_ctx=False, tm=512)

        attn_x = attend(qx, [kc, kx], [vc, vx], tq=1024, name="attn_lat")
        four_x = _fourier(yx, pos_lat, w_f_b, layer=i, name="fourier_lat")
        x_new = ffn(x, attn_x, four_x, is_ctx=False, tm=512)
        if not last:
            attn_c = attend(qc, [kc], [vc], tq=lc, name="attn_ctx")
            four_c = _fourier(yc, pos_ctx, w_f_b, layer=i, name="fourier_ctx")
            ctx = ffn(ctx, attn_c, four_c, is_ctx=True, tm=lc)
        x = x_new
    return x
```

```python
import functools
import math

import numpy as np
import jax
import jax.numpy as jnp
from jax import lax
from jax.experimental import pallas as pl
from jax.experimental.pallas import tpu as pltpu

D_MODEL = 1024
GRID_W = 64
ATTN_WIDTH = 512
FOURIER_WIDTH = 512
HEAD_DIM = 64
N_HEADS = 4
V_HEAD_DIM = 128
N_FGROUPS = 4
FGROUP_DIM = 128
IN_COLS = 2048
ROPE_AXIS_DIM = 32
ROPE_THETA = 10000.0
EPS = 1e-6

LANES = 128
MXU_DIM = 256
MOD_ROWS = 16
VMEM_LIMIT_BYTES = 56 * 1024 * 1024
SAFE_SCORE_BOUND = 40.0
LOG2_E = 1.4426950408889634

F32 = jnp.float32
BF16 = jnp.bfloat16
_NT = (((1,), (1,)), ((), ()))


def _params(*sem):
    return pltpu.CompilerParams(dimension_semantics=sem, vmem_limit_bytes=VMEM_LIMIT_BYTES)


def _resident(shape, layer=None):
    zeros = (0,) * len(shape)
    if layer is None:
        return pl.BlockSpec(shape, lambda *_: zeros, pipeline_mode=pl.Buffered(1))
    return pl.BlockSpec((None,) + tuple(shape), lambda *_: (layer,) + zeros, pipeline_mode=pl.Buffered(1))


def _mod_spec(layer, row_of):
    return pl.BlockSpec((None, None, 6, D_MODEL), lambda *g: (layer, row_of(*g), 0, 0))


def _rope_tables(n_tokens):
    rows = n_tokens // GRID_W
    row = np.repeat(np.arange(rows, dtype=np.float64), GRID_W)
    col = np.tile(np.arange(GRID_W, dtype=np.float64), rows)
    inv = np.float32(ROPE_THETA) ** (-np.arange(0, ROPE_AXIS_DIM, 2, dtype=np.float32) / np.float32(ROPE_AXIS_DIM))
    inv = inv.astype(np.float64)
    half = ROPE_AXIS_DIM // 2
    d = np.arange(HEAD_DIM)
    pos = np.where(d[None, :] < ROPE_AXIS_DIM, row[:, None], col[:, None])
    ang = pos * inv[d % half][None, :]
    low = (d % ROPE_AXIS_DIM) < half
    cos = np.cos(ang)
    sin = np.sin(ang)
    sin_up = np.where(low[None, :], -sin, 0.0)
    sin_dn = np.where(low[None, :], 0.0, sin)
    tile2 = lambda a: jnp.asarray(np.tile(a, (1, LANES // HEAD_DIM)), F32)
    return tile2(cos), tile2(sin_up), tile2(sin_dn)


def _channel_dft():
    n = FGROUP_DIM
    k = (np.arange(n)[:, None] * np.arange(n)[None, :]) % n
    ang = 2.0 * np.pi * k / n
    m = np.concatenate([np.cos(ang), np.sin(ang)], axis=1) / np.sqrt(n)
    return jnp.asarray(m, F32).astype(BF16)


def _position_dft(n_tokens):
    half = n_tokens // 2
    lo = np.arange(half)[:, None]
    m = np.arange(half)[None, :]
    mats = []
    for s in (0, 1):
        k = ((2 * m + s) * lo) % n_tokens
        ang = 2.0 * np.pi * k / n_tokens
        mats.append(jnp.asarray(np.cos(ang) / np.sqrt(n_tokens), F32).astype(BF16))
        mats.append(jnp.asarray(-np.sin(ang) / np.sqrt(n_tokens), F32).astype(BF16))
    return mats


def _segment_ones():
    seg = np.arange(MXU_DIM) // HEAD_DIM
    return jnp.asarray(seg[:, None] == seg[None, :], BF16)


def _mod_kernel(c_ref, w_ref, b_ref, o_ref):
    cv = c_ref[...]
    s = cv / (1.0 + jnp.exp(-cv))
    o_ref[0] = jnp.dot(s.astype(BF16), w_ref[0].astype(BF16), preferred_element_type=F32) + b_ref[0]


def _adaln_mod(cvec, w_ada, b_ada):
    depth, d, n = w_ada.shape
    tn = 1536
    return pl.pallas_call(
        _mod_kernel,
        grid=(depth, n // tn),
        in_specs=[pl.BlockSpec((MOD_ROWS, d), lambda i, j: (0, 0)),
                  pl.BlockSpec((1, d, tn), lambda i, j: (i, 0, j)),
                  pl.BlockSpec((1, 1, tn), lambda i, j: (i, 0, j))],
        out_specs=pl.BlockSpec((1, MOD_ROWS, tn), lambda i, j: (i, 0, j)),
        out_shape=jax.ShapeDtypeStruct((depth, MOD_ROWS, n), F32),
        compiler_params=_params("arbitrary", "arbitrary"),
        name="adaln_mod",
    )(cvec, w_ada, b_ada.reshape(depth, 1, n))


def _proj_kernel(*refs, rope):
    if rope:
        (x_ref, mod_ref, g1_ref, w_ref, qg_ref, kg_ref, ones_ref, cdft_ref,
         cos_ref, sup_ref, sdn_ref, q_ref, k_ref, v_ref, y_ref, f_scr) = refs
    else:
        (x_ref, mod_ref, g1_ref, w_ref, qg_ref, kg_ref, ones_ref, cdft_ref,
         q_ref, k_ref, v_ref, y_ref, f_scr) = refs
    x = x_ref[0]
    tm = x.shape[0]
    ms = jnp.mean(x * x, axis=-1, keepdims=True)
    h = x * lax.rsqrt(ms + EPS) * g1_ref[...]
    h = h * (1.0 + mod_ref[1:2, :]) + mod_ref[0:1, :]
    z = jnp.dot(h.astype(BF16), w_ref[...], preferred_element_type=F32)

    ones = ones_ref[...]
    if rope:
        cos, sup, sdn = cos_ref[...], sup_ref[...], sdn_ref[...]

    def norm_rope(t, gain, scale, out_ref):
        for cb in range(ATTN_WIDTH // MXU_DIM):
            tc = t[:, MXU_DIM * cb:MXU_DIM * (cb + 1)]
            ss = jnp.dot((tc * tc).astype(BF16), ones, preferred_element_type=F32)
            tn = tc * lax.rsqrt(ss * (1.0 / HEAD_DIM) + EPS) * gain
            for hb in range(MXU_DIM // LANES):
                u = tn[:, LANES * hb:LANES * (hb + 1)]
                if rope:
                    u = (u * cos + pltpu.roll(u, LANES - ROPE_AXIS_DIM // 2, 1) * sup
                         + pltpu.roll(u, ROPE_AXIS_DIM // 2, 1) * sdn)
                c0 = MXU_DIM * cb + LANES * hb
                out_ref[0, :, c0:c0 + LANES] = (u * scale).astype(BF16)

    norm_rope(z[:, 0:ATTN_WIDTH], qg_ref[...], HEAD_DIM ** -0.5 * LOG2_E, q_ref)
    norm_rope(z[:, ATTN_WIDTH:2 * ATTN_WIDTH], kg_ref[...], 1.0, k_ref)
    v_ref[0] = z[:, 2 * ATTN_WIDTH:3 * ATTN_WIDTH].astype(BF16)

    cdft = cdft_ref[...]
    w = FOURIER_WIDTH
    for g in range(N_FGROUPS):
        c0 = FGROUP_DIM * g
        f_scr[g] = z[:, 3 * ATTN_WIDTH + c0:3 * ATTN_WIDTH + c0 + FGROUP_DIM]
        for parity in range(2):
            fp = f_scr[g, pl.ds(parity, tm // 2, stride=2), :].astype(BF16)
            yg = jnp.dot(fp, cdft, preferred_element_type=F32)
            base = 2 * w * parity
            y_ref[0, :, base + c0:base + c0 + FGROUP_DIM] = yg[:, :FGROUP_DIM].astype(BF16)
            y_ref[0, :, base + w + c0:base + w + c0 + FGROUP_DIM] = yg[:, FGROUP_DIM:].astype(BF16)


def _project(x, mod_all, g1, w_in, qg, kg, ones, cdft, rope_tabs, *, layer, is_ctx, tm):
    b, l, d = x.shape
    rope = rope_tabs is not None
    row_of = (lambda i, t: b) if is_ctx else (lambda i, t: i)
    tok = lambda width: pl.BlockSpec((1, tm, width), lambda i, t: (i, t, 0))
    in_specs = [tok(d),
                _mod_spec(layer, row_of),
                _resident((1, d), layer),
                _resident((d, IN_COLS), layer),
                _resident((1, MXU_DIM), layer),
                _resident((1, MXU_DIM), layer),
                _resident((MXU_DIM, MXU_DIM)),
                _resident((FGROUP_DIM, 2 * FGROUP_DIM))]
    args = [x, mod_all, g1, w_in, qg, kg, ones, cdft]
    if rope:
        in_specs += [pl.BlockSpec((tm, LANES), lambda i, t: (t, 0))] * 3
        args += list(rope_tabs)
    out = lambda width: jax.ShapeDtypeStruct((b, l, width), BF16)
    return pl.pallas_call(
        functools.partial(_proj_kernel, rope=rope),
        grid=(b, l // tm),
        in_specs=in_specs,
        out_specs=[tok(ATTN_WIDTH), tok(ATTN_WIDTH), tok(ATTN_WIDTH),
                   pl.BlockSpec((1, tm // 2, 4 * FOURIER_WIDTH), lambda i, t: (i, t, 0))],
        out_shape=[out(ATTN_WIDTH), out(ATTN_WIDTH), out(ATTN_WIDTH),
                   jax.ShapeDtypeStruct((b, l // 2, 4 * FOURIER_WIDTH), BF16)],
        scratch_shapes=[pltpu.VMEM((N_FGROUPS, tm, FGROUP_DIM), F32)],
        compiler_params=_params("arbitrary", "arbitrary"),
        name="proj_ctx" if is_ctx else "proj_lat",
    )(*args)


def _attn_kernel(*refs, n_seg, lambda_init, sub_rows):
    bound_ref, q_ref = refs[0], refs[1]
    k_refs = refs[2:2 + n_seg]
    v_refs = refs[2 + n_seg:2 + 2 * n_seg]
    lam_ref, sg_ref, o_ref = refs[2 + 2 * n_seg:]

    def attend(bounded):
        lv = lam_ref[...]
        lam = (jnp.exp(jnp.sum(lv[0:1] * lv[1:2], axis=-1, keepdims=True))
               - jnp.exp(jnp.sum(lv[2:3] * lv[3:4], axis=-1, keepdims=True)) + lambda_init)
        lane = lax.broadcasted_iota(jnp.int32, (1, LANES), 1)
        first = (lane < HEAD_DIM).astype(BF16)
        tiles = [pl.ds(r0, sub_rows) for r0 in range(0, q_ref.shape[1], sub_rows)]
        probs = scores(tiles[0], first, bounded)
        for t, rows in enumerate(tiles):
            nxt = scores(tiles[t + 1], first, bounded) if t + 1 < len(tiles) else None
            finish(rows, probs, lam)
            probs = nxt

    def scores(rows, first, bounded):
        q = q_ref[0, rows, :]

        def one_map(qm):
            ss = [lax.dot_general(qm, k[0], _NT, preferred_element_type=F32) for k in k_refs]
            if bounded:
                ps = [jnp.exp2(s) for s in ss]
            else:
                m = functools.reduce(jnp.maximum, [jnp.max(s, axis=-1, keepdims=True) for s in ss])
                ps = [jnp.exp2(s - m) for s in ss]
            l = functools.reduce(lambda a, c: a + c, [jnp.sum(p, axis=-1, keepdims=True) for p in ps])
            return [p.astype(BF16) for p in ps], l

        return one_map(q * first), one_map(q * (1.0 - first).astype(BF16))

    def finish(rows, probs, lam):
        (p1, l1), (p2, l2) = probs
        c = (lam * l1 / l2).astype(BF16)
        o = functools.reduce(lambda a, b: a + b, [
            jnp.dot(pa - c * pb, v[0], preferred_element_type=F32) for pa, pb, v in zip(p1, p2, v_refs)])
        o = o * (1.0 / l1)
        ms = jnp.mean(o * o, axis=-1, keepdims=True)
        o = o * lax.rsqrt(ms + EPS) * sg_ref[...] * (1.0 - lambda_init)
        o_ref[0, rows, :] = o.astype(BF16)

    safe = bound_ref[0, 0] <= SAFE_SCORE_BOUND
    pl.when(safe)(lambda: attend(True))
    pl.when(jnp.logical_not(safe))(lambda: attend(False))


def _attention(bound, q, ks, vs, lam_rows, subln_g, *, layer, lambda_init, tq, name):
    b, lq, _ = q.shape
    n_seg = len(ks)
    qo_spec = pl.BlockSpec((1, tq, LANES), lambda i, h, t: (i, t, h))
    kv_specs = [pl.BlockSpec((1, k.shape[1], LANES), lambda i, h, t: (i, 0, h)) for k in ks]
    return pl.pallas_call(
        functools.partial(_attn_kernel, n_seg=n_seg, lambda_init=lambda_init, sub_rows=min(tq, MXU_DIM)),
        grid=(b, N_HEADS, lq // tq),
        in_specs=[pl.BlockSpec(memory_space=pltpu.SMEM), qo_spec] + kv_specs + kv_specs
                 + [_resident((4, HEAD_DIM), layer), _resident((1, V_HEAD_DIM), layer)],
        out_specs=qo_spec,
        out_shape=jax.ShapeDtypeStruct((b, lq, ATTN_WIDTH), BF16),
        compiler_params=_params("arbitrary", "arbitrary", "arbitrary"),
        name=name,
    )(bound, q, *ks, *vs, lam_rows, subln_g)


def _fourier_kernel(y_ref, ce_ref, se_ref, co_ref, so_ref, wf_ref, o_ref):
    w = FOURIER_WIDTH
    e = (jnp.dot(ce_ref[...], y_ref[0, :, 0:w], preferred_element_type=F32)
         + jnp.dot(se_ref[...], y_ref[0, :, w:2 * w], preferred_element_type=F32))
    o = (jnp.dot(co_ref[...], y_ref[0, :, 2 * w:3 * w], preferred_element_type=F32)
         + jnp.dot(so_ref[...], y_ref[0, :, 3 * w:4 * w], preferred_element_type=F32))
    top = (e + o).astype(BF16)
    bot = (e - o).astype(BF16)
    for g in range(N_FGROUPS):
        c0 = FGROUP_DIM * g
        wg = wf_ref[g]
        o_ref[0, 0, :, c0:c0 + FGROUP_DIM] = jnp.dot(
            top[:, c0:c0 + FGROUP_DIM], wg, preferred_element_type=F32).astype(BF16)
        o_ref[0, 1, :, c0:c0 + FGROUP_DIM] = jnp.dot(
            bot[:, c0:c0 + FGROUP_DIM], wg, preferred_element_type=F32).astype(BF16)


def _fourier(y2, pos_mats, w_f, *, layer, name):
    b, half, _ = y2.shape
    out = pl.pallas_call(
        _fourier_kernel,
        grid=(b,),
        in_specs=[pl.BlockSpec((1, half, 4 * FOURIER_WIDTH), lambda i: (i, 0, 0))]
                 + [_resident((half, half))] * 4
                 + [_resident((N_FGROUPS, FGROUP_DIM, FGROUP_DIM), layer)],
        out_specs=pl.BlockSpec((1, 2, half, FOURIER_WIDTH), lambda i: (i, 0, 0, 0)),
        out_shape=jax.ShapeDtypeStruct((b, 2, half, FOURIER_WIDTH), BF16),
        compiler_params=_params("arbitrary"),
        name=name,
    )(y2, *pos_mats, w_f)
    return out.reshape(b, 2 * half, FOURIER_WIDTH)


def _ffn_kernel(x_ref, a_ref, f_ref, mod_ref, g2_ref, wo_ref, wg_ref, wu_ref, wd_ref, o_ref):
    x = x_ref[0]
    mix = (jnp.dot(a_ref[0], wo_ref[0:ATTN_WIDTH, :], preferred_element_type=F32)
           + jnp.dot(f_ref[0], wo_ref[ATTN_WIDTH:, :], preferred_element_type=F32))
    x1 = x + mod_ref[2:3, :] * mix
    ms = jnp.mean(x1 * x1, axis=-1, keepdims=True)
    h = x1 * lax.rsqrt(ms + EPS) * g2_ref[...]
    h = (h * (1.0 + mod_ref[4:5, :]) + mod_ref[3:4, :]).astype(BF16)
    gate = jnp.dot(h, wg_ref[...], preferred_element_type=F32)
    up = jnp.dot(h, wu_ref[...], preferred_element_type=F32)
    act = (gate / (1.0 + jnp.exp(-gate)) * up).astype(BF16)
    y = jnp.dot(act, wd_ref[...], preferred_element_type=F32)
    o_ref[0] = x1 + mod_ref[5:6, :] * y


def _out_ffn(x, attn, four, mod_all, g2, w_out, w_gate, w_up, w_down, *, layer, is_ctx, tm):
    b, l, d = x.shape
    d_ff = w_gate.shape[-1]
    row_of = (lambda i, t: b) if is_ctx else (lambda i, t: i)
    tok = lambda width: pl.BlockSpec((1, tm, width), lambda i, t: (i, t, 0))
    return pl.pallas_call(
        _ffn_kernel,
        grid=(b, l // tm),
        in_specs=[tok(d), tok(ATTN_WIDTH), tok(FOURIER_WIDTH),
                  _mod_spec(layer, row_of),
                  _resident((1, d), layer),
                  _resident((d, d), layer),
                  _resident((d, d_ff), layer),
                  _resident((d, d_ff), layer),
                  _resident((d_ff, d), layer)],
        out_specs=tok(d),
        out_shape=jax.ShapeDtypeStruct((b, l, d), F32),
        compiler_params=_params("arbitrary", "arbitrary"),
        name="out_ffn_ctx" if is_ctx else "out_ffn_lat",
    )(x, attn, four, mod_all, g2, w_out, w_gate, w_up, w_down)


def kernel(x, c, ctx, c_ctx, w_ada, b_ada, norm1_g, norm2_g, w_in, q_norm_g, k_norm_g, lambda_q1, lambda_k1, lambda_q2, lambda_k2, subln_g, w_fourier, w_out, w_gate, w_up, w_down):
    depth = w_ada.shape[0]
    b, l, d = x.shape
    lc = ctx.shape[1]
    assert b + 1 <= MOD_ROWS

    rope_tabs = _rope_tables(l)
    cdft = _channel_dft()
    pos_lat = _position_dft(l)
    pos_ctx = _position_dft(lc)
    ones = _segment_ones()

    cvec = jnp.zeros((MOD_ROWS, d), F32).at[:b].set(c).at[b].set(c_ctx)
    mod_all = _adaln_mod(cvec, w_ada, b_ada).reshape(depth, MOD_ROWS, 6, d)

    w_in_b, w_out_b = w_in.astype(BF16), w_out.astype(BF16)
    w_gate_b, w_up_b, w_down_b = w_gate.astype(BF16), w_up.astype(BF16), w_down.astype(BF16)
    w_f_b = w_fourier.astype(BF16)
    g1, g2 = norm1_g.reshape(depth, 1, d), norm2_g.reshape(depth, 1, d)
    tile_gain = lambda g: jnp.tile(g, (1, MXU_DIM // HEAD_DIM)).reshape(depth, 1, MXU_DIM)
    qg, kg = tile_gain(q_norm_g), tile_gain(k_norm_g)
    lam_rows = jnp.stack([lambda_q1, lambda_k1, lambda_q2, lambda_k2], axis=1)
    sg = subln_g.reshape(depth, 1, V_HEAD_DIM)
    score_bound = (HEAD_DIM ** 0.5 * LOG2_E) * jnp.max(jnp.abs(q_norm_g), axis=1) * jnp.max(jnp.abs(k_norm_g), axis=1)

    for i in range(depth):
        last = i == depth - 1
        lambda_init = 0.8 - 0.6 * math.exp(-0.3 * i)
        bound = score_bound[i].reshape(1, 1)
        proj = functools.partial(_project, mod_all=mod_all, g1=g1, w_in=w_in_b, qg=qg, kg=kg,
                                 ones=ones, cdft=cdft, layer=i)
        attend = functools.partial(_attention, bound, lam_rows=lam_rows, subln_g=sg, layer=i,
                                   lambda_init=lambda_init)
        ffn = functools.partial(_out_ffn, mod_all=mod_all, g2=g2, w_out=w_out_b, w_gate=w_gate_b,
                                w_up=w_up_b, w_down=w_down_b, layer=i)

        qc, kc, vc, yc = proj(ctx, rope_tabs=None, is_ctx=True, tm=lc)
        qx, kx, vx, yx = proj(x, rope_tabs=rope_tabs, is_ctx=False, tm=512)

        attn_x = attend(qx, [kc, kx], [vc, vx], tq=l, name="attn_lat")
        four_x = _fourier(yx, pos_lat, w_f_b, layer=i, name="fourier_lat")
        x_new = ffn(x, attn_x, four_x, is_ctx=False, tm=512)
        if not last:
            attn_c = attend(qc, [kc], [vc], tq=lc, name="attn_ctx")
            four_c = _fourier(yc, pos_ctx, w_f_b, layer=i, name="fourier_ctx")
            ctx = ffn(ctx, attn_c, four_c, is_ctx=True, tm=lc)
        x = x_new
    return x
```

```python
import functools
import math

import numpy as np
import jax
import jax.numpy as jnp
from jax import lax
from jax.experimental import pallas as pl
from jax.experimental.pallas import tpu as pltpu

D_MODEL = 1024
GRID_W = 64
ATTN_WIDTH = 512
FOURIER_WIDTH = 512
HEAD_DIM = 64
N_HEADS = 4
V_HEAD_DIM = 128
N_FGROUPS = 4
FGROUP_DIM = 128
IN_COLS = 2048
ROPE_AXIS_DIM = 32
ROPE_THETA = 10000.0
EPS = 1e-6

LANES = 128
MXU_DIM = 256
MOD_ROWS = 16
VMEM_LIMIT_BYTES = 56 * 1024 * 1024
SAFE_SCORE_BOUND = 40.0
LOG2_E = 1.4426950408889634

F32 = jnp.float32
BF16 = jnp.bfloat16
_NT = (((1,), (1,)), ((), ()))


def _params(*sem):
    return pltpu.CompilerParams(dimension_semantics=sem, vmem_limit_bytes=VMEM_LIMIT_BYTES)


def _resident(shape, layer=None):
    zeros = (0,) * len(shape)
    if layer is None:
        return pl.BlockSpec(shape, lambda *_: zeros, pipeline_mode=pl.Buffered(1))
    return pl.BlockSpec((None,) + tuple(shape), lambda *_: (layer,) + zeros, pipeline_mode=pl.Buffered(1))


def _mod_spec(layer, row_of):
    return pl.BlockSpec((None, None, 6, D_MODEL), lambda *g: (layer, row_of(*g), 0, 0))


def _rope_tables(n_tokens):
    rows = n_tokens // GRID_W
    row = np.repeat(np.arange(rows, dtype=np.float64), GRID_W)
    col = np.tile(np.arange(GRID_W, dtype=np.float64), rows)
    inv = np.float32(ROPE_THETA) ** (-np.arange(0, ROPE_AXIS_DIM, 2, dtype=np.float32) / np.float32(ROPE_AXIS_DIM))
    inv = inv.astype(np.float64)
    half = ROPE_AXIS_DIM // 2
    d = np.arange(HEAD_DIM)
    pos = np.where(d[None, :] < ROPE_AXIS_DIM, row[:, None], col[:, None])
    ang = pos * inv[d % half][None, :]
    low = (d % ROPE_AXIS_DIM) < half
    cos = np.cos(ang)
    sin = np.where(low[None, :], -np.sin(ang), np.sin(ang))
    tile2 = lambda a: jnp.asarray(np.tile(a, (1, LANES // HEAD_DIM)), F32)
    return tile2(cos), tile2(sin)


def _channel_dft():
    n = FGROUP_DIM
    k = (np.arange(n)[:, None] * np.arange(n)[None, :]) % n
    ang = 2.0 * np.pi * k / n
    m = np.concatenate([np.cos(ang), np.sin(ang)], axis=1) / np.sqrt(n)
    return jnp.asarray(m, F32).astype(BF16)


def _position_dft(n_tokens):
    half = n_tokens // 2
    lo = np.arange(half)[:, None]
    m = np.arange(half)[None, :]
    mats = []
    for s in (0, 1):
        k = ((2 * m + s) * lo) % n_tokens
        ang = 2.0 * np.pi * k / n_tokens
        mats.append(jnp.asarray(np.cos(ang) / np.sqrt(n_tokens), F32).astype(BF16))
        mats.append(jnp.asarray(-np.sin(ang) / np.sqrt(n_tokens), F32).astype(BF16))
    return mats


def _segment_ones():
    seg = np.arange(MXU_DIM) // HEAD_DIM
    return jnp.asarray(seg[:, None] == seg[None, :], BF16)


def _mod_kernel(c_ref, w_ref, b_ref, o_ref):
    cv = c_ref[...]
    s = cv / (1.0 + jnp.exp(-cv))
    o_ref[0] = jnp.dot(s.astype(BF16), w_ref[0].astype(BF16), preferred_element_type=F32) + b_ref[0]


def _adaln_mod(cvec, w_ada, b_ada):
    depth, d, n = w_ada.shape
    tn = 1536
    return pl.pallas_call(
        _mod_kernel,
        grid=(depth, n // tn),
        in_specs=[pl.BlockSpec((MOD_ROWS, d), lambda i, j: (0, 0)),
                  pl.BlockSpec((1, d, tn), lambda i, j: (i, 0, j)),
                  pl.BlockSpec((1, 1, tn), lambda i, j: (i, 0, j))],
        out_specs=pl.BlockSpec((1, MOD_ROWS, tn), lambda i, j: (i, 0, j)),
        out_shape=jax.ShapeDtypeStruct((depth, MOD_ROWS, n), F32),
        compiler_params=_params("arbitrary", "arbitrary"),
        name="adaln_mod",
    )(cvec, w_ada, b_ada.reshape(depth, 1, n))


def _proj_kernel(*refs, rope):
    if rope:
        (x_ref, mod_ref, g1_ref, w_ref, qg_ref, kg_ref, ones_ref, cdft_ref,
         cos_ref, sin_ref, q_ref, k_ref, v_ref, y_ref, f_scr) = refs
    else:
        (x_ref, mod_ref, g1_ref, w_ref, qg_ref, kg_ref, ones_ref, cdft_ref,
         q_ref, k_ref, v_ref, y_ref, f_scr) = refs
    tm = x_ref.shape[1]
    sub = f_scr.shape[1]
    swap_idx = lax.broadcasted_iota(jnp.int32, (sub, LANES), 1) ^ (ROPE_AXIS_DIM // 2)

    def project(r0):
        x = x_ref[0, r0:r0 + sub, :]
        ms = jnp.mean(x * x, axis=-1, keepdims=True)
        h = x * lax.rsqrt(ms + EPS) * g1_ref[...]
        h = h * (1.0 + mod_ref[1:2, :]) + mod_ref[0:1, :]
        return jnp.dot(h.astype(BF16), w_ref[...], preferred_element_type=F32)

    def norm_rope(t, gain, scale, out_ref, r0):
        for cb in range(ATTN_WIDTH // MXU_DIM):
            tc = t[:, MXU_DIM * cb:MXU_DIM * (cb + 1)]
            ss = jnp.dot((tc * tc).astype(BF16), ones_ref[...], preferred_element_type=F32)
            tn = tc * lax.rsqrt(ss * (1.0 / HEAD_DIM) + EPS) * gain
            for hb in range(MXU_DIM // LANES):
                u = tn[:, LANES * hb:LANES * (hb + 1)]
                if rope:
                    partner = jnp.take_along_axis(u, swap_idx, axis=1)
                    u = u * cos_ref[r0:r0 + sub, :] + partner * sin_ref[r0:r0 + sub, :]
                c0 = MXU_DIM * cb + LANES * hb
                out_ref[0, r0:r0 + sub, c0:c0 + LANES] = (u * scale).astype(BF16)

    def emit(r0, z):
        norm_rope(z[:, 0:ATTN_WIDTH], qg_ref[...], HEAD_DIM ** -0.5 * LOG2_E, q_ref, r0)
        norm_rope(z[:, ATTN_WIDTH:2 * ATTN_WIDTH], kg_ref[...], 1.0, k_ref, r0)
        v_ref[0, r0:r0 + sub, :] = z[:, 2 * ATTN_WIDTH:3 * ATTN_WIDTH].astype(BF16)
        w = FOURIER_WIDTH
        h0 = r0 // 2
        for g in range(N_FGROUPS):
            c0 = FGROUP_DIM * g
            f_scr[g] = z[:, 3 * ATTN_WIDTH + c0:3 * ATTN_WIDTH + c0 + FGROUP_DIM]
            for parity in range(2):
                fp = f_scr[g, pl.ds(parity, sub // 2, stride=2), :].astype(BF16)
                yg = jnp.dot(fp, cdft_ref[g], preferred_element_type=F32)
                base = 2 * w * parity
                y_ref[0, h0:h0 + sub // 2, base + c0:base + c0 + FGROUP_DIM] = (
                    yg[:, :FGROUP_DIM].astype(BF16))
                y_ref[0, h0:h0 + sub // 2, base + w + c0:base + w + c0 + FGROUP_DIM] = (
                    yg[:, FGROUP_DIM:].astype(BF16))

    starts = list(range(0, tm, sub))
    z = project(starts[0])
    for t, r0 in enumerate(starts):
        z_next = project(starts[t + 1]) if t + 1 < len(starts) else None
        emit(r0, z)
        z = z_next


def _project(x, mod_all, g1, w_in, qg, kg, ones, cdft, rope_tabs, *, layer, is_ctx, tm):
    b, l, d = x.shape
    rope = rope_tabs is not None
    row_of = (lambda i, t: b) if is_ctx else (lambda i, t: i)
    tok = lambda width: pl.BlockSpec((1, tm, width), lambda i, t: (i, t, 0))
    in_specs = [tok(d),
                _mod_spec(layer, row_of),
                _resident((1, d), layer),
                _resident((d, IN_COLS), layer),
                _resident((1, MXU_DIM), layer),
                _resident((1, MXU_DIM), layer),
                _resident((MXU_DIM, MXU_DIM)),
                _resident((N_FGROUPS, FGROUP_DIM, 2 * FGROUP_DIM), layer)]
    args = [x, mod_all, g1, w_in, qg, kg, ones, cdft]
    if rope:
        in_specs += [pl.BlockSpec((tm, LANES), lambda i, t: (t, 0))] * 2
        args += list(rope_tabs)
    out = lambda width: jax.ShapeDtypeStruct((b, l, width), BF16)
    return pl.pallas_call(
        functools.partial(_proj_kernel, rope=rope),
        grid=(b, l // tm),
        in_specs=in_specs,
        out_specs=[tok(ATTN_WIDTH), tok(ATTN_WIDTH), tok(ATTN_WIDTH),
                   pl.BlockSpec((1, tm // 2, 4 * FOURIER_WIDTH), lambda i, t: (i, t, 0))],
        out_shape=[out(ATTN_WIDTH), out(ATTN_WIDTH), out(ATTN_WIDTH),
                   jax.ShapeDtypeStruct((b, l // 2, 4 * FOURIER_WIDTH), BF16)],
        scratch_shapes=[pltpu.VMEM((N_FGROUPS, min(tm, MXU_DIM), FGROUP_DIM), F32)],
        compiler_params=_params("arbitrary", "arbitrary"),
        name="proj_ctx" if is_ctx else "proj_lat",
    )(*args)


def _attn_kernel(*refs, n_seg, lambda_init, sub_rows):
    bound_ref, q_ref = refs[0], refs[1]
    k_refs = refs[2:2 + n_seg]
    v_refs = refs[2 + n_seg:2 + 2 * n_seg]
    lam_ref, sg_ref, o_ref = refs[2 + 2 * n_seg:]

    def attend(bounded):
        lv = lam_ref[...]
        lam = (jnp.exp(jnp.sum(lv[0:1] * lv[1:2], axis=-1, keepdims=True))
               - jnp.exp(jnp.sum(lv[2:3] * lv[3:4], axis=-1, keepdims=True)) + lambda_init)
        lane = lax.broadcasted_iota(jnp.int32, (1, LANES), 1)
        first = (lane < HEAD_DIM).astype(BF16)
        items = [(pl.ds(LANES * h, LANES), pl.ds(r0, sub_rows))
                 for h in range(q_ref.shape[2] // LANES) for r0 in range(0, q_ref.shape[1], sub_rows)]
        probs = scores(items[0], first, bounded)
        for t, item in enumerate(items):
            nxt = scores(items[t + 1], first, bounded) if t + 1 < len(items) else None
            finish(item, probs, lam)
            probs = nxt

    def scores(item, first, bounded):
        cols, rows = item
        q = q_ref[0, rows, cols]

        def one_map(qm):
            ss = [lax.dot_general(qm, k[0, :, cols], _NT, preferred_element_type=F32) for k in k_refs]
            if bounded:
                ps = [jnp.exp2(s) for s in ss]
            else:
                m = functools.reduce(jnp.maximum, [jnp.max(s, axis=-1, keepdims=True) for s in ss])
                ps = [jnp.exp2(s - m) for s in ss]
            l = functools.reduce(lambda a, c: a + c, [jnp.sum(p, axis=-1, keepdims=True) for p in ps])
            return [p.astype(BF16) for p in ps], l

        return one_map(q * first), one_map(q * (1.0 - first).astype(BF16))

    def finish(item, probs, lam):
        cols, rows = item
        (p1, l1), (p2, l2) = probs
        c = (lam * l1 / l2).astype(BF16)
        o = functools.reduce(lambda a, b: a + b, [
            jnp.dot(pa - c * pb, v[0, :, cols], preferred_element_type=F32)
            for pa, pb, v in zip(p1, p2, v_refs)])
        o = o * (1.0 / l1)
        ms = jnp.mean(o * o, axis=-1, keepdims=True)
        o = o * lax.rsqrt(ms + EPS) * sg_ref[...] * (1.0 - lambda_init)
        o_ref[0, rows, cols] = o.astype(BF16)

    safe = bound_ref[0, 0] <= SAFE_SCORE_BOUND
    pl.when(safe)(lambda: attend(True))
    pl.when(jnp.logical_not(safe))(lambda: attend(False))


def _attention(bound, q, ks, vs, lam_rows, subln_g, *, layer, lambda_init, tq, heads, name):
    b, lq, _ = q.shape
    n_seg = len(ks)
    width = heads * LANES
    qo_spec = pl.BlockSpec((1, tq, width), lambda i, h, t: (i, t, h))
    kv_specs = [pl.BlockSpec((1, k.shape[1], width), lambda i, h, t: (i, 0, h)) for k in ks]
    return pl.pallas_call(
        functools.partial(_attn_kernel, n_seg=n_seg, lambda_init=lambda_init, sub_rows=min(tq, MXU_DIM)),
        grid=(b, N_HEADS // heads, lq // tq),
        in_specs=[pl.BlockSpec(memory_space=pltpu.SMEM), qo_spec] + kv_specs + kv_specs
                 + [_resident((4, HEAD_DIM), layer), _resident((1, V_HEAD_DIM), layer)],
        out_specs=qo_spec,
        out_shape=jax.ShapeDtypeStruct((b, lq, ATTN_WIDTH), BF16),
        compiler_params=_params("arbitrary", "arbitrary", "arbitrary"),
        name=name,
    )(bound, q, *ks, *vs, lam_rows, subln_g)


def _fold_kernel(cdft_ref, wf_ref, o_ref):
    for g in range(N_FGROUPS):
        wg = wf_ref[g].astype(BF16)
        for part in range(2):
            cols = pl.ds(FGROUP_DIM * part, FGROUP_DIM)
            o_ref[g, :, cols] = jnp.dot(cdft_ref[:, cols], wg, preferred_element_type=F32).astype(BF16)


def _fold_group_map(cdft, w_fourier):
    depth = w_fourier.shape[0]
    return pl.pallas_call(
        _fold_kernel,
        grid=(depth,),
        in_specs=[_resident((FGROUP_DIM, 2 * FGROUP_DIM)),
                  pl.BlockSpec((None, N_FGROUPS, FGROUP_DIM, FGROUP_DIM), lambda i: (i, 0, 0, 0))],
        out_specs=pl.BlockSpec((None, N_FGROUPS, FGROUP_DIM, 2 * FGROUP_DIM), lambda i: (i, 0, 0, 0)),
        out_shape=jax.ShapeDtypeStruct((depth, N_FGROUPS, FGROUP_DIM, 2 * FGROUP_DIM), BF16),
        compiler_params=_params("arbitrary"),
        name="fold_group_map",
    )(cdft, w_fourier)


def _fourier_kernel(y_ref, ce_ref, se_ref, co_ref, so_ref, o_ref):
    w = FOURIER_WIDTH
    e = (jnp.dot(ce_ref[...], y_ref[0, :, 0:w], preferred_element_type=F32)
         + jnp.dot(se_ref[...], y_ref[0, :, w:2 * w], preferred_element_type=F32))
    o = (jnp.dot(co_ref[...], y_ref[0, :, 2 * w:3 * w], preferred_element_type=F32)
         + jnp.dot(so_ref[...], y_ref[0, :, 3 * w:4 * w], preferred_element_type=F32))
    o_ref[0, 0] = (e + o).astype(BF16)
    o_ref[0, 1] = (e - o).astype(BF16)


def _fourier(y2, pos_mats, *, name):
    b, half, _ = y2.shape
    out = pl.pallas_call(
        _fourier_kernel,
        grid=(b,),
        in_specs=[pl.BlockSpec((1, half, 4 * FOURIER_WIDTH), lambda i: (i, 0, 0))]
                 + [_resident((half, half))] * 4,
        out_specs=pl.BlockSpec((1, 2, half, FOURIER_WIDTH), lambda i: (i, 0, 0, 0)),
        out_shape=jax.ShapeDtypeStruct((b, 2, half, FOURIER_WIDTH), BF16),
        compiler_params=_params("arbitrary"),
        name=name,
    )(y2, *pos_mats)
    return out.reshape(b, 2 * half, FOURIER_WIDTH)


def _ffn_kernel(x_ref, a_ref, f_ref, mod_ref, g2_ref, wo_ref, wg_ref, wu_ref, wd_ref, o_ref):
    x = x_ref[0]
    mix = (jnp.dot(a_ref[0], wo_ref[0:ATTN_WIDTH, :], preferred_element_type=F32)
           + jnp.dot(f_ref[0], wo_ref[ATTN_WIDTH:, :], preferred_element_type=F32))
    x1 = x + mod_ref[2:3, :] * mix
    ms = jnp.mean(x1 * x1, axis=-1, keepdims=True)
    h = x1 * lax.rsqrt(ms + EPS) * g2_ref[...]
    h = (h * (1.0 + mod_ref[4:5, :]) + mod_ref[3:4, :]).astype(BF16)
    gate = jnp.dot(h, wg_ref[...], preferred_element_type=F32)
    up = jnp.dot(h, wu_ref[...], preferred_element_type=F32)
    act = (gate / (1.0 + jnp.exp(-gate)) * up).astype(BF16)
    y = jnp.dot(act, wd_ref[...], preferred_element_type=F32)
    o_ref[0] = x1 + mod_ref[5:6, :] * y


def _out_ffn(x, attn, four, mod_all, g2, w_out, w_gate, w_up, w_down, *, layer, is_ctx, tm):
    b, l, d = x.shape
    d_ff = w_gate.shape[-1]
    row_of = (lambda i, t: b) if is_ctx else (lambda i, t: i)
    tok = lambda width: pl.BlockSpec((1, tm, width), lambda i, t: (i, t, 0))
    return pl.pallas_call(
        _ffn_kernel,
        grid=(b, l // tm),
        in_specs=[tok(d), tok(ATTN_WIDTH), tok(FOURIER_WIDTH),
                  _mod_spec(layer, row_of),
                  _resident((1, d), layer),
                  _resident((d, d), layer),
                  _resident((d, d_ff), layer),
                  _resident((d, d_ff), layer),
                  _resident((d_ff, d), layer)],
        out_specs=tok(d),
        out_shape=jax.ShapeDtypeStruct((b, l, d), F32),
        compiler_params=_params("arbitrary", "arbitrary"),
        name="out_ffn_ctx" if is_ctx else "out_ffn_lat",
    )(x, attn, four, mod_all, g2, w_out, w_gate, w_up, w_down)


def kernel(x, c, ctx, c_ctx, w_ada, b_ada, norm1_g, norm2_g, w_in, q_norm_g, k_norm_g, lambda_q1, lambda_k1, lambda_q2, lambda_k2, subln_g, w_fourier, w_out, w_gate, w_up, w_down):
    depth = w_ada.shape[0]
    b, l, d = x.shape
    lc = ctx.shape[1]
    assert b + 1 <= MOD_ROWS

    rope_tabs = _rope_tables(l)
    cdft = _fold_group_map(_channel_dft(), w_fourier)
    pos_lat = _position_dft(l)
    pos_ctx = _position_dft(lc)
    ones = _segment_ones()

    cvec = jnp.zeros((MOD_ROWS, d), F32).at[:b].set(c).at[b].set(c_ctx)
    mod_all = _adaln_mod(cvec, w_ada, b_ada).reshape(depth, MOD_ROWS, 6, d)

    w_in_b, w_out_b = w_in.astype(BF16), w_out.astype(BF16)
    w_gate_b, w_up_b, w_down_b = w_gate.astype(BF16), w_up.astype(BF16), w_down.astype(BF16)
    g1, g2 = norm1_g.reshape(depth, 1, d), norm2_g.reshape(depth, 1, d)
    tile_gain = lambda g: jnp.tile(g, (1, MXU_DIM // HEAD_DIM)).reshape(depth, 1, MXU_DIM)
    qg, kg = tile_gain(q_norm_g), tile_gain(k_norm_g)
    lam_rows = jnp.stack([lambda_q1, lambda_k1, lambda_q2, lambda_k2], axis=1)
    sg = subln_g.reshape(depth, 1, V_HEAD_DIM)
    score_bound = (HEAD_DIM ** 0.5 * LOG2_E) * jnp.max(jnp.abs(q_norm_g), axis=1) * jnp.max(jnp.abs(k_norm_g), axis=1)

    for i in range(depth):
        last = i == depth - 1
        lambda_init = 0.8 - 0.6 * math.exp(-0.3 * i)
        bound = score_bound[i].reshape(1, 1)
        proj = functools.partial(_project, mod_all=mod_all, g1=g1, w_in=w_in_b, qg=qg, kg=kg,
                                 ones=ones, cdft=cdft, layer=i)
        attend = functools.partial(_attention, bound, lam_rows=lam_rows, subln_g=sg, layer=i,
                                   lambda_init=lambda_init)
        ffn = functools.partial(_out_ffn, mod_all=mod_all, g2=g2, w_out=w_out_b, w_gate=w_gate_b,
                                w_up=w_up_b, w_down=w_down_b, layer=i)

        qc, kc, vc, yc = proj(ctx, rope_tabs=None, is_ctx=True, tm=lc)
        qx, kx, vx, yx = proj(x, rope_tabs=rope_tabs, is_ctx=False, tm=1024)

        attn_x = attend(qx, [kc, kx], [vc, vx], tq=l, heads=1, name="attn_lat")
        four_x = _fourier(yx, pos_lat, name="fourier_lat")
        x_new = ffn(x, attn_x, four_x, is_ctx=False, tm=512)
        if not last:
            attn_c = attend(qc, [kc], [vc], tq=lc, heads=N_HEADS, name="attn_ctx")
            four_c = _fourier(yc, pos_ctx, name="fourier_ctx")
            ctx = ffn(ctx, attn_c, four_c, is_ctx=True, tm=lc)
        x = x_new
    return x
```

```python
import functools
import math

import numpy as np
import jax
import jax.numpy as jnp
from jax import lax
from jax.experimental import pallas as pl
from jax.experimental.pallas import tpu as pltpu

D_MODEL = 1024
GRID_W = 64
ATTN_WIDTH = 512
FOURIER_WIDTH = 512
HEAD_DIM = 64
N_HEADS = 4
V_HEAD_DIM = 128
N_FGROUPS = 4
FGROUP_DIM = 128
IN_COLS = 2048
ROPE_AXIS_DIM = 32
ROPE_THETA = 10000.0
EPS = 1e-6

LANES = 128
MXU_DIM = 256
MOD_ROWS = 16
VMEM_LIMIT_BYTES = 56 * 1024 * 1024
SAFE_SCORE_BOUND = 40.0
LOG2_E = 1.4426950408889634

F32 = jnp.float32
BF16 = jnp.bfloat16
_NT = (((1,), (1,)), ((), ()))


def _params(*sem):
    return pltpu.CompilerParams(dimension_semantics=sem, vmem_limit_bytes=VMEM_LIMIT_BYTES)


def _resident(shape, layer=None):
    zeros = (0,) * len(shape)
    if layer is None:
        return pl.BlockSpec(shape, lambda *_: zeros, pipeline_mode=pl.Buffered(1))
    return pl.BlockSpec((None,) + tuple(shape), lambda *_: (layer,) + zeros, pipeline_mode=pl.Buffered(1))


def _mod_spec(layer, row_of):
    return pl.BlockSpec((None, None, 6, D_MODEL), lambda *g: (layer, row_of(*g), 0, 0))


def _rope_tables(n_tokens):
    rows = n_tokens // GRID_W
    row = np.repeat(np.arange(rows, dtype=np.float64), GRID_W)
    col = np.tile(np.arange(GRID_W, dtype=np.float64), rows)
    inv = np.float32(ROPE_THETA) ** (-np.arange(0, ROPE_AXIS_DIM, 2, dtype=np.float32) / np.float32(ROPE_AXIS_DIM))
    inv = inv.astype(np.float64)
    half = ROPE_AXIS_DIM // 2
    d = np.arange(HEAD_DIM)
    pos = np.where(d[None, :] < ROPE_AXIS_DIM, row[:, None], col[:, None])
    ang = pos * inv[d % half][None, :]
    low = (d % ROPE_AXIS_DIM) < half
    cos = np.cos(ang)
    sin = np.where(low[None, :], -np.sin(ang), np.sin(ang))
    tile2 = lambda a: jnp.asarray(np.tile(a, (1, LANES // HEAD_DIM)), F32)
    return tile2(cos), tile2(sin)


def _channel_dft():
    n = FGROUP_DIM
    k = (np.arange(n)[:, None] * np.arange(n)[None, :]) % n
    ang = 2.0 * np.pi * k / n
    m = np.concatenate([np.cos(ang), np.sin(ang)], axis=1) / np.sqrt(n)
    return jnp.asarray(m, F32).astype(BF16)


def _position_dft(n_tokens):
    half = n_tokens // 2
    lo = np.arange(half)[:, None]
    m = np.arange(half)[None, :]
    mats = []
    for s in (0, 1):
        k = ((2 * m + s) * lo) % n_tokens
        ang = 2.0 * np.pi * k / n_tokens
        mats.append(jnp.asarray(np.cos(ang) / np.sqrt(n_tokens), F32).astype(BF16))
        mats.append(jnp.asarray(-np.sin(ang) / np.sqrt(n_tokens), F32).astype(BF16))
    return mats


def _segment_ones():
    seg = np.arange(MXU_DIM) // HEAD_DIM
    return jnp.asarray(seg[:, None] == seg[None, :], BF16)


def _mod_kernel(c_ref, w_ref, b_ref, o_ref):
    cv = c_ref[...]
    s = cv / (1.0 + jnp.exp(-cv))
    o_ref[0] = jnp.dot(s.astype(BF16), w_ref[0].astype(BF16), preferred_element_type=F32) + b_ref[0]


def _adaln_mod(cvec, w_ada, b_ada):
    depth, d, n = w_ada.shape
    tn = 1536
    return pl.pallas_call(
        _mod_kernel,
        grid=(depth, n // tn),
        in_specs=[pl.BlockSpec((MOD_ROWS, d), lambda i, j: (0, 0)),
                  pl.BlockSpec((1, d, tn), lambda i, j: (i, 0, j)),
                  pl.BlockSpec((1, 1, tn), lambda i, j: (i, 0, j))],
        out_specs=pl.BlockSpec((1, MOD_ROWS, tn), lambda i, j: (i, 0, j)),
        out_shape=jax.ShapeDtypeStruct((depth, MOD_ROWS, n), F32),
        compiler_params=_params("arbitrary", "arbitrary"),
        name="adaln_mod",
    )(cvec, w_ada, b_ada.reshape(depth, 1, n))


def _proj_kernel(*refs, rope):
    if rope:
        (x_ref, mod_ref, g1_ref, w_ref, qg_ref, kg_ref, ones_ref, cdft_ref,
         cos_ref, sin_ref, q_ref, k_ref, v_ref, y_ref, f_scr) = refs
    else:
        (x_ref, mod_ref, g1_ref, w_ref, qg_ref, kg_ref, ones_ref, cdft_ref,
         q_ref, k_ref, v_ref, y_ref, f_scr) = refs
    tm = x_ref.shape[1]
    sub = f_scr.shape[1]
    swap_idx = lax.broadcasted_iota(jnp.int32, (sub, LANES), 1) ^ (ROPE_AXIS_DIM // 2)

    def project(r0):
        x = x_ref[0, r0:r0 + sub, :]
        ms = jnp.mean(x * x, axis=-1, keepdims=True)
        h = x * lax.rsqrt(ms + EPS) * g1_ref[...]
        h = h * (1.0 + mod_ref[1:2, :]) + mod_ref[0:1, :]
        return jnp.dot(h.astype(BF16), w_ref[...], preferred_element_type=F32)

    def norm_rope(t, gain, scale, out_ref, r0):
        for cb in range(ATTN_WIDTH // MXU_DIM):
            tc = t[:, MXU_DIM * cb:MXU_DIM * (cb + 1)]
            ss = jnp.dot((tc * tc).astype(BF16), ones_ref[...], preferred_element_type=F32)
            tn = tc * lax.rsqrt(ss * (1.0 / HEAD_DIM) + EPS) * gain
            for hb in range(MXU_DIM // LANES):
                u = tn[:, LANES * hb:LANES * (hb + 1)]
                if rope:
                    partner = jnp.take_along_axis(u, swap_idx, axis=1)
                    u = u * cos_ref[r0:r0 + sub, :] + partner * sin_ref[r0:r0 + sub, :]
                c0 = MXU_DIM * cb + LANES * hb
                out_ref[0, r0:r0 + sub, c0:c0 + LANES] = (u * scale).astype(BF16)

    def emit(r0, z):
        norm_rope(z[:, 0:ATTN_WIDTH], qg_ref[...], HEAD_DIM ** -0.5 * LOG2_E, q_ref, r0)
        norm_rope(z[:, ATTN_WIDTH:2 * ATTN_WIDTH], kg_ref[...], 1.0, k_ref, r0)
        v_ref[0, r0:r0 + sub, :] = z[:, 2 * ATTN_WIDTH:3 * ATTN_WIDTH].astype(BF16)
        w = FOURIER_WIDTH
        h0 = r0 // 2
        for g in range(N_FGROUPS):
            c0 = FGROUP_DIM * g
            f_scr[g] = z[:, 3 * ATTN_WIDTH + c0:3 * ATTN_WIDTH + c0 + FGROUP_DIM]
            for parity in range(2):
                fp = f_scr[g, pl.ds(parity, sub // 2, stride=2), :].astype(BF16)
                yg = jnp.dot(fp, cdft_ref[g], preferred_element_type=F32)
                base = 2 * w * parity
                y_ref[0, h0:h0 + sub // 2, base + c0:base + c0 + FGROUP_DIM] = (
                    yg[:, :FGROUP_DIM].astype(BF16))
                y_ref[0, h0:h0 + sub // 2, base + w + c0:base + w + c0 + FGROUP_DIM] = (
                    yg[:, FGROUP_DIM:].astype(BF16))

    starts = list(range(0, tm, sub))
    z = project(starts[0])
    for t, r0 in enumerate(starts):
        z_next = project(starts[t + 1]) if t + 1 < len(starts) else None
        emit(r0, z)
        z = z_next


def _project(x, mod_all, g1, w_in, qg, kg, ones, cdft, rope_tabs, *, layer, mod_row, tm, name):
    b, l, d = x.shape
    rope = rope_tabs is not None
    row_of = (lambda i, t: i) if mod_row is None else (lambda i, t: mod_row)
    tok = lambda width: pl.BlockSpec((1, tm, width), lambda i, t: (i, t, 0))
    in_specs = [tok(d),
                _mod_spec(layer, row_of),
                _resident((1, d), layer),
                _resident((d, IN_COLS), layer),
                _resident((1, MXU_DIM), layer),
                _resident((1, MXU_DIM), layer),
                _resident((MXU_DIM, MXU_DIM)),
                _resident((N_FGROUPS, FGROUP_DIM, 2 * FGROUP_DIM), layer)]
    args = [x, mod_all, g1, w_in, qg, kg, ones, cdft]
    if rope:
        in_specs += [pl.BlockSpec((tm, LANES), lambda i, t: (t, 0))] * 2
        args += list(rope_tabs)
    out = lambda width: jax.ShapeDtypeStruct((b, l, width), BF16)
    return pl.pallas_call(
        functools.partial(_proj_kernel, rope=rope),
        grid=(b, l // tm),
        in_specs=in_specs,
        out_specs=[tok(ATTN_WIDTH), tok(ATTN_WIDTH), tok(ATTN_WIDTH),
                   pl.BlockSpec((1, tm // 2, 4 * FOURIER_WIDTH), lambda i, t: (i, t, 0))],
        out_shape=[out(ATTN_WIDTH), out(ATTN_WIDTH), out(ATTN_WIDTH),
                   jax.ShapeDtypeStruct((b, l // 2, 4 * FOURIER_WIDTH), BF16)],
        scratch_shapes=[pltpu.VMEM((N_FGROUPS, min(tm, MXU_DIM), FGROUP_DIM), F32)],
        compiler_params=_params("arbitrary", "arbitrary"),
        name=name,
    )(*args)


def _attn_kernel(*refs, n_seg, lambda_init, sub_rows):
    bound_ref, q_ref = refs[0], refs[1]
    k_refs = refs[2:2 + n_seg]
    v_refs = refs[2 + n_seg:2 + 2 * n_seg]
    lam_ref, sg_ref, o_ref = refs[2 + 2 * n_seg:]

    def attend(bounded):
        lv = lam_ref[...]
        lam = (jnp.exp(jnp.sum(lv[0:1] * lv[1:2], axis=-1, keepdims=True))
               - jnp.exp(jnp.sum(lv[2:3] * lv[3:4], axis=-1, keepdims=True)) + lambda_init)
        lane = lax.broadcasted_iota(jnp.int32, (1, LANES), 1)
        first = (lane < HEAD_DIM).astype(BF16)
        items = [(pl.ds(LANES * h, LANES), pl.ds(r0, sub_rows))
                 for h in range(q_ref.shape[2] // LANES) for r0 in range(0, q_ref.shape[1], sub_rows)]
        probs = scores(items[0], first, bounded)
        for t, item in enumerate(items):
            nxt = scores(items[t + 1], first, bounded) if t + 1 < len(items) else None
            finish(item, probs, lam)
            probs = nxt

    def scores(item, first, bounded):
        cols, rows = item
        q = q_ref[0, rows, cols]

        def one_map(qm):
            ss = [lax.dot_general(qm, k[0, :, cols], _NT, preferred_element_type=F32) for k in k_refs]
            if bounded:
                ps = [jnp.exp2(s) for s in ss]
            else:
                m = functools.reduce(jnp.maximum, [jnp.max(s, axis=-1, keepdims=True) for s in ss])
                ps = [jnp.exp2(s - m) for s in ss]
            l = functools.reduce(lambda a, c: a + c, [jnp.sum(p, axis=-1, keepdims=True) for p in ps])
            return [p.astype(BF16) for p in ps], l

        return one_map(q * first), one_map(q * (1.0 - first).astype(BF16))

    def finish(item, probs, lam):
        cols, rows = item
        (p1, l1), (p2, l2) = probs
        c = (lam * l1 / l2).astype(BF16)
        o = functools.reduce(lambda a, b: a + b, [
            jnp.dot(pa - c * pb, v[0, :, cols], preferred_element_type=F32)
            for pa, pb, v in zip(p1, p2, v_refs)])
        o = o * (1.0 / l1)
        ms = jnp.mean(o * o, axis=-1, keepdims=True)
        o = o * lax.rsqrt(ms + EPS) * sg_ref[...] * (1.0 - lambda_init)
        o_ref[0, rows, cols] = o.astype(BF16)

    safe = bound_ref[0, 0] <= SAFE_SCORE_BOUND
    pl.when(safe)(lambda: attend(True))
    pl.when(jnp.logical_not(safe))(lambda: attend(False))


def _attention(bound, q, ks, vs, lam_rows, subln_g, *, layer, lambda_init, tq, heads, name):
    b, lq, _ = q.shape
    n_seg = len(ks)
    width = heads * LANES
    qo_spec = pl.BlockSpec((1, tq, width), lambda i, h, t: (i, t, h))
    kv_specs = [pl.BlockSpec((1, k.shape[1], width), lambda i, h, t: (i, 0, h)) for k in ks]
    return pl.pallas_call(
        functools.partial(_attn_kernel, n_seg=n_seg, lambda_init=lambda_init, sub_rows=min(tq, MXU_DIM)),
        grid=(b, N_HEADS // heads, lq // tq),
        in_specs=[pl.BlockSpec(memory_space=pltpu.SMEM), qo_spec] + kv_specs + kv_specs
                 + [_resident((4, HEAD_DIM), layer), _resident((1, V_HEAD_DIM), layer)],
        out_specs=qo_spec,
        out_shape=jax.ShapeDtypeStruct((b, lq, ATTN_WIDTH), BF16),
        compiler_params=_params("arbitrary", "arbitrary", "arbitrary"),
        name=name,
    )(bound, q, *ks, *vs, lam_rows, subln_g)


def _fold_kernel(cdft_ref, wf_ref, o_ref):
    for g in range(N_FGROUPS):
        wg = wf_ref[g].astype(BF16)
        for part in range(2):
            cols = pl.ds(FGROUP_DIM * part, FGROUP_DIM)
            o_ref[g, :, cols] = jnp.dot(cdft_ref[:, cols], wg, preferred_element_type=F32).astype(BF16)


def _fold_group_map(cdft, w_fourier):
    depth = w_fourier.shape[0]
    return pl.pallas_call(
        _fold_kernel,
        grid=(depth,),
        in_specs=[_resident((FGROUP_DIM, 2 * FGROUP_DIM)),
                  pl.BlockSpec((None, N_FGROUPS, FGROUP_DIM, FGROUP_DIM), lambda i: (i, 0, 0, 0))],
        out_specs=pl.BlockSpec((None, N_FGROUPS, FGROUP_DIM, 2 * FGROUP_DIM), lambda i: (i, 0, 0, 0)),
        out_shape=jax.ShapeDtypeStruct((depth, N_FGROUPS, FGROUP_DIM, 2 * FGROUP_DIM), BF16),
        compiler_params=_params("arbitrary"),
        name="fold_group_map",
    )(cdft, w_fourier)


def _fourier_kernel(y_ref, ce_ref, se_ref, co_ref, so_ref, o_ref):
    w = FOURIER_WIDTH
    e = (jnp.dot(ce_ref[...], y_ref[0, :, 0:w], preferred_element_type=F32)
         + jnp.dot(se_ref[...], y_ref[0, :, w:2 * w], preferred_element_type=F32))
    o = (jnp.dot(co_ref[...], y_ref[0, :, 2 * w:3 * w], preferred_element_type=F32)
         + jnp.dot(so_ref[...], y_ref[0, :, 3 * w:4 * w], preferred_element_type=F32))
    o_ref[0, 0] = (e + o).astype(BF16)
    o_ref[0, 1] = (e - o).astype(BF16)


def _fourier(y2, pos_mats, *, name):
    b, half, _ = y2.shape
    out = pl.pallas_call(
        _fourier_kernel,
        grid=(b,),
        in_specs=[pl.BlockSpec((1, half, 4 * FOURIER_WIDTH), lambda i: (i, 0, 0))]
                 + [_resident((half, half))] * 4,
        out_specs=pl.BlockSpec((1, 2, half, FOURIER_WIDTH), lambda i: (i, 0, 0, 0)),
        out_shape=jax.ShapeDtypeStruct((b, 2, half, FOURIER_WIDTH), BF16),
        compiler_params=_params("arbitrary"),
        name=name,
    )(y2, *pos_mats)
    return out.reshape(b, 2 * half, FOURIER_WIDTH)


def _ffn_kernel(x_ref, a_ref, f_ref, mod_ref, g2_ref, wo_ref, wg_ref, wu_ref, wd_ref, o_ref, *, sub):
    tm = x_ref.shape[1]

    def mix_norm(r0):
        rows = pl.ds(r0, sub)
        mix = (jnp.dot(a_ref[0, rows, :], wo_ref[0:ATTN_WIDTH, :], preferred_element_type=F32)
               + jnp.dot(f_ref[0, rows, :], wo_ref[ATTN_WIDTH:, :], preferred_element_type=F32))
        x1 = x_ref[0, rows, :] + mod_ref[2:3, :] * mix
        ms = jnp.mean(x1 * x1, axis=-1, keepdims=True)
        h = x1 * lax.rsqrt(ms + EPS) * g2_ref[...]
        h = (h * (1.0 + mod_ref[4:5, :]) + mod_ref[3:4, :]).astype(BF16)
        return x1, h

    def swiglu(r0, x1, h):
        gate = jnp.dot(h, wg_ref[...], preferred_element_type=F32)
        up = jnp.dot(h, wu_ref[...], preferred_element_type=F32)
        act = (gate / (1.0 + jnp.exp(-gate)) * up).astype(BF16)
        y = jnp.dot(act, wd_ref[...], preferred_element_type=F32)
        o_ref[0, pl.ds(r0, sub), :] = x1 + mod_ref[5:6, :] * y

    starts = list(range(0, tm, sub))
    cur = mix_norm(starts[0])
    for t, r0 in enumerate(starts):
        nxt = mix_norm(starts[t + 1]) if t + 1 < len(starts) else None
        swiglu(r0, *cur)
        cur = nxt


def _out_ffn(x, attn, four, mod_all, g2, w_out, w_gate, w_up, w_down, *, layer, mod_row, tm, name):
    b, l, d = x.shape
    d_ff = w_gate.shape[-1]
    row_of = (lambda i, t: i) if mod_row is None else (lambda i, t: mod_row)
    tok = lambda width: pl.BlockSpec((1, tm, width), lambda i, t: (i, t, 0))
    return pl.pallas_call(
        functools.partial(_ffn_kernel, sub=min(tm, MXU_DIM)),
        grid=(b, l // tm),
        in_specs=[tok(d), tok(ATTN_WIDTH), tok(FOURIER_WIDTH),
                  _mod_spec(layer, row_of),
                  _resident((1, d), layer),
                  _resident((d, d), layer),
                  _resident((d, d_ff), layer),
                  _resident((d, d_ff), layer),
                  _resident((d_ff, d), layer)],
        out_specs=tok(d),
        out_shape=jax.ShapeDtypeStruct((b, l, d), F32),
        compiler_params=_params("arbitrary", "arbitrary"),
        name=name,
    )(x, attn, four, mod_all, g2, w_out, w_gate, w_up, w_down)


def kernel(x, c, ctx, c_ctx, w_ada, b_ada, norm1_g, norm2_g, w_in, q_norm_g, k_norm_g, lambda_q1, lambda_k1, lambda_q2, lambda_k2, subln_g, w_fourier, w_out, w_gate, w_up, w_down):
    depth = w_ada.shape[0]
    b, l, d = x.shape
    lc = ctx.shape[1]
    assert b + 1 <= MOD_ROWS and lc % 2 == 0
    tm_ctx = math.gcd(b * lc, 1024)

    rope_tabs = _rope_tables(l)
    cdft = _fold_group_map(_channel_dft(), w_fourier)
    pos_lat = _position_dft(l)
    pos_ctx = _position_dft(lc)
    ones = _segment_ones()

    cvec = jnp.zeros((MOD_ROWS, d), F32).at[:b].set(c).at[b].set(c_ctx)
    mod_all = _adaln_mod(cvec, w_ada, b_ada).reshape(depth, MOD_ROWS, 6, d)

    w_in_b, w_out_b = w_in.astype(BF16), w_out.astype(BF16)
    w_gate_b, w_up_b, w_down_b = w_gate.astype(BF16), w_up.astype(BF16), w_down.astype(BF16)
    g1, g2 = norm1_g.reshape(depth, 1, d), norm2_g.reshape(depth, 1, d)
    tile_gain = lambda g: jnp.tile(g, (1, MXU_DIM // HEAD_DIM)).reshape(depth, 1, MXU_DIM)
    qg, kg = tile_gain(q_norm_g), tile_gain(k_norm_g)
    lam_rows = jnp.stack([lambda_q1, lambda_k1, lambda_q2, lambda_k2], axis=1)
    sg = subln_g.reshape(depth, 1, V_HEAD_DIM)
    score_bound = (HEAD_DIM ** 0.5 * LOG2_E) * jnp.max(jnp.abs(q_norm_g), axis=1) * jnp.max(jnp.abs(k_norm_g), axis=1)

    for i in range(depth):
        last = i == depth - 1
        lambda_init = 0.8 - 0.6 * math.exp(-0.3 * i)
        bound = score_bound[i].reshape(1, 1)
        proj = functools.partial(_project, mod_all=mod_all, g1=g1, w_in=w_in_b, qg=qg, kg=kg,
                                 ones=ones, cdft=cdft, layer=i)
        attend = functools.partial(_attention, bound, lam_rows=lam_rows, subln_g=sg, layer=i,
                                   lambda_init=lambda_init)
        ffn = functools.partial(_out_ffn, mod_all=mod_all, g2=g2, w_out=w_out_b, w_gate=w_gate_b,
                                w_up=w_up_b, w_down=w_down_b, layer=i)

        flat = lambda a: a.reshape(1, b * lc, a.shape[-1])
        qc, kc, vc, yc = proj(flat(ctx), rope_tabs=None, mod_row=b, tm=tm_ctx, name="proj_ctx")
        qc, kc, vc = (a.reshape(b, lc, ATTN_WIDTH) for a in (qc, kc, vc))
        yc = yc.reshape(b, lc // 2, 4 * FOURIER_WIDTH)
        qx, kx, vx, yx = proj(x, rope_tabs=rope_tabs, mod_row=None, tm=1024, name="proj_lat")

        attn_x = attend(qx, [kc, kx], [vc, vx], tq=l, heads=1, name="attn_lat")
        four_x = _fourier(yx, pos_lat, name="fourier_lat")
        x_new = ffn(x, attn_x, four_x, mod_row=None, tm=1024, name="out_ffn_lat")
        if not last:
            attn_c = attend(qc, [kc], [vc], tq=lc, heads=N_HEADS, name="attn_ctx")
            four_c = _fourier(yc, pos_ctx, name="fourier_ctx")
            ctx = ffn(flat(ctx), flat(attn_c), flat(four_c), mod_row=b, tm=tm_ctx,
                      name="out_ffn_ctx").reshape(b, lc, d)
        x = x_new
    return x
```

```python
import functools
import math

import numpy as np
import jax
import jax.numpy as jnp
from jax import lax
from jax.experimental import pallas as pl
from jax.experimental.pallas import tpu as pltpu

D_MODEL = 1024
GRID_W = 64
ATTN_WIDTH = 512
FOURIER_WIDTH = 512
HEAD_DIM = 64
N_HEADS = 4
V_HEAD_DIM = 128
N_FGROUPS = 4
FGROUP_DIM = 128
IN_COLS = 2048
ROPE_AXIS_DIM = 32
ROPE_THETA = 10000.0
EPS = 1e-6

LANES = 128
MXU_DIM = 256
MOD_ROWS = 16
VMEM_LIMIT_BYTES = 56 * 1024 * 1024
SAFE_SCORE_BOUND = 40.0
LOG2_E = 1.4426950408889634

F32 = jnp.float32
BF16 = jnp.bfloat16
_NT = (((1,), (1,)), ((), ()))


def _params(*sem):
    return pltpu.CompilerParams(dimension_semantics=sem, vmem_limit_bytes=VMEM_LIMIT_BYTES)


def _resident(shape, layer=None):
    zeros = (0,) * len(shape)
    if layer is None:
        return pl.BlockSpec(shape, lambda *_: zeros, pipeline_mode=pl.Buffered(1))
    return pl.BlockSpec((None,) + tuple(shape), lambda *_: (layer,) + zeros, pipeline_mode=pl.Buffered(1))


def _mod_spec(layer, row_of):
    return pl.BlockSpec((None, None, 6, D_MODEL), lambda *g: (layer, row_of(*g), 0, 0))


def _rope_tables(n_tokens):
    rows = n_tokens // GRID_W
    row = np.repeat(np.arange(rows, dtype=np.float64), GRID_W)
    col = np.tile(np.arange(GRID_W, dtype=np.float64), rows)
    inv = np.float32(ROPE_THETA) ** (-np.arange(0, ROPE_AXIS_DIM, 2, dtype=np.float32) / np.float32(ROPE_AXIS_DIM))
    inv = inv.astype(np.float64)
    half = ROPE_AXIS_DIM // 2
    d = np.arange(HEAD_DIM)
    pos = np.where(d[None, :] < ROPE_AXIS_DIM, row[:, None], col[:, None])
    ang = pos * inv[d % half][None, :]
    low = (d % ROPE_AXIS_DIM) < half
    cos = np.cos(ang)
    sin = np.where(low[None, :], -np.sin(ang), np.sin(ang))
    tile2 = lambda a: jnp.asarray(np.tile(a, (1, LANES // HEAD_DIM)), F32)
    return tile2(cos), tile2(sin)


def _channel_dft():
    n = FGROUP_DIM
    k = (np.arange(n)[:, None] * np.arange(n)[None, :]) % n
    ang = 2.0 * np.pi * k / n
    m = np.concatenate([np.cos(ang), np.sin(ang)], axis=1) / np.sqrt(n)
    return jnp.asarray(m, F32).astype(BF16)


def _position_dft(n_tokens):
    half = n_tokens // 2
    lo = np.arange(half)[:, None]
    m = np.arange(half)[None, :]
    mats = []
    for s in (0, 1):
        k = ((2 * m + s) * lo) % n_tokens
        ang = 2.0 * np.pi * k / n_tokens
        mats.append(jnp.asarray(np.cos(ang) / np.sqrt(n_tokens), F32).astype(BF16))
        mats.append(jnp.asarray(-np.sin(ang) / np.sqrt(n_tokens), F32).astype(BF16))
    return mats


def _segment_mean():
    seg = np.arange(MXU_DIM) // HEAD_DIM
    return jnp.asarray((seg[:, None] == seg[None, :]) / HEAD_DIM, BF16)


def _mod_kernel(c_ref, w_ref, b_ref, o_ref):
    cv = c_ref[...]
    s = cv / (1.0 + jnp.exp(-cv))
    o_ref[0] = jnp.dot(s.astype(BF16), w_ref[0].astype(BF16), preferred_element_type=F32) + b_ref[0]


def _adaln_mod(cvec, w_ada, b_ada):
    depth, d, n = w_ada.shape
    tn = 1536
    return pl.pallas_call(
        _mod_kernel,
        grid=(depth, n // tn),
        in_specs=[pl.BlockSpec((MOD_ROWS, d), lambda i, j: (0, 0)),
                  pl.BlockSpec((1, d, tn), lambda i, j: (i, 0, j)),
                  pl.BlockSpec((1, 1, tn), lambda i, j: (i, 0, j))],
        out_specs=pl.BlockSpec((1, MOD_ROWS, tn), lambda i, j: (i, 0, j)),
        out_shape=jax.ShapeDtypeStruct((depth, MOD_ROWS, n), F32),
        compiler_params=_params("arbitrary", "arbitrary"),
        name="adaln_mod",
    )(cvec, w_ada, b_ada.reshape(depth, 1, n))


def _proj_kernel(*refs, rope):
    if rope:
        (x_ref, mod_ref, g1_ref, w_ref, qg_ref, kg_ref, seg_ref, cdft_ref,
         cos_ref, sin_ref, q_ref, k_ref, v_ref, y_ref, f_scr) = refs
    else:
        (x_ref, mod_ref, g1_ref, w_ref, qg_ref, kg_ref, seg_ref, cdft_ref,
         q_ref, k_ref, v_ref, y_ref, f_scr) = refs
    tm = x_ref.shape[1]
    sub = f_scr.shape[1]
    swap_idx = lax.broadcasted_iota(jnp.int32, (sub, LANES), 1) ^ (ROPE_AXIS_DIM // 2)

    qk_cols = 2 * ATTN_WIDTH

    gain_scale = g1_ref[...] * (1.0 + mod_ref[1:2, :])

    def norm_mod(r0):
        x = x_ref[0, r0:r0 + sub, :]
        ms = jnp.mean(x * x, axis=-1, keepdims=True)
        return (x * lax.rsqrt(ms + EPS) * gain_scale + mod_ref[0:1, :]).astype(BF16)

    def project_qk(h):
        return jnp.dot(h, w_ref[:, 0:qk_cols], preferred_element_type=F32)

    def project_vf(h):
        return jnp.dot(h, w_ref[:, qk_cols:], preferred_element_type=F32)

    def norm_rope(t, gain, out_ref, r0):
        for cb in range(ATTN_WIDTH // MXU_DIM):
            tc = t[:, MXU_DIM * cb:MXU_DIM * (cb + 1)]
            ms = jnp.dot((tc * tc).astype(BF16), seg_ref[...], preferred_element_type=F32)
            tn = tc * lax.rsqrt(ms + EPS) * gain
            for hb in range(MXU_DIM // LANES):
                u = tn[:, LANES * hb:LANES * (hb + 1)]
                if rope:
                    partner = jnp.take_along_axis(u, swap_idx, axis=1)
                    u = u * cos_ref[r0:r0 + sub, :] + partner * sin_ref[r0:r0 + sub, :]
                c0 = MXU_DIM * cb + LANES * hb
                out_ref[0, r0:r0 + sub, c0:c0 + LANES] = u.astype(BF16)

    def emit_qk(r0, z):
        norm_rope(z[:, 0:ATTN_WIDTH], qg_ref[...], q_ref, r0)
        norm_rope(z[:, ATTN_WIDTH:], kg_ref[...], k_ref, r0)

    def emit_vf(r0, z):
        v_ref[0, r0:r0 + sub, :] = z[:, 0:ATTN_WIDTH].astype(BF16)
        w = FOURIER_WIDTH
        h0 = r0 // 2
        for g in range(N_FGROUPS):
            c0 = FGROUP_DIM * g
            f_scr[g] = z[:, ATTN_WIDTH + c0:ATTN_WIDTH + c0 + FGROUP_DIM]
            for parity in range(2):
                fp = f_scr[g, pl.ds(parity, sub // 2, stride=2), :].astype(BF16)
                yg = jnp.dot(fp, cdft_ref[g], preferred_element_type=F32)
                base = 2 * w * parity
                y_ref[0, h0:h0 + sub // 2, base + c0:base + c0 + FGROUP_DIM] = (
                    yg[:, :FGROUP_DIM].astype(BF16))
                y_ref[0, h0:h0 + sub // 2, base + w + c0:base + w + c0 + FGROUP_DIM] = (
                    yg[:, FGROUP_DIM:].astype(BF16))

    starts = list(range(0, tm, sub))
    h = norm_mod(starts[0])
    z_qk = project_qk(h)
    pending_vf = None
    for t, r0 in enumerate(starts):
        if pending_vf is not None:
            emit_vf(*pending_vf)
        pending_vf = (r0, project_vf(h))
        emit_qk(r0, z_qk)
        if t + 1 < len(starts):
            h = norm_mod(starts[t + 1])
            z_qk = project_qk(h)
    emit_vf(*pending_vf)


def _project(x, mod_all, g1, w_in, qg, kg, ones, cdft, rope_tabs, *, layer, mod_row, tm, name):
    b, l, d = x.shape
    rope = rope_tabs is not None
    row_of = (lambda i, t: i) if mod_row is None else (lambda i, t: mod_row)
    tok = lambda width: pl.BlockSpec((1, tm, width), lambda i, t: (i, t, 0))
    in_specs = [tok(d),
                _mod_spec(layer, row_of),
                _resident((1, d), layer),
                _resident((d, IN_COLS), layer),
                _resident((1, MXU_DIM), layer),
                _resident((1, MXU_DIM), layer),
                _resident((MXU_DIM, MXU_DIM)),
                _resident((N_FGROUPS, FGROUP_DIM, 2 * FGROUP_DIM), layer)]
    args = [x, mod_all, g1, w_in, qg, kg, ones, cdft]
    if rope:
        in_specs += [pl.BlockSpec((tm, LANES), lambda i, t: (t, 0))] * 2
        args += list(rope_tabs)
    out = lambda width: jax.ShapeDtypeStruct((b, l, width), BF16)
    return pl.pallas_call(
        functools.partial(_proj_kernel, rope=rope),
        grid=(b, l // tm),
        in_specs=in_specs,
        out_specs=[tok(ATTN_WIDTH), tok(ATTN_WIDTH), tok(ATTN_WIDTH),
                   pl.BlockSpec((1, tm // 2, 4 * FOURIER_WIDTH), lambda i, t: (i, t, 0))],
        out_shape=[out(ATTN_WIDTH), out(ATTN_WIDTH), out(ATTN_WIDTH),
                   jax.ShapeDtypeStruct((b, l // 2, 4 * FOURIER_WIDTH), BF16)],
        scratch_shapes=[pltpu.VMEM((N_FGROUPS, min(tm, MXU_DIM), FGROUP_DIM), F32)],
        compiler_params=_params("arbitrary", "arbitrary"),
        name=name,
    )(*args)


def _attn_kernel(*refs, n_seg, lambda_init, sub_rows):
    bound_ref, q_ref = refs[0], refs[1]
    k_refs = refs[2:2 + n_seg]
    v_refs = refs[2 + n_seg:2 + 2 * n_seg]
    lam_ref, sg_ref, o_ref = refs[2 + 2 * n_seg:]

    def attend(bounded):
        lv = lam_ref[...]
        lam = (jnp.exp(jnp.sum(lv[0:1] * lv[1:2], axis=-1, keepdims=True))
               - jnp.exp(jnp.sum(lv[2:3] * lv[3:4], axis=-1, keepdims=True)) + lambda_init)
        lane = lax.broadcasted_iota(jnp.int32, (1, LANES), 1)
        first = (lane < HEAD_DIM).astype(BF16)
        items = [(pl.ds(LANES * h, LANES), pl.ds(r0, sub_rows))
                 for h in range(q_ref.shape[2] // LANES) for r0 in range(0, q_ref.shape[1], sub_rows)]
        probs = scores(items[0], first, bounded)
        for t, item in enumerate(items):
            nxt = scores(items[t + 1], first, bounded) if t + 1 < len(items) else None
            finish(item, probs, lam)
            probs = nxt

    def scores(item, first, bounded):
        cols, rows = item
        q = q_ref[0, rows, cols]

        def one_map(qm):
            ss = [lax.dot_general(qm, k[0, :, cols], _NT, preferred_element_type=F32) for k in k_refs]
            if bounded:
                ps = [jnp.exp2(s) for s in ss]
            else:
                m = functools.reduce(jnp.maximum, [jnp.max(s, axis=-1, keepdims=True) for s in ss])
                ps = [jnp.exp2(s - m) for s in ss]
            l = functools.reduce(lambda a, c: a + c, [jnp.sum(p, axis=-1, keepdims=True) for p in ps])
            return [p.astype(BF16) for p in ps], l

        return one_map(q * first), one_map(q * (1.0 - first).astype(BF16))

    def finish(item, probs, lam):
        cols, rows = item
        (p1, l1), (p2, l2) = probs
        c = (lam * l1 / l2).astype(BF16)
        o = functools.reduce(lambda a, b: a + b, [
            jnp.dot(pa - c * pb, v[0, :, cols], preferred_element_type=F32)
            for pa, pb, v in zip(p1, p2, v_refs)])
        o = o * (1.0 / l1)
        ms = jnp.mean(o * o, axis=-1, keepdims=True)
        o = o * lax.rsqrt(ms + EPS) * sg_ref[...] * (1.0 - lambda_init)
        o_ref[0, rows, cols] = o.astype(BF16)

    safe = bound_ref[0, 0] <= SAFE_SCORE_BOUND
    pl.when(safe)(lambda: attend(True))
    pl.when(jnp.logical_not(safe))(lambda: attend(False))


def _attention(bound, q, ks, vs, lam_rows, subln_g, *, layer, lambda_init, tq, heads, name):
    b, lq, _ = q.shape
    n_seg = len(ks)
    width = heads * LANES
    qo_spec = pl.BlockSpec((1, tq, width), lambda i, h, t: (i, t, h))
    kv_specs = [pl.BlockSpec((1, k.shape[1], width), lambda i, h, t: (i, 0, h)) for k in ks]
    return pl.pallas_call(
        functools.partial(_attn_kernel, n_seg=n_seg, lambda_init=lambda_init, sub_rows=min(tq, MXU_DIM)),
        grid=(b, N_HEADS // heads, lq // tq),
        in_specs=[pl.BlockSpec(memory_space=pltpu.SMEM), qo_spec] + kv_specs + kv_specs
                 + [_resident((4, HEAD_DIM), layer), _resident((1, V_HEAD_DIM), layer)],
        out_specs=qo_spec,
        out_shape=jax.ShapeDtypeStruct((b, lq, ATTN_WIDTH), BF16),
        compiler_params=_params("arbitrary", "arbitrary", "arbitrary"),
        name=name,
    )(bound, q, *ks, *vs, lam_rows, subln_g)


def _fold_kernel(cdft_ref, wf_ref, o_ref):
    for g in range(N_FGROUPS):
        wg = wf_ref[g].astype(BF16)
        for part in range(2):
            cols = pl.ds(FGROUP_DIM * part, FGROUP_DIM)
            o_ref[g, :, cols] = jnp.dot(cdft_ref[:, cols], wg, preferred_element_type=F32).astype(BF16)


def _fold_group_map(cdft, w_fourier):
    depth = w_fourier.shape[0]
    return pl.pallas_call(
        _fold_kernel,
        grid=(depth,),
        in_specs=[_resident((FGROUP_DIM, 2 * FGROUP_DIM)),
                  pl.BlockSpec((None, N_FGROUPS, FGROUP_DIM, FGROUP_DIM), lambda i: (i, 0, 0, 0))],
        out_specs=pl.BlockSpec((None, N_FGROUPS, FGROUP_DIM, 2 * FGROUP_DIM), lambda i: (i, 0, 0, 0)),
        out_shape=jax.ShapeDtypeStruct((depth, N_FGROUPS, FGROUP_DIM, 2 * FGROUP_DIM), BF16),
        compiler_params=_params("arbitrary"),
        name="fold_group_map",
    )(cdft, w_fourier)


def _fourier_kernel(y_ref, ce_ref, se_ref, co_ref, so_ref, o_ref):
    w = FOURIER_WIDTH
    e = (jnp.dot(ce_ref[...], y_ref[0, :, 0:w], preferred_element_type=F32)
         + jnp.dot(se_ref[...], y_ref[0, :, w:2 * w], preferred_element_type=F32))
    o = (jnp.dot(co_ref[...], y_ref[0, :, 2 * w:3 * w], preferred_element_type=F32)
         + jnp.dot(so_ref[...], y_ref[0, :, 3 * w:4 * w], preferred_element_type=F32))
    o_ref[0, 0] = (e + o).astype(BF16)
    o_ref[0, 1] = (e - o).astype(BF16)


def _fourier(y2, pos_mats, *, name):
    b, half, _ = y2.shape
    out = pl.pallas_call(
        _fourier_kernel,
        grid=(b,),
        in_specs=[pl.BlockSpec((1, half, 4 * FOURIER_WIDTH), lambda i: (i, 0, 0))]
                 + [_resident((half, half))] * 4,
        out_specs=pl.BlockSpec((1, 2, half, FOURIER_WIDTH), lambda i: (i, 0, 0, 0)),
        out_shape=jax.ShapeDtypeStruct((b, 2, half, FOURIER_WIDTH), BF16),
        compiler_params=_params("arbitrary"),
        name=name,
    )(y2, *pos_mats)
    return out.reshape(b, 2 * half, FOURIER_WIDTH)


def _ffn_kernel(x_ref, a_ref, f_ref, mod_ref, g2_ref, wo_ref, wg_ref, wu_ref, wd_ref, o_ref, *, sub):
    tm = x_ref.shape[1]

    def mix_norm(r0):
        rows = pl.ds(r0, sub)
        mix = (jnp.dot(a_ref[0, rows, :], wo_ref[0:ATTN_WIDTH, :], preferred_element_type=F32)
               + jnp.dot(f_ref[0, rows, :], wo_ref[ATTN_WIDTH:, :], preferred_element_type=F32))
        x1 = x_ref[0, rows, :] + mod_ref[2:3, :] * mix
        ms = jnp.mean(x1 * x1, axis=-1, keepdims=True)
        h = x1 * lax.rsqrt(ms + EPS) * g2_ref[...]
        h = (h * (1.0 + mod_ref[4:5, :]) + mod_ref[3:4, :]).astype(BF16)
        return x1, h

    def swiglu(r0, x1, h):
        gate = jnp.dot(h, wg_ref[...], preferred_element_type=F32)
        up = jnp.dot(h, wu_ref[...], preferred_element_type=F32)
        act = (gate / (1.0 + jnp.exp(-gate)) * up).astype(BF16)
        y = jnp.dot(act, wd_ref[...], preferred_element_type=F32)
        o_ref[0, pl.ds(r0, sub), :] = x1 + mod_ref[5:6, :] * y

    starts = list(range(0, tm, sub))
    cur = mix_norm(starts[0])
    for t, r0 in enumerate(starts):
        nxt = mix_norm(starts[t + 1]) if t + 1 < len(starts) else None
        swiglu(r0, *cur)
        cur = nxt


def _out_ffn(x, attn, four, mod_all, g2, w_out, w_gate, w_up, w_down, *, layer, mod_row, tm, name):
    b, l, d = x.shape
    d_ff = w_gate.shape[-1]
    row_of = (lambda i, t: i) if mod_row is None else (lambda i, t: mod_row)
    tok = lambda width: pl.BlockSpec((1, tm, width), lambda i, t: (i, t, 0))
    return pl.pallas_call(
        functools.partial(_ffn_kernel, sub=min(tm, MXU_DIM)),
        grid=(b, l // tm),
        in_specs=[tok(d), tok(ATTN_WIDTH), tok(FOURIER_WIDTH),
                  _mod_spec(layer, row_of),
                  _resident((1, d), layer),
                  _resident((d, d), layer),
                  _resident((d, d_ff), layer),
                  _resident((d, d_ff), layer),
                  _resident((d_ff, d), layer)],
        out_specs=tok(d),
        out_shape=jax.ShapeDtypeStruct((b, l, d), F32),
        compiler_params=_params("arbitrary", "arbitrary"),
        name=name,
    )(x, attn, four, mod_all, g2, w_out, w_gate, w_up, w_down)


def kernel(x, c, ctx, c_ctx, w_ada, b_ada, norm1_g, norm2_g, w_in, q_norm_g, k_norm_g, lambda_q1, lambda_k1, lambda_q2, lambda_k2, subln_g, w_fourier, w_out, w_gate, w_up, w_down):
    depth = w_ada.shape[0]
    b, l, d = x.shape
    lc = ctx.shape[1]
    assert b + 1 <= MOD_ROWS and lc % 2 == 0
    tm_ctx = math.gcd(b * lc, 1024)

    rope_tabs = _rope_tables(l)
    cdft = _fold_group_map(_channel_dft(), w_fourier)
    pos_lat = _position_dft(l)
    pos_ctx = _position_dft(lc)
    ones = _segment_mean()

    cvec = jnp.zeros((MOD_ROWS, d), F32).at[:b].set(c).at[b].set(c_ctx)
    mod_all = _adaln_mod(cvec, w_ada, b_ada).reshape(depth, MOD_ROWS, 6, d)

    w_in_b, w_out_b = w_in.astype(BF16), w_out.astype(BF16)
    w_gate_b, w_up_b, w_down_b = w_gate.astype(BF16), w_up.astype(BF16), w_down.astype(BF16)
    g1, g2 = norm1_g.reshape(depth, 1, d), norm2_g.reshape(depth, 1, d)
    tile_gain = lambda g: jnp.tile(g, (1, MXU_DIM // HEAD_DIM)).reshape(depth, 1, MXU_DIM)
    qg, kg = tile_gain(q_norm_g * (HEAD_DIM ** -0.5 * LOG2_E)), tile_gain(k_norm_g)
    lam_rows = jnp.stack([lambda_q1, lambda_k1, lambda_q2, lambda_k2], axis=1)
    sg = subln_g.reshape(depth, 1, V_HEAD_DIM)
    score_bound = (HEAD_DIM ** 0.5 * LOG2_E) * jnp.max(jnp.abs(q_norm_g), axis=1) * jnp.max(jnp.abs(k_norm_g), axis=1)

    for i in range(depth):
        last = i == depth - 1
        lambda_init = 0.8 - 0.6 * math.exp(-0.3 * i)
        bound = score_bound[i].reshape(1, 1)
        proj = functools.partial(_project, mod_all=mod_all, g1=g1, w_in=w_in_b, qg=qg, kg=kg,
                                 ones=ones, cdft=cdft, layer=i)
        attend = functools.partial(_attention, bound, lam_rows=lam_rows, subln_g=sg, layer=i,
                                   lambda_init=lambda_init)
        ffn = functools.partial(_out_ffn, mod_all=mod_all, g2=g2, w_out=w_out_b, w_gate=w_gate_b,
                                w_up=w_up_b, w_down=w_down_b, layer=i)

        flat = lambda a: a.reshape(1, b * lc, a.shape[-1])
        qc, kc, vc, yc = proj(flat(ctx), rope_tabs=None, mod_row=b, tm=tm_ctx, name="proj_ctx")
        qc, kc, vc = (a.reshape(b, lc, ATTN_WIDTH) for a in (qc, kc, vc))
        yc = yc.reshape(b, lc // 2, 4 * FOURIER_WIDTH)
        qx, kx, vx, yx = proj(x, rope_tabs=rope_tabs, mod_row=None, tm=1024, name="proj_lat")

        attn_x = attend(qx, [kc, kx], [vc, vx], tq=l, heads=1, name="attn_lat")
        four_x = _fourier(yx, pos_lat, name="fourier_lat")
        x_new = ffn(x, attn_x, four_x, mod_row=None, tm=1024, name="out_ffn_lat")
        if not last:
            attn_c = attend(qc, [kc], [vc], tq=lc, heads=N_HEADS, name="attn_ctx")
            four_c = _fourier(yc, pos_ctx, name="fourier_ctx")
            ctx = ffn(flat(ctx), flat(attn_c), flat(four_c), mod_row=b, tm=tm_ctx,
                      name="out_ffn_ctx").reshape(b, lc, d)
        x = x_new
    return x
```

```python
import functools
import math

import numpy as np
import jax
import jax.numpy as jnp
from jax import lax
from jax.experimental import pallas as pl
from jax.experimental.pallas import tpu as pltpu

D_MODEL = 1024
GRID_W = 64
ATTN_WIDTH = 512
FOURIER_WIDTH = 512
HEAD_DIM = 64
N_HEADS = 4
V_HEAD_DIM = 128
N_FGROUPS = 4
FGROUP_DIM = 128
IN_COLS = 2048
ROPE_AXIS_DIM = 32
ROPE_THETA = 10000.0
EPS = 1e-6

LANES = 128
MXU_DIM = 256
BF16_SUBLANES = 16
MOD_ROWS = 16
VMEM_LIMIT_BYTES = 56 * 1024 * 1024
SAFE_SCORE_BOUND = 40.0
LOG2_E = 1.4426950408889634

F32 = jnp.float32
BF16 = jnp.bfloat16
_NT = (((1,), (1,)), ((), ()))


def _params(*sem):
    return pltpu.CompilerParams(dimension_semantics=sem, vmem_limit_bytes=VMEM_LIMIT_BYTES)


def _resident(shape, layer=None):
    zeros = (0,) * len(shape)
    if layer is None:
        return pl.BlockSpec(shape, lambda *_: zeros, pipeline_mode=pl.Buffered(1))
    return pl.BlockSpec((None,) + tuple(shape), lambda *_: (layer,) + zeros, pipeline_mode=pl.Buffered(1))


def _mod_spec(layer, row_of):
    return pl.BlockSpec((None, None, 6, D_MODEL), lambda *g: (layer, row_of(*g), 0, 0))


def _rope_tables(n_tokens):
    rows = n_tokens // GRID_W
    row = np.repeat(np.arange(rows, dtype=np.float64), GRID_W)
    col = np.tile(np.arange(GRID_W, dtype=np.float64), rows)
    inv = np.float32(ROPE_THETA) ** (-np.arange(0, ROPE_AXIS_DIM, 2, dtype=np.float32) / np.float32(ROPE_AXIS_DIM))
    inv = inv.astype(np.float64)
    half = ROPE_AXIS_DIM // 2
    d = np.arange(HEAD_DIM)
    pos = np.where(d[None, :] < ROPE_AXIS_DIM, row[:, None], col[:, None])
    ang = pos * inv[d % half][None, :]
    low = (d % ROPE_AXIS_DIM) < half
    cos = np.cos(ang)
    sin = np.where(low[None, :], -np.sin(ang), np.sin(ang))
    tile2 = lambda a: jnp.asarray(np.tile(a, (1, LANES // HEAD_DIM)), F32)
    return tile2(cos), tile2(sin)


def _channel_dft():
    n = FGROUP_DIM
    k = (np.arange(n)[:, None] * np.arange(n)[None, :]) % n
    ang = 2.0 * np.pi * k / n
    m = np.concatenate([np.cos(ang), np.sin(ang)], axis=1) / np.sqrt(n)
    return jnp.asarray(m, F32).astype(BF16)


def _position_dft(n_tokens):
    half = n_tokens // 2
    lo = np.arange(half)[:, None]
    m = np.arange(half)[None, :]
    mats = []
    for s in (0, 1):
        k = ((2 * m + s) * lo) % n_tokens
        ang = 2.0 * np.pi * k / n_tokens
        mats.append(jnp.asarray(np.cos(ang) / np.sqrt(n_tokens), F32).astype(BF16))
        mats.append(jnp.asarray(-np.sin(ang) / np.sqrt(n_tokens), F32).astype(BF16))
    return mats


def _segment_mean():
    seg = np.arange(MXU_DIM) // HEAD_DIM
    return jnp.asarray((seg[:, None] == seg[None, :]) / HEAD_DIM, BF16)


def _mod_kernel(c_ref, w_ref, b_ref, o_ref):
    cv = c_ref[...]
    s = cv / (1.0 + jnp.exp(-cv))
    o_ref[0] = jnp.dot(s.astype(BF16), w_ref[0].astype(BF16), preferred_element_type=F32) + b_ref[0]


def _adaln_mod(cvec, w_ada, b_ada):
    depth, d, n = w_ada.shape
    tn = 1536
    return pl.pallas_call(
        _mod_kernel,
        grid=(depth, n // tn),
        in_specs=[pl.BlockSpec((MOD_ROWS, d), lambda i, j: (0, 0)),
                  pl.BlockSpec((1, d, tn), lambda i, j: (i, 0, j)),
                  pl.BlockSpec((1, 1, tn), lambda i, j: (i, 0, j))],
        out_specs=pl.BlockSpec((1, MOD_ROWS, tn), lambda i, j: (i, 0, j)),
        out_shape=jax.ShapeDtypeStruct((depth, MOD_ROWS, n), F32),
        compiler_params=_params("arbitrary", "arbitrary"),
        name="adaln_mod",
    )(cvec, w_ada, b_ada.reshape(depth, 1, n))


def _proj_kernel(*refs, rope):
    if rope:
        (x_ref, mod_ref, g1_ref, w_ref, qg_ref, kg_ref, seg_ref, cdft_ref,
         cos_ref, sin_ref, q_ref, k_ref, v_ref, y_ref, f_scr) = refs
    else:
        (x_ref, mod_ref, g1_ref, w_ref, qg_ref, kg_ref, seg_ref, cdft_ref,
         q_ref, k_ref, v_ref, y_ref, f_scr) = refs
    tm = x_ref.shape[1]
    sub = f_scr.shape[1]
    swap_idx = lax.broadcasted_iota(jnp.int32, (sub, LANES), 1) ^ (ROPE_AXIS_DIM // 2)

    qk_cols = 2 * ATTN_WIDTH

    gain_scale = g1_ref[...] * (1.0 + mod_ref[1:2, :])

    def norm_mod(r0):
        x = x_ref[0, r0:r0 + sub, :]
        ms = jnp.mean(x * x, axis=-1, keepdims=True)
        return (x * lax.rsqrt(ms + EPS) * gain_scale + mod_ref[0:1, :]).astype(BF16)

    def project_qk(h):
        return jnp.dot(h, w_ref[:, 0:qk_cols], preferred_element_type=F32)

    def project_vf(h):
        return jnp.dot(h, w_ref[:, qk_cols:], preferred_element_type=F32)

    def norm_rope(t, gain, out_ref, r0):
        for cb in range(ATTN_WIDTH // MXU_DIM):
            tc = t[:, MXU_DIM * cb:MXU_DIM * (cb + 1)]
            ms = jnp.dot((tc * tc).astype(BF16), seg_ref[...], preferred_element_type=F32)
            tn = tc * lax.rsqrt(ms + EPS) * gain
            for hb in range(MXU_DIM // LANES):
                u = tn[:, LANES * hb:LANES * (hb + 1)]
                if rope:
                    partner = jnp.take_along_axis(u, swap_idx, axis=1)
                    u = u * cos_ref[r0:r0 + sub, :] + partner * sin_ref[r0:r0 + sub, :]
                c0 = MXU_DIM * cb + LANES * hb
                out_ref[0, r0:r0 + sub, c0:c0 + LANES] = u.astype(BF16)

    def emit_qk(r0, z):
        norm_rope(z[:, 0:ATTN_WIDTH], qg_ref[...], q_ref, r0)
        norm_rope(z[:, ATTN_WIDTH:], kg_ref[...], k_ref, r0)

    def emit_vf(r0, z):
        v_ref[0, r0:r0 + sub, :] = z[:, 0:ATTN_WIDTH].astype(BF16)
        w = FOURIER_WIDTH
        h0 = r0 // 2
        for g in range(N_FGROUPS):
            c0 = FGROUP_DIM * g
            f_scr[g] = z[:, ATTN_WIDTH + c0:ATTN_WIDTH + c0 + FGROUP_DIM]
            for parity in range(2):
                fp = f_scr[g, pl.ds(parity, sub // 2, stride=2), :].astype(BF16)
                yg = jnp.dot(fp, cdft_ref[g], preferred_element_type=F32)
                base = 2 * w * parity
                y_ref[0, h0:h0 + sub // 2, base + c0:base + c0 + FGROUP_DIM] = (
                    yg[:, :FGROUP_DIM].astype(BF16))
                y_ref[0, h0:h0 + sub // 2, base + w + c0:base + w + c0 + FGROUP_DIM] = (
                    yg[:, FGROUP_DIM:].astype(BF16))

    starts = list(range(0, tm, sub))
    h = norm_mod(starts[0])
    z_qk = project_qk(h)
    pending_vf = None
    for t, r0 in enumerate(starts):
        if pending_vf is not None:
            emit_vf(*pending_vf)
        pending_vf = (r0, project_vf(h))
        emit_qk(r0, z_qk)
        if t + 1 < len(starts):
            h = norm_mod(starts[t + 1])
            z_qk = project_qk(h)
    emit_vf(*pending_vf)


def _project(x, mod_all, g1, w_in, qg, kg, ones, cdft, rope_tabs, *, layer, w_layer, mod_row, tm, name):
    b, l, d = x.shape
    rope = rope_tabs is not None
    row_of = (lambda i, t: i) if mod_row is None else (lambda i, t: mod_row)
    tok = lambda width: pl.BlockSpec((1, tm, width), lambda i, t: (i, t, 0))
    in_specs = [tok(d),
                _mod_spec(layer, row_of),
                _resident((1, d), layer),
                _resident((d, IN_COLS), w_layer),
                _resident((1, MXU_DIM), layer),
                _resident((1, MXU_DIM), layer),
                _resident((MXU_DIM, MXU_DIM)),
                _resident((N_FGROUPS, FGROUP_DIM, 2 * FGROUP_DIM), layer)]
    args = [x, mod_all, g1, w_in, qg, kg, ones, cdft]
    if rope:
        in_specs += [pl.BlockSpec((tm, LANES), lambda i, t: (t, 0))] * 2
        args += list(rope_tabs)
    out = lambda width: jax.ShapeDtypeStruct((b, l, width), BF16)
    return pl.pallas_call(
        functools.partial(_proj_kernel, rope=rope),
        grid=(b, l // tm),
        in_specs=in_specs,
        out_specs=[tok(ATTN_WIDTH), tok(ATTN_WIDTH), tok(ATTN_WIDTH),
                   pl.BlockSpec((1, tm // 2, 4 * FOURIER_WIDTH), lambda i, t: (i, t, 0))],
        out_shape=[out(ATTN_WIDTH), out(ATTN_WIDTH), out(ATTN_WIDTH),
                   jax.ShapeDtypeStruct((b, l // 2, 4 * FOURIER_WIDTH), BF16)],
        scratch_shapes=[pltpu.VMEM((N_FGROUPS, min(tm, MXU_DIM), FGROUP_DIM), F32)],
        compiler_params=_params("arbitrary", "arbitrary"),
        name=name,
    )(*args)


def _attn_kernel(*refs, n_seg, n_cast, lambda_init, sub_rows):
    bound_ref, q_ref = refs[0], refs[1]
    k_refs = refs[2:2 + n_seg]
    v_refs = refs[2 + n_seg:2 + 2 * n_seg]
    lam_ref, sg_ref = refs[2 + 2 * n_seg:4 + 2 * n_seg]
    cast_in = refs[4 + 2 * n_seg:4 + 2 * n_seg + n_cast]
    o_ref = refs[4 + 2 * n_seg + n_cast]
    cast_out = refs[5 + 2 * n_seg + n_cast:]


    def attend(bounded):
        for src, dst in zip(cast_in, cast_out):
            dst[...] = src[...].astype(BF16)
        lv = lam_ref[...]
        lam = (jnp.exp(jnp.sum(lv[0:1] * lv[1:2], axis=-1, keepdims=True))
               - jnp.exp(jnp.sum(lv[2:3] * lv[3:4], axis=-1, keepdims=True)) + lambda_init)
        lane = lax.broadcasted_iota(jnp.int32, (1, LANES), 1)
        first = (lane < HEAD_DIM).astype(BF16)
        items = [(pl.ds(LANES * h, LANES), pl.ds(r0, sub_rows))
                 for h in range(q_ref.shape[2] // LANES) for r0 in range(0, q_ref.shape[1], sub_rows)]
        probs = scores(items[0], first, bounded)
        for t, item in enumerate(items):
            nxt = scores(items[t + 1], first, bounded) if t + 1 < len(items) else None
            finish(item, probs, lam)
            probs = nxt

    def scores(item, first, bounded):
        cols, rows = item
        q = q_ref[0, rows, cols]

        def one_map(qm):
            ss = [lax.dot_general(qm, k[0, :, cols], _NT, preferred_element_type=F32) for k in k_refs]
            if bounded:
                ps = [jnp.exp2(s) for s in ss]
            else:
                m = functools.reduce(jnp.maximum, [jnp.max(s, axis=-1, keepdims=True) for s in ss])
                ps = [jnp.exp2(s - m) for s in ss]
            l = functools.reduce(lambda a, c: a + c, [jnp.sum(p, axis=-1, keepdims=True) for p in ps])
            return [p.astype(BF16) for p in ps], l

        return one_map(q * first), one_map(q * (1.0 - first).astype(BF16))

    def finish(item, probs, lam):
        cols, rows = item
        (p1, l1), (p2, l2) = probs
        c = (lam * l1 / l2).astype(BF16)
        o = functools.reduce(lambda a, b: a + b, [
            jnp.dot(pa - c * pb, v[0, :, cols], preferred_element_type=F32)
            for pa, pb, v in zip(p1, p2, v_refs)])
        o = o * (1.0 / l1)
        ms = jnp.mean(o * o, axis=-1, keepdims=True)
        o = o * lax.rsqrt(ms + EPS) * sg_ref[...] * (1.0 - lambda_init)
        o_ref[0, rows, cols] = o.astype(BF16)

    safe = bound_ref[0, 0] <= SAFE_SCORE_BOUND
    pl.when(safe)(lambda: attend(True))
    pl.when(jnp.logical_not(safe))(lambda: attend(False))


def _attention(bound, q, ks, vs, lam_rows, subln_g, casts=(), *, layer, lambda_init, tq, heads, name):
    b, lq, _ = q.shape
    n_seg = len(ks)
    width = heads * LANES
    grid = (b, N_HEADS // heads, lq // tq)
    qo_spec = pl.BlockSpec((1, tq, width), lambda i, h, t: (i, t, h))
    kv_specs = [pl.BlockSpec((1, k.shape[1], width), lambda i, h, t: (i, 0, h)) for k in ks]

    n_steps = grid[0] * grid[1] * grid[2]
    step_of = lambda i, h, t: (i * grid[1] + h) * grid[2] + t
    cast_in_specs, cast_out_specs, cast_shapes = [], [], []
    for w, w_layer in casts:
        _, r, c = w.shape
        per = next(p for p in range(1, n_steps + 1)
                   if n_steps % p == 0 and r % (n_steps // p) == 0
                   and (r // (n_steps // p)) % BF16_SUBLANES == 0)
        rows = r // (n_steps // per)
        cast_in_specs.append(pl.BlockSpec(
            (None, rows, c), lambda i, h, t, w_layer=w_layer, per=per: (w_layer, step_of(i, h, t) // per, 0)))
        cast_out_specs.append(pl.BlockSpec(
            (None, rows, c), lambda i, h, t, per=per: (0, step_of(i, h, t) // per, 0)))
        cast_shapes.append(jax.ShapeDtypeStruct((1, r, c), BF16))

    out = pl.pallas_call(
        functools.partial(_attn_kernel, n_seg=n_seg, n_cast=len(casts), lambda_init=lambda_init,
                          sub_rows=min(tq, MXU_DIM)),
        grid=grid,
        in_specs=[pl.BlockSpec(memory_space=pltpu.SMEM), qo_spec] + kv_specs + kv_specs
                 + [_resident((4, HEAD_DIM), layer), _resident((1, V_HEAD_DIM), layer)] + cast_in_specs,
        out_specs=[qo_spec] + cast_out_specs,
        out_shape=[jax.ShapeDtypeStruct((b, lq, ATTN_WIDTH), BF16)] + cast_shapes,
        compiler_params=_params("arbitrary", "arbitrary", "arbitrary"),
        name=name,
    )(bound, q, *ks, *vs, lam_rows, subln_g, *[w for w, _ in casts])
    return out[0], out[1:]


def _fold_kernel(cdft_ref, wf_ref, o_ref):
    for g in range(N_FGROUPS):
        wg = wf_ref[g].astype(BF16)
        for part in range(2):
            cols = pl.ds(FGROUP_DIM * part, FGROUP_DIM)
            o_ref[g, :, cols] = jnp.dot(cdft_ref[:, cols], wg, preferred_element_type=F32).astype(BF16)


def _fold_group_map(cdft, w_fourier):
    depth = w_fourier.shape[0]
    return pl.pallas_call(
        _fold_kernel,
        grid=(depth,),
        in_specs=[_resident((FGROUP_DIM, 2 * FGROUP_DIM)),
                  pl.BlockSpec((None, N_FGROUPS, FGROUP_DIM, FGROUP_DIM), lambda i: (i, 0, 0, 0))],
        out_specs=pl.BlockSpec((None, N_FGROUPS, FGROUP_DIM, 2 * FGROUP_DIM), lambda i: (i, 0, 0, 0)),
        out_shape=jax.ShapeDtypeStruct((depth, N_FGROUPS, FGROUP_DIM, 2 * FGROUP_DIM), BF16),
        compiler_params=_params("arbitrary"),
        name="fold_group_map",
    )(cdft, w_fourier)


def _fourier_kernel(y_ref, ce_ref, se_ref, co_ref, so_ref, o_ref):
    w = FOURIER_WIDTH
    e = (jnp.dot(ce_ref[...], y_ref[0, :, 0:w], preferred_element_type=F32)
         + jnp.dot(se_ref[...], y_ref[0, :, w:2 * w], preferred_element_type=F32))
    o = (jnp.dot(co_ref[...], y_ref[0, :, 2 * w:3 * w], preferred_element_type=F32)
         + jnp.dot(so_ref[...], y_ref[0, :, 3 * w:4 * w], preferred_element_type=F32))
    o_ref[0, 0] = (e + o).astype(BF16)
    o_ref[0, 1] = (e - o).astype(BF16)


def _fourier(y2, pos_mats, *, name):
    b, half, _ = y2.shape
    out = pl.pallas_call(
        _fourier_kernel,
        grid=(b,),
        in_specs=[pl.BlockSpec((1, half, 4 * FOURIER_WIDTH), lambda i: (i, 0, 0))]
                 + [_resident((half, half))] * 4,
        out_specs=pl.BlockSpec((1, 2, half, FOURIER_WIDTH), lambda i: (i, 0, 0, 0)),
        out_shape=jax.ShapeDtypeStruct((b, 2, half, FOURIER_WIDTH), BF16),
        compiler_params=_params("arbitrary"),
        name=name,
    )(y2, *pos_mats)
    return out.reshape(b, 2 * half, FOURIER_WIDTH)


def _ffn_kernel(x_ref, a_ref, f_ref, mod_ref, g2_ref, wo_ref, wg_ref, wu_ref, wd_ref, o_ref, *, sub):
    tm = x_ref.shape[1]

    def mix_norm(r0):
        rows = pl.ds(r0, sub)
        mix = (jnp.dot(a_ref[0, rows, :], wo_ref[0:ATTN_WIDTH, :], preferred_element_type=F32)
               + jnp.dot(f_ref[0, rows, :], wo_ref[ATTN_WIDTH:, :], preferred_element_type=F32))
        x1 = x_ref[0, rows, :] + mod_ref[2:3, :] * mix
        ms = jnp.mean(x1 * x1, axis=-1, keepdims=True)
        h = x1 * lax.rsqrt(ms + EPS) * g2_ref[...]
        h = (h * (1.0 + mod_ref[4:5, :]) + mod_ref[3:4, :]).astype(BF16)
        return x1, h

    def swiglu(r0, x1, h):
        gate = jnp.dot(h, wg_ref[...], preferred_element_type=F32)
        up = jnp.dot(h, wu_ref[...], preferred_element_type=F32)
        act = (gate / (1.0 + jnp.exp(-gate)) * up).astype(BF16)
        y = jnp.dot(act, wd_ref[...], preferred_element_type=F32)
        o_ref[0, pl.ds(r0, sub), :] = x1 + mod_ref[5:6, :] * y

    starts = list(range(0, tm, sub))
    cur = mix_norm(starts[0])
    for t, r0 in enumerate(starts):
        nxt = mix_norm(starts[t + 1]) if t + 1 < len(starts) else None
        swiglu(r0, *cur)
        cur = nxt


def _out_ffn(x, attn, four, mod_all, g2, w_out, w_gate, w_up, w_down, *, layer, w_layer, mod_row, tm, name):
    b, l, d = x.shape
    d_ff = w_gate.shape[-1]
    row_of = (lambda i, t: i) if mod_row is None else (lambda i, t: mod_row)
    tok = lambda width: pl.BlockSpec((1, tm, width), lambda i, t: (i, t, 0))
    return pl.pallas_call(
        functools.partial(_ffn_kernel, sub=min(tm, MXU_DIM)),
        grid=(b, l // tm),
        in_specs=[tok(d), tok(ATTN_WIDTH), tok(FOURIER_WIDTH),
                  _mod_spec(layer, row_of),
                  _resident((1, d), layer),
                  _resident((d, d), w_layer),
                  _resident((d, d_ff), w_layer),
                  _resident((d, d_ff), w_layer),
                  _resident((d_ff, d), w_layer)],
        out_specs=tok(d),
        out_shape=jax.ShapeDtypeStruct((b, l, d), F32),
        compiler_params=_params("arbitrary", "arbitrary"),
        name=name,
    )(x, attn, four, mod_all, g2, w_out, w_gate, w_up, w_down)


def kernel(x, c, ctx, c_ctx, w_ada, b_ada, norm1_g, norm2_g, w_in, q_norm_g, k_norm_g, lambda_q1, lambda_k1, lambda_q2, lambda_k2, subln_g, w_fourier, w_out, w_gate, w_up, w_down):
    depth = w_ada.shape[0]
    b, l, d = x.shape
    lc = ctx.shape[1]
    assert b + 1 <= MOD_ROWS and lc % 2 == 0
    tm_ctx = math.gcd(b * lc, 1024)

    rope_tabs = _rope_tables(l)
    cdft = _fold_group_map(_channel_dft(), w_fourier)
    pos_lat = _position_dft(l)
    pos_ctx = _position_dft(lc)
    ones = _segment_mean()

    cvec = jnp.zeros((MOD_ROWS, d), F32).at[:b].set(c).at[b].set(c_ctx)
    mod_all = _adaln_mod(cvec, w_ada, b_ada).reshape(depth, MOD_ROWS, 6, d)

    w_in_b = w_in[:1].astype(BF16)
    g1, g2 = norm1_g.reshape(depth, 1, d), norm2_g.reshape(depth, 1, d)
    tile_gain = lambda g: jnp.tile(g, (1, MXU_DIM // HEAD_DIM)).reshape(depth, 1, MXU_DIM)
    qg, kg = tile_gain(q_norm_g * (HEAD_DIM ** -0.5 * LOG2_E)), tile_gain(k_norm_g)
    lam_rows = jnp.stack([lambda_q1, lambda_k1, lambda_q2, lambda_k2], axis=1)
    sg = subln_g.reshape(depth, 1, V_HEAD_DIM)
    score_bound = (HEAD_DIM ** 0.5 * LOG2_E) * jnp.max(jnp.abs(q_norm_g), axis=1) * jnp.max(jnp.abs(k_norm_g), axis=1)

    for i in range(depth):
        last = i == depth - 1
        lambda_init = 0.8 - 0.6 * math.exp(-0.3 * i)
        bound = score_bound[i].reshape(1, 1)
        proj = functools.partial(_project, mod_all=mod_all, g1=g1, w_in=w_in_b, qg=qg, kg=kg,
                                 ones=ones, cdft=cdft, layer=i, w_layer=0)
        attend = functools.partial(_attention, bound, lam_rows=lam_rows, subln_g=sg, layer=i,
                                   lambda_init=lambda_init)

        flat = lambda a: a.reshape(1, b * lc, a.shape[-1])
        qc, kc, vc, yc = proj(flat(ctx), rope_tabs=None, mod_row=b, tm=tm_ctx, name="proj_ctx")
        qc, kc, vc = (a.reshape(b, lc, ATTN_WIDTH) for a in (qc, kc, vc))
        yc = yc.reshape(b, lc // 2, 4 * FOURIER_WIDTH)
        qx, kx, vx, yx = proj(x, rope_tabs=rope_tabs, mod_row=None, tm=1024, name="proj_lat")

        casts = [(w, i) for w in (w_out, w_gate, w_up, w_down)] + ([] if last else [(w_in, i + 1)])
        attn_x, rounded = attend(qx, [kc, kx], [vc, vx], casts=casts, tq=l, heads=1, name="attn_lat")
        ffn = functools.partial(_out_ffn, mod_all=mod_all, g2=g2, w_out=rounded[0], w_gate=rounded[1],
                                w_up=rounded[2], w_down=rounded[3], layer=i, w_layer=0)
        if not last:
            w_in_b = rounded[4]
        four_x = _fourier(yx, pos_lat, name="fourier_lat")
        x_new = ffn(x, attn_x, four_x, mod_row=None, tm=1024, name="out_ffn_lat")
        if not last:
            attn_c, _ = attend(qc, [kc], [vc], tq=lc, heads=N_HEADS, name="attn_ctx")
            four_c = _fourier(yc, pos_ctx, name="fourier_ctx")
            ctx = ffn(flat(ctx), flat(attn_c), flat(four_c), mod_row=b, tm=tm_ctx,
                      name="out_ffn_ctx").reshape(b, lc, d)
        x = x_new
    return x
```

```python
import functools
import math

import numpy as np
import jax
import jax.numpy as jnp
from jax import lax
from jax.experimental import pallas as pl
from jax.experimental.pallas import tpu as pltpu

D_MODEL = 1024
GRID_W = 64
ATTN_WIDTH = 512
FOURIER_WIDTH = 512
HEAD_DIM = 64
N_HEADS = 4
V_HEAD_DIM = 128
N_FGROUPS = 4
FGROUP_DIM = 128
IN_COLS = 2048
ROPE_AXIS_DIM = 32
ROPE_THETA = 10000.0
EPS = 1e-6

LANES = 128
MXU_DIM = 256
BF16_SUBLANES = 16
MOD_ROWS = 16
VMEM_LIMIT_BYTES = 56 * 1024 * 1024
SAFE_SCORE_BOUND = 40.0
LOG2_E = 1.4426950408889634

F32 = jnp.float32
BF16 = jnp.bfloat16
_NT = (((1,), (1,)), ((), ()))


def _params(*sem):
    return pltpu.CompilerParams(dimension_semantics=sem, vmem_limit_bytes=VMEM_LIMIT_BYTES)


def _resident(shape, layer=None):
    zeros = (0,) * len(shape)
    if layer is None:
        return pl.BlockSpec(shape, lambda *_: zeros, pipeline_mode=pl.Buffered(1))
    return pl.BlockSpec((None,) + tuple(shape), lambda *_: (layer,) + zeros, pipeline_mode=pl.Buffered(1))


def _mod_spec(layer, row_of):
    return pl.BlockSpec((None, None, 6, D_MODEL), lambda *g: (layer, row_of(*g), 0, 0))


def _rope_tables(n_tokens):
    rows = n_tokens // GRID_W
    row = np.repeat(np.arange(rows, dtype=np.float64), GRID_W)
    col = np.tile(np.arange(GRID_W, dtype=np.float64), rows)
    inv = np.float32(ROPE_THETA) ** (-np.arange(0, ROPE_AXIS_DIM, 2, dtype=np.float32) / np.float32(ROPE_AXIS_DIM))
    inv = inv.astype(np.float64)
    half = ROPE_AXIS_DIM // 2
    d = np.arange(HEAD_DIM)
    pos = np.where(d[None, :] < ROPE_AXIS_DIM, row[:, None], col[:, None])
    ang = pos * inv[d % half][None, :]
    low = (d % ROPE_AXIS_DIM) < half
    cos = np.cos(ang)
    sin = np.where(low[None, :], -np.sin(ang), np.sin(ang))
    tile2 = lambda a: jnp.asarray(np.tile(a, (1, LANES // HEAD_DIM)), F32)
    return tile2(cos), tile2(sin)


def _channel_dft():
    n = FGROUP_DIM
    k = (np.arange(n)[:, None] * np.arange(n)[None, :]) % n
    ang = 2.0 * np.pi * k / n
    m = np.concatenate([np.cos(ang), np.sin(ang)], axis=1) / np.sqrt(n)
    return jnp.asarray(m, F32).astype(BF16)


def _position_dft(n_tokens):
    half = n_tokens // 2
    lo = np.arange(half)[:, None]
    m = np.arange(half)[None, :]
    mats = []
    for s in (0, 1):
        k = ((2 * m + s) * lo) % n_tokens
        ang = 2.0 * np.pi * k / n_tokens
        mats.append(jnp.asarray(np.cos(ang) / np.sqrt(n_tokens), F32).astype(BF16))
        mats.append(jnp.asarray(-np.sin(ang) / np.sqrt(n_tokens), F32).astype(BF16))
    return mats


def _segment_mean():
    seg = np.arange(MXU_DIM) // HEAD_DIM
    return jnp.asarray((seg[:, None] == seg[None, :]) / HEAD_DIM, BF16)


def _mod_kernel(c_ref, w_ref, b_ref, o_ref):
    cv = c_ref[...]
    s = cv / (1.0 + jnp.exp(-cv))
    o_ref[0] = jnp.dot(s.astype(BF16), w_ref[0].astype(BF16), preferred_element_type=F32) + b_ref[0]


def _adaln_mod(cvec, w_ada, b_ada):
    depth, d, n = w_ada.shape
    tn = 1536
    return pl.pallas_call(
        _mod_kernel,
        grid=(depth, n // tn),
        in_specs=[pl.BlockSpec((MOD_ROWS, d), lambda i, j: (0, 0)),
                  pl.BlockSpec((1, d, tn), lambda i, j: (i, 0, j)),
                  pl.BlockSpec((1, 1, tn), lambda i, j: (i, 0, j))],
        out_specs=pl.BlockSpec((1, MOD_ROWS, tn), lambda i, j: (i, 0, j)),
        out_shape=jax.ShapeDtypeStruct((depth, MOD_ROWS, n), F32),
        compiler_params=_params("arbitrary", "arbitrary"),
        name="adaln_mod",
    )(cvec, w_ada, b_ada.reshape(depth, 1, n))


def _proj_kernel(*refs, rope):
    if rope:
        (x_ref, mod_ref, g1_ref, w_ref, qg_ref, kg_ref, seg_ref, cdft_ref,
         cos_ref, sin_ref, q_ref, k_ref, v_ref, y_ref, f_scr) = refs
    else:
        (x_ref, mod_ref, g1_ref, w_ref, qg_ref, kg_ref, seg_ref, cdft_ref,
         q_ref, k_ref, v_ref, y_ref, f_scr) = refs
    tm = x_ref.shape[1]
    sub = f_scr.shape[1]
    swap_idx = lax.broadcasted_iota(jnp.int32, (sub, LANES), 1) ^ (ROPE_AXIS_DIM // 2)

    qk_cols = 2 * ATTN_WIDTH

    gain_scale = g1_ref[...] * (1.0 + mod_ref[1:2, :])

    def norm_mod(r0):
        x = x_ref[0, r0:r0 + sub, :]
        ms = jnp.mean(x * x, axis=-1, keepdims=True)
        return (x * lax.rsqrt(ms + EPS) * gain_scale + mod_ref[0:1, :]).astype(BF16)

    def project_qk(h):
        return jnp.dot(h, w_ref[:, 0:qk_cols], preferred_element_type=F32)

    def project_vf(h):
        return jnp.dot(h, w_ref[:, qk_cols:], preferred_element_type=F32)

    def norm_rope(t, gain, out_ref, r0):
        for cb in range(ATTN_WIDTH // MXU_DIM):
            tc = t[:, MXU_DIM * cb:MXU_DIM * (cb + 1)]
            ms = jnp.dot((tc * tc).astype(BF16), seg_ref[...], preferred_element_type=F32)
            tn = tc * lax.rsqrt(ms + EPS) * gain
            for hb in range(MXU_DIM // LANES):
                u = tn[:, LANES * hb:LANES * (hb + 1)]
                if rope:
                    partner = jnp.take_along_axis(u, swap_idx, axis=1)
                    u = u * cos_ref[r0:r0 + sub, :] + partner * sin_ref[r0:r0 + sub, :]
                c0 = MXU_DIM * cb + LANES * hb
                out_ref[0, r0:r0 + sub, c0:c0 + LANES] = u.astype(BF16)

    def emit_qk(r0, z):
        norm_rope(z[:, 0:ATTN_WIDTH], qg_ref[...], q_ref, r0)
        norm_rope(z[:, ATTN_WIDTH:], kg_ref[...], k_ref, r0)

    def emit_vf(r0, z):
        v_ref[0, r0:r0 + sub, :] = z[:, 0:ATTN_WIDTH].astype(BF16)
        w = FOURIER_WIDTH
        h0 = r0 // 2
        for g in range(N_FGROUPS):
            c0 = FGROUP_DIM * g
            f_scr[g] = z[:, ATTN_WIDTH + c0:ATTN_WIDTH + c0 + FGROUP_DIM]
            for parity in range(2):
                fp = f_scr[g, pl.ds(parity, sub // 2, stride=2), :].astype(BF16)
                yg = jnp.dot(fp, cdft_ref[g], preferred_element_type=F32)
                base = 2 * w * parity
                y_ref[0, h0:h0 + sub // 2, base + c0:base + c0 + FGROUP_DIM] = (
                    yg[:, :FGROUP_DIM].astype(BF16))
                y_ref[0, h0:h0 + sub // 2, base + w + c0:base + w + c0 + FGROUP_DIM] = (
                    yg[:, FGROUP_DIM:].astype(BF16))

    starts = list(range(0, tm, sub))
    h = norm_mod(starts[0])
    z_qk = project_qk(h)
    pending_vf = None
    for t, r0 in enumerate(starts):
        if pending_vf is not None:
            emit_vf(*pending_vf)
        pending_vf = (r0, project_vf(h))
        emit_qk(r0, z_qk)
        if t + 1 < len(starts):
            h = norm_mod(starts[t + 1])
            z_qk = project_qk(h)
    emit_vf(*pending_vf)


def _project(x, mod_all, g1, w_in, qg, kg, ones, cdft, rope_tabs, *, layer, w_layer, mod_row, tm, name):
    b, l, d = x.shape
    rope = rope_tabs is not None
    row_of = (lambda i, t: i) if mod_row is None else (lambda i, t: mod_row)
    tok = lambda width: pl.BlockSpec((1, tm, width), lambda i, t: (i, t, 0))
    in_specs = [tok(d),
                _mod_spec(layer, row_of),
                _resident((1, d), layer),
                _resident((d, IN_COLS), w_layer),
                _resident((1, MXU_DIM), layer),
                _resident((1, MXU_DIM), layer),
                _resident((MXU_DIM, MXU_DIM)),
                _resident((N_FGROUPS, FGROUP_DIM, 2 * FGROUP_DIM), layer)]
    args = [x, mod_all, g1, w_in, qg, kg, ones, cdft]
    if rope:
        in_specs += [pl.BlockSpec((tm, LANES), lambda i, t: (t, 0))] * 2
        args += list(rope_tabs)
    out = lambda width: jax.ShapeDtypeStruct((b, l, width), BF16)
    return pl.pallas_call(
        functools.partial(_proj_kernel, rope=rope),
        grid=(b, l // tm),
        in_specs=in_specs,
        out_specs=[tok(ATTN_WIDTH), tok(ATTN_WIDTH), tok(ATTN_WIDTH),
                   pl.BlockSpec((1, tm // 2, 4 * FOURIER_WIDTH), lambda i, t: (i, t, 0))],
        out_shape=[out(ATTN_WIDTH), out(ATTN_WIDTH), out(ATTN_WIDTH),
                   jax.ShapeDtypeStruct((b, l // 2, 4 * FOURIER_WIDTH), BF16)],
        scratch_shapes=[pltpu.VMEM((N_FGROUPS, min(tm, MXU_DIM), FGROUP_DIM), F32)],
        compiler_params=_params("arbitrary", "arbitrary"),
        name=name,
    )(*args)


def _proj_kv_kernel(x_ref, mod_ref, g1_ref, wk_ref, wv_ref, kg_ref, seg_ref, k_ref, v_ref, *, sub):
    gain_scale = g1_ref[...] * (1.0 + mod_ref[1:2, :])
    for r0 in range(0, x_ref.shape[1], sub):
        x = x_ref[0, r0:r0 + sub, :]
        ms = jnp.mean(x * x, axis=-1, keepdims=True)
        h = (x * lax.rsqrt(ms + EPS) * gain_scale + mod_ref[0:1, :]).astype(BF16)
        zk = jnp.dot(h, wk_ref[...], preferred_element_type=F32)
        v_ref[0, r0:r0 + sub, :] = jnp.dot(h, wv_ref[...], preferred_element_type=F32).astype(BF16)
        for cb in range(ATTN_WIDTH // MXU_DIM):
            cols = pl.ds(MXU_DIM * cb, MXU_DIM)
            tc = zk[:, MXU_DIM * cb:MXU_DIM * (cb + 1)]
            seg_ms = jnp.dot((tc * tc).astype(BF16), seg_ref[...], preferred_element_type=F32)
            k_ref[0, r0:r0 + sub, cols] = (tc * lax.rsqrt(seg_ms + EPS) * kg_ref[...]).astype(BF16)


def _project_kv(x, mod_all, g1, w_in, kg, seg, *, layer, w_layer, mod_row, tm, name):
    b, l, d = x.shape
    tok = lambda width: pl.BlockSpec((1, tm, width), lambda i, t: (i, t, 0))
    w_cols = lambda blk: pl.BlockSpec((None, d, ATTN_WIDTH), lambda i, t: (w_layer, 0, blk),
                                      pipeline_mode=pl.Buffered(1))
    out = jax.ShapeDtypeStruct((b, l, ATTN_WIDTH), BF16)
    return pl.pallas_call(
        functools.partial(_proj_kv_kernel, sub=min(tm, MXU_DIM)),
        grid=(b, l // tm),
        in_specs=[tok(d), _mod_spec(layer, lambda i, t: mod_row), _resident((1, d), layer),
                  w_cols(1), w_cols(2),
                  _resident((1, MXU_DIM), layer), _resident((MXU_DIM, MXU_DIM))],
        out_specs=[tok(ATTN_WIDTH), tok(ATTN_WIDTH)],
        out_shape=[out, out],
        compiler_params=_params("arbitrary", "arbitrary"),
        name=name,
    )(x, mod_all, g1, w_in, w_in, kg, seg)


def _attn_kernel(*refs, n_seg, n_cast, lambda_init, sub_rows):
    bound_ref, q_ref = refs[0], refs[1]
    k_refs = refs[2:2 + n_seg]
    v_refs = refs[2 + n_seg:2 + 2 * n_seg]
    lam_ref, sg_ref = refs[2 + 2 * n_seg:4 + 2 * n_seg]
    cast_in = refs[4 + 2 * n_seg:4 + 2 * n_seg + n_cast]
    o_ref = refs[4 + 2 * n_seg + n_cast]
    cast_out = refs[5 + 2 * n_seg + n_cast:]


    def attend(bounded):
        wide_cache.clear()
        for src, dst in zip(cast_in, cast_out):
            dst[...] = src[...].astype(BF16)
        lv = lam_ref[...]
        lam = (jnp.exp(jnp.sum(lv[0:1] * lv[1:2], axis=-1, keepdims=True))
               - jnp.exp(jnp.sum(lv[2:3] * lv[3:4], axis=-1, keepdims=True)) + lambda_init)
        lane = lax.broadcasted_iota(jnp.int32, (1, LANES), 1)
        first = (lane < HEAD_DIM).astype(BF16)
        items = [(pl.ds(LANES * h, LANES), pl.ds(r0, sub_rows))
                 for h in range(q_ref.shape[2] // LANES) for r0 in range(0, q_ref.shape[1], sub_rows)]
        probs = scores(items[0], first, bounded)
        for t, item in enumerate(items):
            nxt = scores(items[t + 1], first, bounded) if t + 1 < len(items) else None
            finish(item, probs, lam)
            probs = nxt

    def scores(item, first, bounded):
        cols, rows = item
        q = q_ref[0, rows, cols]

        def one_map(qm):
            ss = [lax.dot_general(qm, k[0, :, cols], _NT, preferred_element_type=F32) for k in k_refs]
            if bounded:
                ps = [jnp.exp2(s) for s in ss]
            else:
                m = functools.reduce(jnp.maximum, [jnp.max(s, axis=-1, keepdims=True) for s in ss])
                ps = [jnp.exp2(s - m) for s in ss]
            l = functools.reduce(lambda a, c: a + c, [jnp.sum(p, axis=-1, keepdims=True) for p in ps])
            return [p.astype(BF16) for p in ps], l

        return one_map(q * first), one_map(q * (1.0 - first).astype(BF16))

    wide_cache = {}

    def wide_v(cols):
        if cols.start not in wide_cache:
            wide_cache[cols.start] = [
                jnp.concatenate([v[0, :, cols]] * (MXU_DIM // LANES), axis=1) for v in v_refs]
        return wide_cache[cols.start]

    def finish(item, probs, lam):
        cols, rows = item
        (p1, l1), (p2, l2) = probs
        c = (lam * l1 / l2).astype(BF16)
        o = functools.reduce(lambda a, b: a + b, [
            jnp.dot(pa - c * pb, vw, preferred_element_type=F32)[:, :LANES]
            for pa, pb, vw in zip(p1, p2, wide_v(cols))])
        o = o * (1.0 / l1)
        ms = jnp.mean(o * o, axis=-1, keepdims=True)
        o = o * lax.rsqrt(ms + EPS) * sg_ref[...] * (1.0 - lambda_init)
        o_ref[0, rows, cols] = o.astype(BF16)

    safe = bound_ref[0, 0] <= SAFE_SCORE_BOUND
    pl.when(safe)(lambda: attend(True))
    pl.when(jnp.logical_not(safe))(lambda: attend(False))


def _attention(bound, q, ks, vs, lam_rows, subln_g, casts=(), *, layer, lambda_init, tq, heads, name):
    b, lq, _ = q.shape
    n_seg = len(ks)
    width = heads * LANES
    grid = (b, N_HEADS // heads, lq // tq)
    qo_spec = pl.BlockSpec((1, tq, width), lambda i, h, t: (i, t, h))
    kv_specs = [pl.BlockSpec((1, k.shape[1], width), lambda i, h, t: (i, 0, h)) for k in ks]

    n_steps = grid[0] * grid[1] * grid[2]
    step_of = lambda i, h, t: (i * grid[1] + h) * grid[2] + t
    cast_in_specs, cast_out_specs, cast_shapes = [], [], []
    for w, w_layer in casts:
        _, r, c = w.shape
        per = next(p for p in range(1, n_steps + 1)
                   if n_steps % p == 0 and r % (n_steps // p) == 0
                   and (r // (n_steps // p)) % BF16_SUBLANES == 0)
        rows = r // (n_steps // per)
        cast_in_specs.append(pl.BlockSpec(
            (None, rows, c), lambda i, h, t, w_layer=w_layer, per=per: (w_layer, step_of(i, h, t) // per, 0)))
        cast_out_specs.append(pl.BlockSpec(
            (None, rows, c), lambda i, h, t, per=per: (0, step_of(i, h, t) // per, 0)))
        cast_shapes.append(jax.ShapeDtypeStruct((1, r, c), BF16))

    out = pl.pallas_call(
        functools.partial(_attn_kernel, n_seg=n_seg, n_cast=len(casts), lambda_init=lambda_init,
                          sub_rows=min(tq, MXU_DIM)),
        grid=grid,
        in_specs=[pl.BlockSpec(memory_space=pltpu.SMEM), qo_spec] + kv_specs + kv_specs
                 + [_resident((4, HEAD_DIM), layer), _resident((1, V_HEAD_DIM), layer)] + cast_in_specs,
        out_specs=[qo_spec] + cast_out_specs,
        out_shape=[jax.ShapeDtypeStruct((b, lq, ATTN_WIDTH), BF16)] + cast_shapes,
        compiler_params=_params("arbitrary", "arbitrary", "arbitrary"),
        name=name,
    )(bound, q, *ks, *vs, lam_rows, subln_g, *[w for w, _ in casts])
    return out[0], out[1:]


def _fold_kernel(cdft_ref, wf_ref, o_ref):
    for g in range(N_FGROUPS):
        wg = wf_ref[g].astype(BF16)
        for part in range(2):
            cols = pl.ds(FGROUP_DIM * part, FGROUP_DIM)
            o_ref[g, :, cols] = jnp.dot(cdft_ref[:, cols], wg, preferred_element_type=F32).astype(BF16)


def _fold_group_map(cdft, w_fourier):
    depth = w_fourier.shape[0]
    return pl.pallas_call(
        _fold_kernel,
        grid=(depth,),
        in_specs=[_resident((FGROUP_DIM, 2 * FGROUP_DIM)),
                  pl.BlockSpec((None, N_FGROUPS, FGROUP_DIM, FGROUP_DIM), lambda i: (i, 0, 0, 0))],
        out_specs=pl.BlockSpec((None, N_FGROUPS, FGROUP_DIM, 2 * FGROUP_DIM), lambda i: (i, 0, 0, 0)),
        out_shape=jax.ShapeDtypeStruct((depth, N_FGROUPS, FGROUP_DIM, 2 * FGROUP_DIM), BF16),
        compiler_params=_params("arbitrary"),
        name="fold_group_map",
    )(cdft, w_fourier)


def _fourier_kernel(y_ref, ce_ref, se_ref, co_ref, so_ref, o_ref):
    w = FOURIER_WIDTH
    e = (jnp.dot(ce_ref[...], y_ref[0, :, 0:w], preferred_element_type=F32)
         + jnp.dot(se_ref[...], y_ref[0, :, w:2 * w], preferred_element_type=F32))
    o = (jnp.dot(co_ref[...], y_ref[0, :, 2 * w:3 * w], preferred_element_type=F32)
         + jnp.dot(so_ref[...], y_ref[0, :, 3 * w:4 * w], preferred_element_type=F32))
    o_ref[0, 0] = (e + o).astype(BF16)
    o_ref[0, 1] = (e - o).astype(BF16)


def _fourier(y2, pos_mats, *, name):
    b, half, _ = y2.shape
    out = pl.pallas_call(
        _fourier_kernel,
        grid=(b,),
        in_specs=[pl.BlockSpec((1, half, 4 * FOURIER_WIDTH), lambda i: (i, 0, 0))]
                 + [_resident((half, half))] * 4,
        out_specs=pl.BlockSpec((1, 2, half, FOURIER_WIDTH), lambda i: (i, 0, 0, 0)),
        out_shape=jax.ShapeDtypeStruct((b, 2, half, FOURIER_WIDTH), BF16),
        compiler_params=_params("arbitrary"),
        name=name,
    )(y2, *pos_mats)
    return out.reshape(b, 2 * half, FOURIER_WIDTH)


def _ffn_kernel(x_ref, a_ref, f_ref, mod_ref, g2_ref, wo_ref, wg_ref, wu_ref, wd_ref, o_ref, *, sub):
    tm = x_ref.shape[1]

    def mix_norm(r0):
        rows = pl.ds(r0, sub)
        mix = (jnp.dot(a_ref[0, rows, :], wo_ref[0:ATTN_WIDTH, :], preferred_element_type=F32)
               + jnp.dot(f_ref[0, rows, :], wo_ref[ATTN_WIDTH:, :], preferred_element_type=F32))
        x1 = x_ref[0, rows, :] + mod_ref[2:3, :] * mix
        ms = jnp.mean(x1 * x1, axis=-1, keepdims=True)
        h = x1 * lax.rsqrt(ms + EPS) * g2_ref[...]
        h = (h * (1.0 + mod_ref[4:5, :]) + mod_ref[3:4, :]).astype(BF16)
        return x1, h

    def swiglu(r0, x1, h):
        gate = jnp.dot(h, wg_ref[...], preferred_element_type=F32)
        up = jnp.dot(h, wu_ref[...], preferred_element_type=F32)
        act = (gate / (1.0 + jnp.exp(-gate)) * up).astype(BF16)
        y = jnp.dot(act, wd_ref[...], preferred_element_type=F32)
        o_ref[0, pl.ds(r0, sub), :] = x1 + mod_ref[5:6, :] * y

    starts = list(range(0, tm, sub))
    cur = mix_norm(starts[0])
    for t, r0 in enumerate(starts):
        nxt = mix_norm(starts[t + 1]) if t + 1 < len(starts) else None
        swiglu(r0, *cur)
        cur = nxt


def _out_ffn(x, attn, four, mod_all, g2, w_out, w_gate, w_up, w_down, *, layer, w_layer, mod_row, tm, name):
    b, l, d = x.shape
    d_ff = w_gate.shape[-1]
    row_of = (lambda i, t: i) if mod_row is None else (lambda i, t: mod_row)
    tok = lambda width: pl.BlockSpec((1, tm, width), lambda i, t: (i, t, 0))
    return pl.pallas_call(
        functools.partial(_ffn_kernel, sub=min(tm, MXU_DIM)),
        grid=(b, l // tm),
        in_specs=[tok(d), tok(ATTN_WIDTH), tok(FOURIER_WIDTH),
                  _mod_spec(layer, row_of),
                  _resident((1, d), layer),
                  _resident((d, d), w_layer),
                  _resident((d, d_ff), w_layer),
                  _resident((d, d_ff), w_layer),
                  _resident((d_ff, d), w_layer)],
        out_specs=tok(d),
        out_shape=jax.ShapeDtypeStruct((b, l, d), F32),
        compiler_params=_params("arbitrary", "arbitrary"),
        name=name,
    )(x, attn, four, mod_all, g2, w_out, w_gate, w_up, w_down)


def kernel(x, c, ctx, c_ctx, w_ada, b_ada, norm1_g, norm2_g, w_in, q_norm_g, k_norm_g, lambda_q1, lambda_k1, lambda_q2, lambda_k2, subln_g, w_fourier, w_out, w_gate, w_up, w_down):
    depth = w_ada.shape[0]
    b, l, d = x.shape
    lc = ctx.shape[1]
    assert b + 1 <= MOD_ROWS and lc % 2 == 0
    tm_ctx = math.gcd(b * lc, 1024)

    rope_tabs = _rope_tables(l)
    cdft = _fold_group_map(_channel_dft(), w_fourier)
    pos_lat = _position_dft(l)
    pos_ctx = _position_dft(lc)
    ones = _segment_mean()

    cvec = jnp.zeros((MOD_ROWS, d), F32).at[:b].set(c).at[b].set(c_ctx)
    mod_all = _adaln_mod(cvec, w_ada, b_ada).reshape(depth, MOD_ROWS, 6, d)

    w_in_b = w_in[:1].astype(BF16)
    g1, g2 = norm1_g.reshape(depth, 1, d), norm2_g.reshape(depth, 1, d)
    tile_gain = lambda g: jnp.tile(g, (1, MXU_DIM // HEAD_DIM)).reshape(depth, 1, MXU_DIM)
    qg, kg = tile_gain(q_norm_g * (HEAD_DIM ** -0.5 * LOG2_E)), tile_gain(k_norm_g)
    lam_rows = jnp.stack([lambda_q1, lambda_k1, lambda_q2, lambda_k2], axis=1)
    sg = subln_g.reshape(depth, 1, V_HEAD_DIM)
    score_bound = (HEAD_DIM ** 0.5 * LOG2_E) * jnp.max(jnp.abs(q_norm_g), axis=1) * jnp.max(jnp.abs(k_norm_g), axis=1)

    for i in range(depth):
        last = i == depth - 1
        lambda_init = 0.8 - 0.6 * math.exp(-0.3 * i)
        bound = score_bound[i].reshape(1, 1)
        proj = functools.partial(_project, mod_all=mod_all, g1=g1, w_in=w_in_b, qg=qg, kg=kg,
                                 ones=ones, cdft=cdft, layer=i, w_layer=0)
        attend = functools.partial(_attention, bound, lam_rows=lam_rows, subln_g=sg, layer=i,
                                   lambda_init=lambda_init)

        flat = lambda a: a.reshape(1, b * lc, a.shape[-1])
        if last:
            kc, vc = _project_kv(flat(ctx), mod_all, g1, w_in_b, kg, ones, layer=i, w_layer=0,
                                 mod_row=b, tm=tm_ctx, name="proj_ctx_kv")
            kc, vc = (a.reshape(b, lc, ATTN_WIDTH) for a in (kc, vc))
        else:
            qc, kc, vc, yc = proj(flat(ctx), rope_tabs=None, mod_row=b, tm=tm_ctx, name="proj_ctx")
            qc, kc, vc = (a.reshape(b, lc, ATTN_WIDTH) for a in (qc, kc, vc))
            yc = yc.reshape(b, lc // 2, 4 * FOURIER_WIDTH)
        qx, kx, vx, yx = proj(x, rope_tabs=rope_tabs, mod_row=None, tm=1024, name="proj_lat")

        casts = [(w, i) for w in (w_out, w_gate, w_up, w_down)] + ([] if last else [(w_in, i + 1)])
        attn_x, rounded = attend(qx, [kc, kx], [vc, vx], casts=casts, tq=l, heads=1, name="attn_lat")
        ffn = functools.partial(_out_ffn, mod_all=mod_all, g2=g2, w_out=rounded[0], w_gate=rounded[1],
                                w_up=rounded[2], w_down=rounded[3], layer=i, w_layer=0)
        if not last:
            w_in_b = rounded[4]
        four_x = _fourier(yx, pos_lat, name="fourier_lat")
        x_new = ffn(x, attn_x, four_x, mod_row=None, tm=1024, name="out_ffn_lat")
        if not last:
            attn_c, _ = attend(qc, [kc], [vc], tq=lc, heads=N_HEADS, name="attn_ctx")
            four_c = _fourier(yc, pos_ctx, name="fourier_ctx")
            ctx = ffn(flat(ctx), flat(attn_c), flat(four_c), mod_row=b, tm=tm_ctx,
                      name="out_ffn_ctx").reshape(b, lc, d)
        x = x_new
    return x
```

```python
import functools
import math

import numpy as np
import jax
import jax.numpy as jnp
from jax import lax
from jax.experimental import pallas as pl
from jax.experimental.pallas import tpu as pltpu

D_MODEL = 1024
GRID_W = 64
ATTN_WIDTH = 512
FOURIER_WIDTH = 512
HEAD_DIM = 64
N_HEADS = 4
V_HEAD_DIM = 128
N_FGROUPS = 4
FGROUP_DIM = 128
IN_COLS = 2048
ROPE_AXIS_DIM = 32
ROPE_THETA = 10000.0
EPS = 1e-6

LANES = 128
MXU_DIM = 256
BF16_SUBLANES = 16
MOD_ROWS = 16
VMEM_LIMIT_BYTES = 56 * 1024 * 1024
SAFE_SCORE_BOUND = 40.0
LOG2_E = 1.4426950408889634

F32 = jnp.float32
BF16 = jnp.bfloat16
_NT = (((1,), (1,)), ((), ()))


def _params(*sem):
    return pltpu.CompilerParams(dimension_semantics=sem, vmem_limit_bytes=VMEM_LIMIT_BYTES)


def _resident(shape, layer=None):
    zeros = (0,) * len(shape)
    if layer is None:
        return pl.BlockSpec(shape, lambda *_: zeros, pipeline_mode=pl.Buffered(1))
    return pl.BlockSpec((None,) + tuple(shape), lambda *_: (layer,) + zeros, pipeline_mode=pl.Buffered(1))


def _mod_spec(layer, row_of):
    return pl.BlockSpec((None, None, 6, D_MODEL), lambda *g: (layer, row_of(*g), 0, 0))


def _rope_tables(n_tokens):
    rows = n_tokens // GRID_W
    row = np.repeat(np.arange(rows, dtype=np.float64), GRID_W)
    col = np.tile(np.arange(GRID_W, dtype=np.float64), rows)
    inv = np.float32(ROPE_THETA) ** (-np.arange(0, ROPE_AXIS_DIM, 2, dtype=np.float32) / np.float32(ROPE_AXIS_DIM))
    inv = inv.astype(np.float64)
    half = ROPE_AXIS_DIM // 2
    d = np.arange(HEAD_DIM)
    pos = np.where(d[None, :] < ROPE_AXIS_DIM, row[:, None], col[:, None])
    ang = pos * inv[d % half][None, :]
    low = (d % ROPE_AXIS_DIM) < half
    cos = np.cos(ang)
    sin = np.where(low[None, :], -np.sin(ang), np.sin(ang))
    tile2 = lambda a: jnp.asarray(np.tile(a, (1, LANES // HEAD_DIM)), F32)
    return tile2(cos), tile2(sin)


def _channel_dft():
    n = FGROUP_DIM
    k = (np.arange(n)[:, None] * np.arange(n)[None, :]) % n
    ang = 2.0 * np.pi * k / n
    m = np.concatenate([np.cos(ang), np.sin(ang)], axis=1) / np.sqrt(n)
    return jnp.asarray(m, F32).astype(BF16)


def _position_dft(n_tokens):
    half = n_tokens // 2
    lo = np.arange(half)[:, None]
    m = np.arange(half)[None, :]
    mats = []
    for s in (0, 1):
        k = ((2 * m + s) * lo) % n_tokens
        ang = 2.0 * np.pi * k / n_tokens
        mats.append(jnp.asarray(np.cos(ang) / np.sqrt(n_tokens), F32).astype(BF16))
        mats.append(jnp.asarray(-np.sin(ang) / np.sqrt(n_tokens), F32).astype(BF16))
    return mats


def _segment_mean():
    seg = np.arange(MXU_DIM) // HEAD_DIM
    return jnp.asarray((seg[:, None] == seg[None, :]) / HEAD_DIM, BF16)


def _mod_kernel(c_ref, w_ref, b_ref, o_ref):
    cv = c_ref[...]
    s = cv / (1.0 + jnp.exp(-cv))
    o_ref[0] = jnp.dot(s.astype(BF16), w_ref[0].astype(BF16), preferred_element_type=F32) + b_ref[0]


def _adaln_mod(cvec, w_ada, b_ada):
    depth, d, n = w_ada.shape
    tn = 1536
    return pl.pallas_call(
        _mod_kernel,
        grid=(depth, n // tn),
        in_specs=[pl.BlockSpec((MOD_ROWS, d), lambda i, j: (0, 0)),
                  pl.BlockSpec((1, d, tn), lambda i, j: (i, 0, j)),
                  pl.BlockSpec((1, 1, tn), lambda i, j: (i, 0, j))],
        out_specs=pl.BlockSpec((1, MOD_ROWS, tn), lambda i, j: (i, 0, j)),
        out_shape=jax.ShapeDtypeStruct((depth, MOD_ROWS, n), F32),
        compiler_params=_params("arbitrary", "arbitrary"),
        name="adaln_mod",
    )(cvec, w_ada, b_ada.reshape(depth, 1, n))


def _proj_kernel(*refs, rope):
    if rope:
        (x_ref, mod_ref, g1_ref, w_ref, qg_ref, kg_ref, seg_ref, cdft_ref,
         cos_ref, sin_ref, q_ref, k_ref, v_ref, y_ref, f_scr) = refs
    else:
        (x_ref, mod_ref, g1_ref, w_ref, qg_ref, kg_ref, seg_ref, cdft_ref,
         q_ref, k_ref, v_ref, y_ref, f_scr) = refs
    tm = x_ref.shape[1]
    sub = f_scr.shape[1]
    swap_idx = lax.broadcasted_iota(jnp.int32, (sub, LANES), 1) ^ (ROPE_AXIS_DIM // 2)

    qk_cols = 2 * ATTN_WIDTH

    gain_scale = g1_ref[...] * (1.0 + mod_ref[1:2, :])

    def norm_mod(r0):
        x = x_ref[0, r0:r0 + sub, :]
        ms = jnp.mean(x * x, axis=-1, keepdims=True)
        return (x * lax.rsqrt(ms + EPS) * gain_scale + mod_ref[0:1, :]).astype(BF16)

    def project_qk(h):
        return jnp.dot(h, w_ref[:, 0:qk_cols], preferred_element_type=F32)

    def project_vf(h):
        return jnp.dot(h, w_ref[:, qk_cols:], preferred_element_type=F32)

    def norm_rope(t, gain, out_ref, r0):
        for cb in range(ATTN_WIDTH // MXU_DIM):
            tc = t[:, MXU_DIM * cb:MXU_DIM * (cb + 1)]
            ms = jnp.dot((tc * tc).astype(BF16), seg_ref[...], preferred_element_type=F32)
            tn = tc * lax.rsqrt(ms + EPS) * gain
            for hb in range(MXU_DIM // LANES):
                u = tn[:, LANES * hb:LANES * (hb + 1)]
                if rope:
                    partner = jnp.take_along_axis(u, swap_idx, axis=1)
                    u = u * cos_ref[r0:r0 + sub, :] + partner * sin_ref[r0:r0 + sub, :]
                c0 = MXU_DIM * cb + LANES * hb
                out_ref[0, r0:r0 + sub, c0:c0 + LANES] = u.astype(BF16)

    def emit_qk(r0, z):
        norm_rope(z[:, 0:ATTN_WIDTH], qg_ref[...], q_ref, r0)
        norm_rope(z[:, ATTN_WIDTH:], kg_ref[...], k_ref, r0)

    def emit_vf(r0, z):
        v_ref[0, r0:r0 + sub, :] = z[:, 0:ATTN_WIDTH].astype(BF16)
        w = FOURIER_WIDTH
        h0 = r0 // 2
        for g in range(N_FGROUPS):
            c0 = FGROUP_DIM * g
            f_scr[g] = z[:, ATTN_WIDTH + c0:ATTN_WIDTH + c0 + FGROUP_DIM]
            for parity in range(2):
                fp = f_scr[g, pl.ds(parity, sub // 2, stride=2), :].astype(BF16)
                yg = jnp.dot(fp, cdft_ref[g], preferred_element_type=F32)
                base = 2 * w * parity
                y_ref[0, h0:h0 + sub // 2, base + c0:base + c0 + FGROUP_DIM] = (
                    yg[:, :FGROUP_DIM].astype(BF16))
                y_ref[0, h0:h0 + sub // 2, base + w + c0:base + w + c0 + FGROUP_DIM] = (
                    yg[:, FGROUP_DIM:].astype(BF16))

    starts = list(range(0, tm, sub))
    h = norm_mod(starts[0])
    z_qk = project_qk(h)
    pending_vf = None
    for t, r0 in enumerate(starts):
        if pending_vf is not None:
            emit_vf(*pending_vf)
        pending_vf = (r0, project_vf(h))
        emit_qk(r0, z_qk)
        if t + 1 < len(starts):
            h = norm_mod(starts[t + 1])
            z_qk = project_qk(h)
    emit_vf(*pending_vf)


def _project(x, mod_all, g1, w_in, qg, kg, ones, cdft, rope_tabs, *, layer, w_layer, mod_row, tm, name):
    b, l, d = x.shape
    rope = rope_tabs is not None
    row_of = (lambda i, t: i) if mod_row is None else (lambda i, t: mod_row)
    tok = lambda width: pl.BlockSpec((1, tm, width), lambda i, t: (i, t, 0))
    in_specs = [tok(d),
                _mod_spec(layer, row_of),
                _resident((1, d), layer),
                _resident((d, IN_COLS), w_layer),
                _resident((1, MXU_DIM), layer),
                _resident((1, MXU_DIM), layer),
                _resident((MXU_DIM, MXU_DIM)),
                _resident((N_FGROUPS, FGROUP_DIM, 2 * FGROUP_DIM), layer)]
    args = [x, mod_all, g1, w_in, qg, kg, ones, cdft]
    if rope:
        in_specs += [pl.BlockSpec((tm, LANES), lambda i, t: (t, 0))] * 2
        args += list(rope_tabs)
    out = lambda width: jax.ShapeDtypeStruct((b, l, width), BF16)
    return pl.pallas_call(
        functools.partial(_proj_kernel, rope=rope),
        grid=(b, l // tm),
        in_specs=in_specs,
        out_specs=[tok(ATTN_WIDTH), tok(ATTN_WIDTH), tok(ATTN_WIDTH),
                   pl.BlockSpec((1, tm // 2, 4 * FOURIER_WIDTH), lambda i, t: (i, t, 0))],
        out_shape=[out(ATTN_WIDTH), out(ATTN_WIDTH), out(ATTN_WIDTH),
                   jax.ShapeDtypeStruct((b, l // 2, 4 * FOURIER_WIDTH), BF16)],
        scratch_shapes=[pltpu.VMEM((N_FGROUPS, min(tm, MXU_DIM), FGROUP_DIM), F32)],
        compiler_params=_params("arbitrary", "arbitrary"),
        name=name,
    )(*args)


def _proj_kv_kernel(x_ref, mod_ref, g1_ref, wk_ref, wv_ref, kg_ref, seg_ref, k_ref, v_ref, *, sub):
    gain_scale = g1_ref[...] * (1.0 + mod_ref[1:2, :])
    for r0 in range(0, x_ref.shape[1], sub):
        x = x_ref[0, r0:r0 + sub, :]
        ms = jnp.mean(x * x, axis=-1, keepdims=True)
        h = (x * lax.rsqrt(ms + EPS) * gain_scale + mod_ref[0:1, :]).astype(BF16)
        zk = jnp.dot(h, wk_ref[...], preferred_element_type=F32)
        v_ref[0, r0:r0 + sub, :] = jnp.dot(h, wv_ref[...], preferred_element_type=F32).astype(BF16)
        for cb in range(ATTN_WIDTH // MXU_DIM):
            cols = pl.ds(MXU_DIM * cb, MXU_DIM)
            tc = zk[:, MXU_DIM * cb:MXU_DIM * (cb + 1)]
            seg_ms = jnp.dot((tc * tc).astype(BF16), seg_ref[...], preferred_element_type=F32)
            k_ref[0, r0:r0 + sub, cols] = (tc * lax.rsqrt(seg_ms + EPS) * kg_ref[...]).astype(BF16)


def _project_kv(x, mod_all, g1, w_in, kg, seg, *, layer, w_layer, mod_row, tm, name):
    b, l, d = x.shape
    tok = lambda width: pl.BlockSpec((1, tm, width), lambda i, t: (i, t, 0))
    w_cols = lambda blk: pl.BlockSpec((None, d, ATTN_WIDTH), lambda i, t: (w_layer, 0, blk),
                                      pipeline_mode=pl.Buffered(1))
    out = jax.ShapeDtypeStruct((b, l, ATTN_WIDTH), BF16)
    return pl.pallas_call(
        functools.partial(_proj_kv_kernel, sub=min(tm, MXU_DIM)),
        grid=(b, l // tm),
        in_specs=[tok(d), _mod_spec(layer, lambda i, t: mod_row), _resident((1, d), layer),
                  w_cols(1), w_cols(2),
                  _resident((1, MXU_DIM), layer), _resident((MXU_DIM, MXU_DIM))],
        out_specs=[tok(ATTN_WIDTH), tok(ATTN_WIDTH)],
        out_shape=[out, out],
        compiler_params=_params("arbitrary", "arbitrary"),
        name=name,
    )(x, mod_all, g1, w_in, w_in, kg, seg)


def _attn_kernel(*refs, n_seg, n_cast, lambda_init, sub_rows):
    bound_ref, q_ref = refs[0], refs[1]
    k_refs = refs[2:2 + n_seg]
    v_refs = refs[2 + n_seg:2 + 2 * n_seg]
    lam_ref, sg_ref = refs[2 + 2 * n_seg:4 + 2 * n_seg]
    cast_in = refs[4 + 2 * n_seg:4 + 2 * n_seg + n_cast]
    o_ref = refs[4 + 2 * n_seg + n_cast]
    cast_out = refs[5 + 2 * n_seg + n_cast:]


    def attend(bounded):
        wide_cache.clear()
        for src, dst in zip(cast_in, cast_out):
            dst[...] = src[...].astype(BF16)
        lv = lam_ref[...]
        lam = (jnp.exp(jnp.sum(lv[0:1] * lv[1:2], axis=-1, keepdims=True))
               - jnp.exp(jnp.sum(lv[2:3] * lv[3:4], axis=-1, keepdims=True)) + lambda_init)
        lane = lax.broadcasted_iota(jnp.int32, (1, LANES), 1)
        first = (lane < HEAD_DIM).astype(BF16)
        items = [(pl.ds(LANES * h, LANES), pl.ds(r0, sub_rows))
                 for h in range(q_ref.shape[2] // LANES) for r0 in range(0, q_ref.shape[1], sub_rows)]
        probs = scores(items[0], first, bounded)
        for t, item in enumerate(items):
            nxt = scores(items[t + 1], first, bounded) if t + 1 < len(items) else None
            finish(item, probs, lam)
            probs = nxt

    def scores(item, first, bounded):
        cols, rows = item
        q = q_ref[0, rows, cols]

        def one_map(qm):
            ss = [lax.dot_general(qm, k[0, :, cols], _NT, preferred_element_type=F32) for k in k_refs]
            if bounded:
                ps = [jnp.exp2(s) for s in ss]
            else:
                m = functools.reduce(jnp.maximum, [jnp.max(s, axis=-1, keepdims=True) for s in ss])
                ps = [jnp.exp2(s - m) for s in ss]
            l = functools.reduce(lambda a, c: a + c, [jnp.sum(p, axis=-1, keepdims=True) for p in ps])
            return [p.astype(BF16) for p in ps], l

        return one_map(q * first), one_map(q * (1.0 - first).astype(BF16))

    wide_cache = {}

    def wide_v(cols):
        if cols.start not in wide_cache:
            wide_cache[cols.start] = [
                jnp.concatenate([v[0, :, cols]] * (MXU_DIM // LANES), axis=1) for v in v_refs]
        return wide_cache[cols.start]

    def finish(item, probs, lam):
        cols, rows = item
        (p1, l1), (p2, l2) = probs
        c = (lam * l1 / l2).astype(BF16)
        o = functools.reduce(lambda a, b: a + b, [
            jnp.dot(pa - c * pb, vw, preferred_element_type=F32)[:, :LANES]
            for pa, pb, vw in zip(p1, p2, wide_v(cols))])
        o = o * (1.0 / l1)
        ms = jnp.mean(o * o, axis=-1, keepdims=True)
        o = o * lax.rsqrt(ms + EPS) * sg_ref[...] * (1.0 - lambda_init)
        o_ref[0, rows, cols] = o.astype(BF16)

    safe = bound_ref[0, 0] <= SAFE_SCORE_BOUND
    pl.when(safe)(lambda: attend(True))
    pl.when(jnp.logical_not(safe))(lambda: attend(False))


def _attention(bound, q, ks, vs, lam_rows, subln_g, casts=(), *, layer, lambda_init, tq, heads, name):
    b, lq, _ = q.shape
    n_seg = len(ks)
    width = heads * LANES
    grid = (b, N_HEADS // heads, lq // tq)
    qo_spec = pl.BlockSpec((1, tq, width), lambda i, h, t: (i, t, h))
    kv_specs = [pl.BlockSpec((1, k.shape[1], width), lambda i, h, t: (i, 0, h)) for k in ks]

    n_steps = grid[0] * grid[1] * grid[2]
    step_of = lambda i, h, t: (i * grid[1] + h) * grid[2] + t
    cast_in_specs, cast_out_specs, cast_shapes = [], [], []
    for w, w_layer in casts:
        _, r, c = w.shape
        per = next(p for p in range(1, n_steps + 1)
                   if n_steps % p == 0 and r % (n_steps // p) == 0
                   and (r // (n_steps // p)) % BF16_SUBLANES == 0)
        rows = r // (n_steps // per)
        cast_in_specs.append(pl.BlockSpec(
            (None, rows, c), lambda i, h, t, w_layer=w_layer, per=per: (w_layer, step_of(i, h, t) // per, 0)))
        cast_out_specs.append(pl.BlockSpec(
            (None, rows, c), lambda i, h, t, per=per: (0, step_of(i, h, t) // per, 0)))
        cast_shapes.append(jax.ShapeDtypeStruct((1, r, c), BF16))

    out = pl.pallas_call(
        functools.partial(_attn_kernel, n_seg=n_seg, n_cast=len(casts), lambda_init=lambda_init,
                          sub_rows=min(tq, MXU_DIM)),
        grid=grid,
        in_specs=[pl.BlockSpec(memory_space=pltpu.SMEM), qo_spec] + kv_specs + kv_specs
                 + [_resident((4, HEAD_DIM), layer), _resident((1, V_HEAD_DIM), layer)] + cast_in_specs,
        out_specs=[qo_spec] + cast_out_specs,
        out_shape=[jax.ShapeDtypeStruct((b, lq, ATTN_WIDTH), BF16)] + cast_shapes,
        compiler_params=_params("arbitrary", "arbitrary", "arbitrary"),
        name=name,
    )(bound, q, *ks, *vs, lam_rows, subln_g, *[w for w, _ in casts])
    return out[0], out[1:]


def _fold_kernel(cdft_ref, wf_ref, o_ref):
    for g in range(N_FGROUPS):
        wg = wf_ref[g].astype(BF16)
        for part in range(2):
            cols = pl.ds(FGROUP_DIM * part, FGROUP_DIM)
            o_ref[g, :, cols] = jnp.dot(cdft_ref[:, cols], wg, preferred_element_type=F32).astype(BF16)


def _fold_group_map(cdft, w_fourier):
    depth = w_fourier.shape[0]
    return pl.pallas_call(
        _fold_kernel,
        grid=(depth,),
        in_specs=[_resident((FGROUP_DIM, 2 * FGROUP_DIM)),
                  pl.BlockSpec((None, N_FGROUPS, FGROUP_DIM, FGROUP_DIM), lambda i: (i, 0, 0, 0))],
        out_specs=pl.BlockSpec((None, N_FGROUPS, FGROUP_DIM, 2 * FGROUP_DIM), lambda i: (i, 0, 0, 0)),
        out_shape=jax.ShapeDtypeStruct((depth, N_FGROUPS, FGROUP_DIM, 2 * FGROUP_DIM), BF16),
        compiler_params=_params("arbitrary"),
        name="fold_group_map",
    )(cdft, w_fourier)


def _fourier_kernel(y_ref, ce_ref, se_ref, co_ref, so_ref, o_ref):
    w = FOURIER_WIDTH
    e = (jnp.dot(ce_ref[...], y_ref[0, :, 0:w], preferred_element_type=F32)
         + jnp.dot(se_ref[...], y_ref[0, :, w:2 * w], preferred_element_type=F32))
    o = (jnp.dot(co_ref[...], y_ref[0, :, 2 * w:3 * w], preferred_element_type=F32)
         + jnp.dot(so_ref[...], y_ref[0, :, 3 * w:4 * w], preferred_element_type=F32))
    o_ref[0, 0] = (e + o).astype(BF16)
    o_ref[0, 1] = (e - o).astype(BF16)


def _fourier(y2, pos_mats, *, name):
    b, half, _ = y2.shape
    out = pl.pallas_call(
        _fourier_kernel,
        grid=(b,),
        in_specs=[pl.BlockSpec((1, half, 4 * FOURIER_WIDTH), lambda i: (i, 0, 0))]
                 + [_resident((half, half))] * 4,
        out_specs=pl.BlockSpec((1, 2, half, FOURIER_WIDTH), lambda i: (i, 0, 0, 0)),
        out_shape=jax.ShapeDtypeStruct((b, 2, half, FOURIER_WIDTH), BF16),
        compiler_params=_params("arbitrary"),
        name=name,
    )(y2, *pos_mats)
    return out.reshape(b, 2 * half, FOURIER_WIDTH)


def _ffn_kernel(x_ref, a_ref, f_ref, mod_ref, g2_ref, wo_ref, wg_ref, wu_ref, wd_ref, o_ref, *, sub):
    tm = x_ref.shape[1]

    def mix_norm(r0):
        rows = pl.ds(r0, sub)
        mix = (jnp.dot(a_ref[0, rows, :], wo_ref[0:ATTN_WIDTH, :], preferred_element_type=F32)
               + jnp.dot(f_ref[0, rows, :], wo_ref[ATTN_WIDTH:, :], preferred_element_type=F32))
        x1 = x_ref[0, rows, :] + mod_ref[2:3, :] * mix
        ms = jnp.mean(x1 * x1, axis=-1, keepdims=True)
        h = x1 * lax.rsqrt(ms + EPS) * g2_ref[...]
        h = (h * (1.0 + mod_ref[4:5, :]) + mod_ref[3:4, :]).astype(BF16)
        return x1, h

    def swiglu(r0, x1, h):
        gate = jnp.dot(h, wg_ref[...], preferred_element_type=F32)
        up = jnp.dot(h, wu_ref[...], preferred_element_type=F32)
        act = (gate / (1.0 + jnp.exp(-gate)) * up).astype(BF16)
        y = jnp.dot(act, wd_ref[...], preferred_element_type=F32)
        o_ref[0, pl.ds(r0, sub), :] = x1 + mod_ref[5:6, :] * y

    starts = list(range(0, tm, sub))
    cur = mix_norm(starts[0])
    for t, r0 in enumerate(starts):
        nxt = mix_norm(starts[t + 1]) if t + 1 < len(starts) else None
        swiglu(r0, *cur)
        cur = nxt


def _out_ffn(x, attn, four, mod_all, g2, w_out, w_gate, w_up, w_down, *, layer, w_layer, mod_row, tm, name):
    b, l, d = x.shape
    d_ff = w_gate.shape[-1]
    row_of = (lambda i, t: i) if mod_row is None else (lambda i, t: mod_row)
    tok = lambda width: pl.BlockSpec((1, tm, width), lambda i, t: (i, t, 0))
    return pl.pallas_call(
        functools.partial(_ffn_kernel, sub=min(tm, MXU_DIM)),
        grid=(b, l // tm),
        in_specs=[tok(d), tok(ATTN_WIDTH), tok(FOURIER_WIDTH),
                  _mod_spec(layer, row_of),
                  _resident((1, d), layer),
                  _resident((d, d), w_layer),
                  _resident((d, d_ff), w_layer),
                  _resident((d, d_ff), w_layer),
                  _resident((d_ff, d), w_layer)],
        out_specs=tok(d),
        out_shape=jax.ShapeDtypeStruct((b, l, d), F32),
        compiler_params=_params("arbitrary", "arbitrary"),
        name=name,
    )(x, attn, four, mod_all, g2, w_out, w_gate, w_up, w_down)


def kernel(x, c, ctx, c_ctx, w_ada, b_ada, norm1_g, norm2_g, w_in, q_norm_g, k_norm_g, lambda_q1, lambda_k1, lambda_q2, lambda_k2, subln_g, w_fourier, w_out, w_gate, w_up, w_down):
    depth = w_ada.shape[0]
    b, l, d = x.shape
    lc = ctx.shape[1]
    assert b + 1 <= MOD_ROWS and lc % 2 == 0
    tm_ctx = math.gcd(b * lc, 1024)

    rope_tabs = _rope_tables(l)
    cdft = _fold_group_map(_channel_dft(), w_fourier)
    pos_lat = _position_dft(l)
    pos_ctx = _position_dft(lc)
    ones = _segment_mean()

    cvec = jnp.zeros((MOD_ROWS, d), F32).at[:b].set(c).at[b].set(c_ctx)
    mod_all = _adaln_mod(cvec, w_ada, b_ada).reshape(depth, MOD_ROWS, 6, d)

    w_in_b = w_in.astype(BF16)
    g1, g2 = norm1_g.reshape(depth, 1, d), norm2_g.reshape(depth, 1, d)
    tile_gain = lambda g: jnp.tile(g, (1, MXU_DIM // HEAD_DIM)).reshape(depth, 1, MXU_DIM)
    qg, kg = tile_gain(q_norm_g * (HEAD_DIM ** -0.5 * LOG2_E)), tile_gain(k_norm_g)
    lam_rows = jnp.stack([lambda_q1, lambda_k1, lambda_q2, lambda_k2], axis=1)
    sg = subln_g.reshape(depth, 1, V_HEAD_DIM)
    score_bound = (HEAD_DIM ** 0.5 * LOG2_E) * jnp.max(jnp.abs(q_norm_g), axis=1) * jnp.max(jnp.abs(k_norm_g), axis=1)

    for i in range(depth):
        last = i == depth - 1
        lambda_init = 0.8 - 0.6 * math.exp(-0.3 * i)
        bound = score_bound[i].reshape(1, 1)
        proj = functools.partial(_project, mod_all=mod_all, g1=g1, w_in=w_in_b, qg=qg, kg=kg,
                                 ones=ones, cdft=cdft, layer=i, w_layer=i)
        attend = functools.partial(_attention, bound, lam_rows=lam_rows, subln_g=sg, layer=i,
                                   lambda_init=lambda_init)

        flat = lambda a: a.reshape(1, b * lc, a.shape[-1])
        if last:
            kc, vc = _project_kv(flat(ctx), mod_all, g1, w_in_b, kg, ones, layer=i, w_layer=i,
                                 mod_row=b, tm=tm_ctx, name="proj_ctx_kv")
            kc, vc = (a.reshape(b, lc, ATTN_WIDTH) for a in (kc, vc))
        else:
            qc, kc, vc, yc = proj(flat(ctx), rope_tabs=None, mod_row=b, tm=tm_ctx, name="proj_ctx")
            qc, kc, vc = (a.reshape(b, lc, ATTN_WIDTH) for a in (qc, kc, vc))
            yc = yc.reshape(b, lc // 2, 4 * FOURIER_WIDTH)
        qx, kx, vx, yx = proj(x, rope_tabs=rope_tabs, mod_row=None, tm=1024, name="proj_lat")

        casts = [(w, i) for w in (w_out, w_gate, w_up, w_down)]
        attn_x, rounded = attend(qx, [kc, kx], [vc, vx], casts=casts, tq=l, heads=1, name="attn_lat")
        ffn = functools.partial(_out_ffn, mod_all=mod_all, g2=g2, w_out=rounded[0], w_gate=rounded[1],
                                w_up=rounded[2], w_down=rounded[3], layer=i, w_layer=0)
        four_x = _fourier(yx, pos_lat, name="fourier_lat")
        x_new = ffn(x, attn_x, four_x, mod_row=None, tm=1024, name="out_ffn_lat")
        if not last:
            attn_c, _ = attend(qc, [kc], [vc], tq=lc, heads=N_HEADS, name="attn_ctx")
            four_c = _fourier(yc, pos_ctx, name="fourier_ctx")
            ctx = ffn(flat(ctx), flat(attn_c), flat(four_c), mod_row=b, tm=tm_ctx,
                      name="out_ffn_ctx").reshape(b, lc, d)
        x = x_new
    return x
```

```python
import functools
import math

import numpy as np
import jax
import jax.numpy as jnp
from jax import lax
from jax.experimental import pallas as pl
from jax.experimental.pallas import tpu as pltpu

D_MODEL = 1024
GRID_W = 64
ATTN_WIDTH = 512
FOURIER_WIDTH = 512
HEAD_DIM = 64
N_HEADS = 4
V_HEAD_DIM = 128
N_FGROUPS = 4
FGROUP_DIM = 128
IN_COLS = 2048
ROPE_AXIS_DIM = 32
ROPE_THETA = 10000.0
EPS = 1e-6

LANES = 128
MXU_DIM = 256
BF16_SUBLANES = 16
MOD_ROWS = 16
VMEM_LIMIT_BYTES = 56 * 1024 * 1024
SAFE_SCORE_BOUND = 40.0
LOG2_E = 1.4426950408889634

F32 = jnp.float32
BF16 = jnp.bfloat16
_NT = (((1,), (1,)), ((), ()))


def _params(*sem):
    return pltpu.CompilerParams(dimension_semantics=sem, vmem_limit_bytes=VMEM_LIMIT_BYTES)


def _resident(shape, layer=None):
    zeros = (0,) * len(shape)
    if layer is None:
        return pl.BlockSpec(shape, lambda *_: zeros, pipeline_mode=pl.Buffered(1))
    return pl.BlockSpec((None,) + tuple(shape), lambda *_: (layer,) + zeros, pipeline_mode=pl.Buffered(1))


def _mod_spec(layer, row_of):
    return pl.BlockSpec((None, None, 6, D_MODEL), lambda *g: (layer, row_of(*g), 0, 0))


def _rope_tables(n_tokens):
    rows = n_tokens // GRID_W
    row = np.repeat(np.arange(rows, dtype=np.float64), GRID_W)
    col = np.tile(np.arange(GRID_W, dtype=np.float64), rows)
    inv = np.float32(ROPE_THETA) ** (-np.arange(0, ROPE_AXIS_DIM, 2, dtype=np.float32) / np.float32(ROPE_AXIS_DIM))
    inv = inv.astype(np.float64)
    half = ROPE_AXIS_DIM // 2
    d = np.arange(HEAD_DIM)
    pos = np.where(d[None, :] < ROPE_AXIS_DIM, row[:, None], col[:, None])
    ang = pos * inv[d % half][None, :]
    low = (d % ROPE_AXIS_DIM) < half
    cos = np.cos(ang)
    sin = np.where(low[None, :], -np.sin(ang), np.sin(ang))
    tile2 = lambda a: jnp.asarray(np.tile(a, (1, LANES // HEAD_DIM)), F32)
    return tile2(cos), tile2(sin)


def _channel_dft():
    n = FGROUP_DIM
    k = (np.arange(n)[:, None] * np.arange(n)[None, :]) % n
    ang = 2.0 * np.pi * k / n
    m = np.concatenate([np.cos(ang), np.sin(ang)], axis=1) / np.sqrt(n)
    return jnp.asarray(m, F32).astype(BF16)


def _position_dft(n_tokens):
    half = n_tokens // 2
    lo = np.arange(half)[:, None]
    m = np.arange(half)[None, :]
    mats = []
    for s in (0, 1):
        k = ((2 * m + s) * lo) % n_tokens
        ang = 2.0 * np.pi * k / n_tokens
        mats.append(jnp.asarray(np.cos(ang) / np.sqrt(n_tokens), F32).astype(BF16))
        mats.append(jnp.asarray(-np.sin(ang) / np.sqrt(n_tokens), F32).astype(BF16))
    return mats


def _segment_mean():
    seg = np.arange(MXU_DIM) // HEAD_DIM
    return jnp.asarray((seg[:, None] == seg[None, :]) / HEAD_DIM, BF16)


def _mod_kernel(c_ref, w_ref, b_ref, o_ref):
    cv = c_ref[...]
    s = cv / (1.0 + jnp.exp(-cv))
    o_ref[0] = jnp.dot(s.astype(BF16), w_ref[0].astype(BF16), preferred_element_type=F32) + b_ref[0]


def _adaln_mod(cvec, w_ada, b_ada):
    depth, d, n = w_ada.shape
    tn = 3072
    return pl.pallas_call(
        _mod_kernel,
        grid=(depth, n // tn),
        in_specs=[pl.BlockSpec((MOD_ROWS, d), lambda i, j: (0, 0)),
                  pl.BlockSpec((1, d, tn), lambda i, j: (i, 0, j)),
                  pl.BlockSpec((1, 1, tn), lambda i, j: (i, 0, j))],
        out_specs=pl.BlockSpec((1, MOD_ROWS, tn), lambda i, j: (i, 0, j)),
        out_shape=jax.ShapeDtypeStruct((depth, MOD_ROWS, n), F32),
        compiler_params=_params("arbitrary", "arbitrary"),
        name="adaln_mod",
    )(cvec, w_ada, b_ada.reshape(depth, 1, n))


def _proj_kernel(*refs, rope):
    if rope:
        (x_ref, mod_ref, g1_ref, w_ref, qg_ref, kg_ref, seg_ref, cdft_ref,
         cos_ref, sin_ref, q_ref, k_ref, v_ref, y_ref, f_scr) = refs
    else:
        (x_ref, mod_ref, g1_ref, w_ref, qg_ref, kg_ref, seg_ref, cdft_ref,
         q_ref, k_ref, v_ref, y_ref, f_scr) = refs
    tm = x_ref.shape[1]
    sub = f_scr.shape[1]
    swap_idx = lax.broadcasted_iota(jnp.int32, (sub, LANES), 1) ^ (ROPE_AXIS_DIM // 2)

    qk_cols = 2 * ATTN_WIDTH

    gain_scale = g1_ref[...] * (1.0 + mod_ref[1:2, :])

    def norm_mod(r0):
        x = x_ref[0, r0:r0 + sub, :]
        ms = jnp.mean(x * x, axis=-1, keepdims=True)
        return (x * lax.rsqrt(ms + EPS) * gain_scale + mod_ref[0:1, :]).astype(BF16)

    def project_qk(h):
        return jnp.dot(h, w_ref[:, 0:qk_cols], preferred_element_type=F32)

    def project_vf(h):
        return jnp.dot(h, w_ref[:, qk_cols:], preferred_element_type=F32)

    def norm_rope(t, gain, out_ref, r0):
        for cb in range(ATTN_WIDTH // MXU_DIM):
            tc = t[:, MXU_DIM * cb:MXU_DIM * (cb + 1)]
            ms = jnp.dot((tc * tc).astype(BF16), seg_ref[...], preferred_element_type=F32)
            tn = tc * lax.rsqrt(ms + EPS) * gain
            for hb in range(MXU_DIM // LANES):
                u = tn[:, LANES * hb:LANES * (hb + 1)]
                if rope:
                    partner = jnp.take_along_axis(u, swap_idx, axis=1)
                    u = u * cos_ref[r0:r0 + sub, :] + partner * sin_ref[r0:r0 + sub, :]
                c0 = MXU_DIM * cb + LANES * hb
                out_ref[0, r0:r0 + sub, c0:c0 + LANES] = u.astype(BF16)

    def emit_qk(r0, z):
        norm_rope(z[:, 0:ATTN_WIDTH], qg_ref[...], q_ref, r0)
        norm_rope(z[:, ATTN_WIDTH:], kg_ref[...], k_ref, r0)

    def emit_vf(r0, z):
        v_ref[0, r0:r0 + sub, :] = z[:, 0:ATTN_WIDTH].astype(BF16)
        w = FOURIER_WIDTH
        h0 = r0 // 2
        for g in range(N_FGROUPS):
            c0 = FGROUP_DIM * g
            f_scr[g] = z[:, ATTN_WIDTH + c0:ATTN_WIDTH + c0 + FGROUP_DIM]
            for parity in range(2):
                fp = f_scr[g, pl.ds(parity, sub // 2, stride=2), :].astype(BF16)
                yg = jnp.dot(fp, cdft_ref[g], preferred_element_type=F32)
                base = 2 * w * parity
                y_ref[0, h0:h0 + sub // 2, base + c0:base + c0 + FGROUP_DIM] = (
                    yg[:, :FGROUP_DIM].astype(BF16))
                y_ref[0, h0:h0 + sub // 2, base + w + c0:base + w + c0 + FGROUP_DIM] = (
                    yg[:, FGROUP_DIM:].astype(BF16))

    starts = list(range(0, tm, sub))
    h = norm_mod(starts[0])
    z_qk = project_qk(h)
    pending_vf = None
    for t, r0 in enumerate(starts):
        if pending_vf is not None:
            emit_vf(*pending_vf)
        pending_vf = (r0, project_vf(h))
        emit_qk(r0, z_qk)
        if t + 1 < len(starts):
            h = norm_mod(starts[t + 1])
            z_qk = project_qk(h)
    emit_vf(*pending_vf)


def _project(x, mod_all, g1, w_in, qg, kg, ones, cdft, rope_tabs, *, layer, w_layer, mod_row, tm, name):
    b, l, d = x.shape
    rope = rope_tabs is not None
    row_of = (lambda i, t: i) if mod_row is None else (lambda i, t: mod_row)
    tok = lambda width: pl.BlockSpec((1, tm, width), lambda i, t: (i, t, 0))
    in_specs = [tok(d),
                _mod_spec(layer, row_of),
                _resident((1, d), layer),
                _resident((d, IN_COLS), w_layer),
                _resident((1, MXU_DIM), layer),
                _resident((1, MXU_DIM), layer),
                _resident((MXU_DIM, MXU_DIM)),
                _resident((N_FGROUPS, FGROUP_DIM, 2 * FGROUP_DIM), layer)]
    args = [x, mod_all, g1, w_in, qg, kg, ones, cdft]
    if rope:
        in_specs += [pl.BlockSpec((tm, LANES), lambda i, t: (t, 0))] * 2
        args += list(rope_tabs)
    out = lambda width: jax.ShapeDtypeStruct((b, l, width), BF16)
    return pl.pallas_call(
        functools.partial(_proj_kernel, rope=rope),
        grid=(b, l // tm),
        in_specs=in_specs,
        out_specs=[tok(ATTN_WIDTH), tok(ATTN_WIDTH), tok(ATTN_WIDTH),
                   pl.BlockSpec((1, tm // 2, 4 * FOURIER_WIDTH), lambda i, t: (i, t, 0))],
        out_shape=[out(ATTN_WIDTH), out(ATTN_WIDTH), out(ATTN_WIDTH),
                   jax.ShapeDtypeStruct((b, l // 2, 4 * FOURIER_WIDTH), BF16)],
        scratch_shapes=[pltpu.VMEM((N_FGROUPS, min(tm, MXU_DIM), FGROUP_DIM), F32)],
        compiler_params=_params("arbitrary", "arbitrary"),
        name=name,
    )(*args)


def _proj_kv_kernel(x_ref, mod_ref, g1_ref, wk_ref, wv_ref, kg_ref, seg_ref, k_ref, v_ref, *, sub):
    gain_scale = g1_ref[...] * (1.0 + mod_ref[1:2, :])
    for r0 in range(0, x_ref.shape[1], sub):
        x = x_ref[0, r0:r0 + sub, :]
        ms = jnp.mean(x * x, axis=-1, keepdims=True)
        h = (x * lax.rsqrt(ms + EPS) * gain_scale + mod_ref[0:1, :]).astype(BF16)
        zk = jnp.dot(h, wk_ref[...], preferred_element_type=F32)
        v_ref[0, r0:r0 + sub, :] = jnp.dot(h, wv_ref[...], preferred_element_type=F32).astype(BF16)
        for cb in range(ATTN_WIDTH // MXU_DIM):
            cols = pl.ds(MXU_DIM * cb, MXU_DIM)
            tc = zk[:, MXU_DIM * cb:MXU_DIM * (cb + 1)]
            seg_ms = jnp.dot((tc * tc).astype(BF16), seg_ref[...], preferred_element_type=F32)
            k_ref[0, r0:r0 + sub, cols] = (tc * lax.rsqrt(seg_ms + EPS) * kg_ref[...]).astype(BF16)


def _project_kv(x, mod_all, g1, w_in, kg, seg, *, layer, w_layer, mod_row, tm, name):
    b, l, d = x.shape
    tok = lambda width: pl.BlockSpec((1, tm, width), lambda i, t: (i, t, 0))
    w_cols = lambda blk: pl.BlockSpec((None, d, ATTN_WIDTH), lambda i, t: (w_layer, 0, blk),
                                      pipeline_mode=pl.Buffered(1))
    out = jax.ShapeDtypeStruct((b, l, ATTN_WIDTH), BF16)
    return pl.pallas_call(
        functools.partial(_proj_kv_kernel, sub=min(tm, MXU_DIM)),
        grid=(b, l // tm),
        in_specs=[tok(d), _mod_spec(layer, lambda i, t: mod_row), _resident((1, d), layer),
                  w_cols(1), w_cols(2),
                  _resident((1, MXU_DIM), layer), _resident((MXU_DIM, MXU_DIM))],
        out_specs=[tok(ATTN_WIDTH), tok(ATTN_WIDTH)],
        out_shape=[out, out],
        compiler_params=_params("arbitrary", "arbitrary"),
        name=name,
    )(x, mod_all, g1, w_in, w_in, kg, seg)


def _attn_kernel(*refs, n_seg, n_cast, lambda_init, sub_rows):
    bound_ref, q_ref = refs[0], refs[1]
    k_refs = refs[2:2 + n_seg]
    v_refs = refs[2 + n_seg:2 + 2 * n_seg]
    lam_ref, sg_ref = refs[2 + 2 * n_seg:4 + 2 * n_seg]
    cast_in = refs[4 + 2 * n_seg:4 + 2 * n_seg + n_cast]
    o_ref = refs[4 + 2 * n_seg + n_cast]
    cast_out = refs[5 + 2 * n_seg + n_cast:]


    def attend(bounded):
        wide_cache.clear()
        for src, dst in zip(cast_in, cast_out):
            dst[...] = src[...].astype(BF16)
        lv = lam_ref[...]
        lam = (jnp.exp(jnp.sum(lv[0:1] * lv[1:2], axis=-1, keepdims=True))
               - jnp.exp(jnp.sum(lv[2:3] * lv[3:4], axis=-1, keepdims=True)) + lambda_init)
        lane = lax.broadcasted_iota(jnp.int32, (1, LANES), 1)
        first = (lane < HEAD_DIM).astype(BF16)
        items = [(pl.ds(LANES * h, LANES), pl.ds(r0, sub_rows))
                 for h in range(q_ref.shape[2] // LANES) for r0 in range(0, q_ref.shape[1], sub_rows)]
        probs = scores(items[0], first, bounded)
        for t, item in enumerate(items):
            nxt = scores(items[t + 1], first, bounded) if t + 1 < len(items) else None
            finish(item, probs, lam)
            probs = nxt

    def scores(item, first, bounded):
        cols, rows = item
        q = q_ref[0, rows, cols]

        def one_map(qm):
            ss = [lax.dot_general(qm, k[0, :, cols], _NT, preferred_element_type=F32) for k in k_refs]
            if bounded:
                ps = [jnp.exp2(s) for s in ss]
            else:
                m = functools.reduce(jnp.maximum, [jnp.max(s, axis=-1, keepdims=True) for s in ss])
                ps = [jnp.exp2(s - m) for s in ss]
            l = functools.reduce(lambda a, c: a + c, [jnp.sum(p, axis=-1, keepdims=True) for p in ps])
            return [p.astype(BF16) for p in ps], l

        return one_map(q * first), one_map(q * (1.0 - first).astype(BF16))

    wide_cache = {}

    def wide_v(cols):
        if cols.start not in wide_cache:
            wide_cache[cols.start] = [
                jnp.concatenate([v[0, :, cols]] * (MXU_DIM // LANES), axis=1) for v in v_refs]
        return wide_cache[cols.start]

    def finish(item, probs, lam):
        cols, rows = item
        (p1, l1), (p2, l2) = probs
        c = (lam * l1 / l2).astype(BF16)
        o = functools.reduce(lambda a, b: a + b, [
            jnp.dot(pa - c * pb, vw, preferred_element_type=F32)[:, :LANES]
            for pa, pb, vw in zip(p1, p2, wide_v(cols))])
        o = o * (1.0 / l1)
        ms = jnp.mean(o * o, axis=-1, keepdims=True)
        o = o * lax.rsqrt(ms + EPS) * sg_ref[...] * (1.0 - lambda_init)
        o_ref[0, rows, cols] = o.astype(BF16)

    safe = bound_ref[0, 0] <= SAFE_SCORE_BOUND
    pl.when(safe)(lambda: attend(True))
    pl.when(jnp.logical_not(safe))(lambda: attend(False))


def _attention(bound, q, ks, vs, lam_rows, subln_g, casts=(), *, layer, lambda_init, tq, heads, name):
    b, lq, _ = q.shape
    n_seg = len(ks)
    width = heads * LANES
    grid = (b, N_HEADS // heads, lq // tq)
    qo_spec = pl.BlockSpec((1, tq, width), lambda i, h, t: (i, t, h))
    kv_specs = [pl.BlockSpec((1, k.shape[1], width), lambda i, h, t: (i, 0, h)) for k in ks]

    n_steps = grid[0] * grid[1] * grid[2]
    step_of = lambda i, h, t: (i * grid[1] + h) * grid[2] + t
    cast_in_specs, cast_out_specs, cast_shapes = [], [], []
    for w, w_layer in casts:
        _, r, c = w.shape
        per = next(p for p in range(1, n_steps + 1)
                   if n_steps % p == 0 and r % (n_steps // p) == 0
                   and (r // (n_steps // p)) % BF16_SUBLANES == 0)
        rows = r // (n_steps // per)
        cast_in_specs.append(pl.BlockSpec(
            (None, rows, c), lambda i, h, t, w_layer=w_layer, per=per: (w_layer, step_of(i, h, t) // per, 0)))
        cast_out_specs.append(pl.BlockSpec(
            (None, rows, c), lambda i, h, t, per=per: (0, step_of(i, h, t) // per, 0)))
        cast_shapes.append(jax.ShapeDtypeStruct((1, r, c), BF16))

    out = pl.pallas_call(
        functools.partial(_attn_kernel, n_seg=n_seg, n_cast=len(casts), lambda_init=lambda_init,
                          sub_rows=min(tq, MXU_DIM)),
        grid=grid,
        in_specs=[pl.BlockSpec(memory_space=pltpu.SMEM), qo_spec] + kv_specs + kv_specs
                 + [_resident((4, HEAD_DIM), layer), _resident((1, V_HEAD_DIM), layer)] + cast_in_specs,
        out_specs=[qo_spec] + cast_out_specs,
        out_shape=[jax.ShapeDtypeStruct((b, lq, ATTN_WIDTH), BF16)] + cast_shapes,
        compiler_params=_params("arbitrary", "arbitrary", "arbitrary"),
        name=name,
    )(bound, q, *ks, *vs, lam_rows, subln_g, *[w for w, _ in casts])
    return out[0], out[1:]


def _fold_kernel(cdft_ref, wf_ref, o_ref):
    for g in range(N_FGROUPS):
        wg = wf_ref[g].astype(BF16)
        for part in range(2):
            cols = pl.ds(FGROUP_DIM * part, FGROUP_DIM)
            o_ref[g, :, cols] = jnp.dot(cdft_ref[:, cols], wg, preferred_element_type=F32).astype(BF16)


def _fold_group_map(cdft, w_fourier):
    depth = w_fourier.shape[0]
    return pl.pallas_call(
        _fold_kernel,
        grid=(depth,),
        in_specs=[_resident((FGROUP_DIM, 2 * FGROUP_DIM)),
                  pl.BlockSpec((None, N_FGROUPS, FGROUP_DIM, FGROUP_DIM), lambda i: (i, 0, 0, 0))],
        out_specs=pl.BlockSpec((None, N_FGROUPS, FGROUP_DIM, 2 * FGROUP_DIM), lambda i: (i, 0, 0, 0)),
        out_shape=jax.ShapeDtypeStruct((depth, N_FGROUPS, FGROUP_DIM, 2 * FGROUP_DIM), BF16),
        compiler_params=_params("arbitrary"),
        name="fold_group_map",
    )(cdft, w_fourier)


def _fourier_kernel(y_ref, ce_ref, se_ref, co_ref, so_ref, o_ref):
    w = FOURIER_WIDTH
    e = (jnp.dot(ce_ref[...], y_ref[0, :, 0:w], preferred_element_type=F32)
         + jnp.dot(se_ref[...], y_ref[0, :, w:2 * w], preferred_element_type=F32))
    o = (jnp.dot(co_ref[...], y_ref[0, :, 2 * w:3 * w], preferred_element_type=F32)
         + jnp.dot(so_ref[...], y_ref[0, :, 3 * w:4 * w], preferred_element_type=F32))
    o_ref[0, 0] = (e + o).astype(BF16)
    o_ref[0, 1] = (e - o).astype(BF16)


def _fourier(y2, pos_mats, *, name):
    b, half, _ = y2.shape
    out = pl.pallas_call(
        _fourier_kernel,
        grid=(b,),
        in_specs=[pl.BlockSpec((1, half, 4 * FOURIER_WIDTH), lambda i: (i, 0, 0))]
                 + [_resident((half, half))] * 4,
        out_specs=pl.BlockSpec((1, 2, half, FOURIER_WIDTH), lambda i: (i, 0, 0, 0)),
        out_shape=jax.ShapeDtypeStruct((b, 2, half, FOURIER_WIDTH), BF16),
        compiler_params=_params("arbitrary"),
        name=name,
    )(y2, *pos_mats)
    return out.reshape(b, 2 * half, FOURIER_WIDTH)


def _ffn_kernel(x_ref, a_ref, f_ref, mod_ref, g2_ref, wo_ref, wg_ref, wu_ref, wd_ref, o_ref, *, sub):
    tm = x_ref.shape[1]
    gain_scale = g2_ref[...] * (1.0 + mod_ref[4:5, :])

    def mix_norm(r0):
        rows = pl.ds(r0, sub)
        mix = (jnp.dot(a_ref[0, rows, :], wo_ref[0:ATTN_WIDTH, :], preferred_element_type=F32)
               + jnp.dot(f_ref[0, rows, :], wo_ref[ATTN_WIDTH:, :], preferred_element_type=F32))
        x1 = x_ref[0, rows, :] + mod_ref[2:3, :] * mix
        ms = jnp.mean(x1 * x1, axis=-1, keepdims=True)
        h = (x1 * lax.rsqrt(ms + EPS) * gain_scale + mod_ref[3:4, :]).astype(BF16)
        return x1, h

    def swiglu(r0, x1, h):
        gate = jnp.dot(h, wg_ref[...], preferred_element_type=F32)
        up = jnp.dot(h, wu_ref[...], preferred_element_type=F32)
        act = (gate / (1.0 + jnp.exp(-gate)) * up).astype(BF16)
        y = jnp.dot(act, wd_ref[...], preferred_element_type=F32)
        o_ref[0, pl.ds(r0, sub), :] = x1 + mod_ref[5:6, :] * y

    starts = list(range(0, tm, sub))
    cur = mix_norm(starts[0])
    for t, r0 in enumerate(starts):
        nxt = mix_norm(starts[t + 1]) if t + 1 < len(starts) else None
        swiglu(r0, *cur)
        cur = nxt


def _out_ffn(x, attn, four, mod_all, g2, w_out, w_gate, w_up, w_down, *, layer, w_layer, mod_row, tm, name):
    b, l, d = x.shape
    d_ff = w_gate.shape[-1]
    row_of = (lambda i, t: i) if mod_row is None else (lambda i, t: mod_row)
    tok = lambda width: pl.BlockSpec((1, tm, width), lambda i, t: (i, t, 0))
    return pl.pallas_call(
        functools.partial(_ffn_kernel, sub=min(tm, MXU_DIM)),
        grid=(b, l // tm),
        in_specs=[tok(d), tok(ATTN_WIDTH), tok(FOURIER_WIDTH),
                  _mod_spec(layer, row_of),
                  _resident((1, d), layer),
                  _resident((d, d), w_layer),
                  _resident((d, d_ff), w_layer),
                  _resident((d, d_ff), w_layer),
                  _resident((d_ff, d), w_layer)],
        out_specs=tok(d),
        out_shape=jax.ShapeDtypeStruct((b, l, d), F32),
        compiler_params=_params("arbitrary", "arbitrary"),
        name=name,
    )(x, attn, four, mod_all, g2, w_out, w_gate, w_up, w_down)


def kernel(x, c, ctx, c_ctx, w_ada, b_ada, norm1_g, norm2_g, w_in, q_norm_g, k_norm_g, lambda_q1, lambda_k1, lambda_q2, lambda_k2, subln_g, w_fourier, w_out, w_gate, w_up, w_down):
    depth = w_ada.shape[0]
    b, l, d = x.shape
    lc = ctx.shape[1]
    assert b + 1 <= MOD_ROWS and lc % 2 == 0
    tm_ctx = math.gcd(b * lc, 512)

    rope_tabs = _rope_tables(l)
    cdft = _fold_group_map(_channel_dft(), w_fourier)
    pos_lat = _position_dft(l)
    pos_ctx = _position_dft(lc)
    ones = _segment_mean()

    cvec = jnp.zeros((MOD_ROWS, d), F32).at[:b].set(c).at[b].set(c_ctx)
    mod_all = _adaln_mod(cvec, w_ada, b_ada).reshape(depth, MOD_ROWS, 6, d)

    w_in_b = w_in.astype(BF16)
    g1, g2 = norm1_g.reshape(depth, 1, d), norm2_g.reshape(depth, 1, d)
    tile_gain = lambda g: jnp.tile(g, (1, MXU_DIM // HEAD_DIM)).reshape(depth, 1, MXU_DIM)
    qg, kg = tile_gain(q_norm_g * (HEAD_DIM ** -0.5 * LOG2_E)), tile_gain(k_norm_g)
    lam_rows = jnp.stack([lambda_q1, lambda_k1, lambda_q2, lambda_k2], axis=1)
    sg = subln_g.reshape(depth, 1, V_HEAD_DIM)
    score_bound = (HEAD_DIM ** 0.5 * LOG2_E) * jnp.max(jnp.abs(q_norm_g), axis=1) * jnp.max(jnp.abs(k_norm_g), axis=1)

    for i in range(depth):
        last = i == depth - 1
        lambda_init = 0.8 - 0.6 * math.exp(-0.3 * i)
        bound = score_bound[i].reshape(1, 1)
        proj = functools.partial(_project, mod_all=mod_all, g1=g1, w_in=w_in_b, qg=qg, kg=kg,
                                 ones=ones, cdft=cdft, layer=i, w_layer=i)
        attend = functools.partial(_attention, bound, lam_rows=lam_rows, subln_g=sg, layer=i,
                                   lambda_init=lambda_init)

        flat = lambda a: a.reshape(1, b * lc, a.shape[-1])
        if last:
            kc, vc = _project_kv(flat(ctx), mod_all, g1, w_in_b, kg, ones, layer=i, w_layer=i,
                                 mod_row=b, tm=tm_ctx, name="proj_ctx_kv")
            kc, vc = (a.reshape(b, lc, ATTN_WIDTH) for a in (kc, vc))
        else:
            qc, kc, vc, yc = proj(flat(ctx), rope_tabs=None, mod_row=b, tm=tm_ctx, name="proj_ctx")
            qc, kc, vc = (a.reshape(b, lc, ATTN_WIDTH) for a in (qc, kc, vc))
            yc = yc.reshape(b, lc // 2, 4 * FOURIER_WIDTH)
        qx, kx, vx, yx = proj(x, rope_tabs=rope_tabs, mod_row=None, tm=1024, name="proj_lat")

        casts = [(w, i) for w in (w_out, w_gate, w_up, w_down)]
        attn_x, rounded = attend(qx, [kc, kx], [vc, vx], casts=casts, tq=l, heads=1, name="attn_lat")
        ffn = functools.partial(_out_ffn, mod_all=mod_all, g2=g2, w_out=rounded[0], w_gate=rounded[1],
                                w_up=rounded[2], w_down=rounded[3], layer=i, w_layer=0)
        four_x = _fourier(yx, pos_lat, name="fourier_lat")
        x_new = ffn(x, attn_x, four_x, mod_row=None, tm=1024, name="out_ffn_lat")
        if not last:
            attn_c, _ = attend(qc, [kc], [vc], tq=lc, heads=N_HEADS, name="attn_ctx")
            four_c = _fourier(yc, pos_ctx, name="fourier_ctx")
            ctx = ffn(flat(ctx), flat(attn_c), flat(four_c), mod_row=b, tm=tm_ctx,
                      name="out_ffn_ctx").reshape(b, lc, d)
        x = x_new
    return x
```

```python
import functools
import math

import numpy as np
import jax
import jax.numpy as jnp
from jax import lax
from jax.experimental import pallas as pl
from jax.experimental.pallas import tpu as pltpu

D_MODEL = 1024
GRID_W = 64
ATTN_WIDTH = 512
FOURIER_WIDTH = 512
HEAD_DIM = 64
N_HEADS = 4
V_HEAD_DIM = 128
N_FGROUPS = 4
FGROUP_DIM = 128
IN_COLS = 2048
ROPE_AXIS_DIM = 32
ROPE_THETA = 10000.0
EPS = 1e-6

LANES = 128
MXU_DIM = 256
BF16_SUBLANES = 16
MOD_ROWS = 16
VMEM_LIMIT_BYTES = 56 * 1024 * 1024
SAFE_SCORE_BOUND = 40.0
TOKEN_TILE = 1024
LOG2_E = 1.4426950408889634

F32 = jnp.float32
BF16 = jnp.bfloat16
_NT = (((1,), (1,)), ((), ()))


def _params(*sem):
    return pltpu.CompilerParams(dimension_semantics=sem, vmem_limit_bytes=VMEM_LIMIT_BYTES)


def _resident(shape, layer=None):
    zeros = (0,) * len(shape)
    if layer is None:
        return pl.BlockSpec(shape, lambda *_: zeros, pipeline_mode=pl.Buffered(1))
    return pl.BlockSpec((None,) + tuple(shape), lambda *_: (layer,) + zeros, pipeline_mode=pl.Buffered(1))


def _mod_spec(layer, row_of):
    return pl.BlockSpec((None, None, 6, D_MODEL), lambda *g: (layer, row_of(*g), 0, 0))


def _rope_tables(n_tokens):
    rows = n_tokens // GRID_W
    row = np.repeat(np.arange(rows, dtype=np.float64), GRID_W)
    col = np.tile(np.arange(GRID_W, dtype=np.float64), rows)
    inv = np.float32(ROPE_THETA) ** (-np.arange(0, ROPE_AXIS_DIM, 2, dtype=np.float32) / np.float32(ROPE_AXIS_DIM))
    inv = inv.astype(np.float64)
    half = ROPE_AXIS_DIM // 2
    d = np.arange(HEAD_DIM)
    pos = np.where(d[None, :] < ROPE_AXIS_DIM, row[:, None], col[:, None])
    ang = pos * inv[d % half][None, :]
    low = (d % ROPE_AXIS_DIM) < half
    cos = np.cos(ang)
    sin = np.where(low[None, :], -np.sin(ang), np.sin(ang))
    tile2 = lambda a: jnp.asarray(np.tile(a, (1, LANES // HEAD_DIM)), F32)
    return tile2(cos), tile2(sin)


def _channel_dft():
    n = FGROUP_DIM
    k = (np.arange(n)[:, None] * np.arange(n)[None, :]) % n
    ang = 2.0 * np.pi * k / n
    m = np.concatenate([np.cos(ang), np.sin(ang)], axis=1) / np.sqrt(n)
    return jnp.asarray(m, F32).astype(BF16)


def _position_dft(n_tokens):
    half = n_tokens // 2
    lo = np.arange(half)[:, None]
    m = np.arange(half)[None, :]
    mats = []
    for s in (0, 1):
        k = ((2 * m + s) * lo) % n_tokens
        ang = 2.0 * np.pi * k / n_tokens
        mats.append(jnp.asarray(np.cos(ang) / np.sqrt(n_tokens), F32).astype(BF16))
        mats.append(jnp.asarray(-np.sin(ang) / np.sqrt(n_tokens), F32).astype(BF16))
    return mats


def _segment_mean():
    seg = np.arange(MXU_DIM) // HEAD_DIM
    return jnp.asarray((seg[:, None] == seg[None, :]) / HEAD_DIM, BF16)


def _mod_kernel(c_ref, w_ref, b_ref, o_ref):
    cv = c_ref[...]
    s = cv / (1.0 + jnp.exp(-cv))
    o_ref[0] = jnp.dot(s.astype(BF16), w_ref[0].astype(BF16), preferred_element_type=F32) + b_ref[0]


def _adaln_mod(cvec, w_ada, b_ada):
    depth, d, n = w_ada.shape
    tn = 1536
    return pl.pallas_call(
        _mod_kernel,
        grid=(depth, n // tn),
        in_specs=[pl.BlockSpec((MOD_ROWS, d), lambda i, j: (0, 0)),
                  pl.BlockSpec((1, d, tn), lambda i, j: (i, 0, j)),
                  pl.BlockSpec((1, 1, tn), lambda i, j: (i, 0, j))],
        out_specs=pl.BlockSpec((1, MOD_ROWS, tn), lambda i, j: (i, 0, j)),
        out_shape=jax.ShapeDtypeStruct((depth, MOD_ROWS, n), F32),
        compiler_params=_params("arbitrary", "arbitrary"),
        name="adaln_mod",
    )(cvec, w_ada, b_ada.reshape(depth, 1, n))


def _proj_kernel(*refs, rope):
    if rope:
        (x_ref, mod_ref, g1_ref, w_ref, qg_ref, kg_ref, seg_ref, cdft_ref,
         cos_ref, sin_ref, q_ref, k_ref, v_ref, y_ref, f_scr) = refs
    else:
        (x_ref, mod_ref, g1_ref, w_ref, qg_ref, kg_ref, seg_ref, cdft_ref,
         q_ref, k_ref, v_ref, y_ref, f_scr) = refs
    tm = x_ref.shape[1]
    sub = f_scr.shape[1]
    swap_idx = lax.broadcasted_iota(jnp.int32, (sub, LANES), 1) ^ (ROPE_AXIS_DIM // 2)

    qk_cols = 2 * ATTN_WIDTH

    gain_scale = g1_ref[...] * (1.0 + mod_ref[1:2, :])

    def norm_mod(r0):
        x = x_ref[0, r0:r0 + sub, :]
        ms = jnp.mean(x * x, axis=-1, keepdims=True)
        return (x * lax.rsqrt(ms + EPS) * gain_scale + mod_ref[0:1, :]).astype(BF16)

    def project_qk(h):
        return jnp.dot(h, w_ref[:, 0:qk_cols], preferred_element_type=F32)

    def project_vf(h):
        return jnp.dot(h, w_ref[:, qk_cols:], preferred_element_type=F32)

    def norm_rope(t, gain, out_ref, r0):
        for cb in range(ATTN_WIDTH // MXU_DIM):
            tc = t[:, MXU_DIM * cb:MXU_DIM * (cb + 1)]
            ms = jnp.dot((tc * tc).astype(BF16), seg_ref[...], preferred_element_type=F32)
            tn = tc * lax.rsqrt(ms + EPS) * gain
            for hb in range(MXU_DIM // LANES):
                u = tn[:, LANES * hb:LANES * (hb + 1)]
                if rope:
                    partner = jnp.take_along_axis(u, swap_idx, axis=1)
                    u = u * cos_ref[r0:r0 + sub, :] + partner * sin_ref[r0:r0 + sub, :]
                c0 = MXU_DIM * cb + LANES * hb
                out_ref[0, r0:r0 + sub, c0:c0 + LANES] = u.astype(BF16)

    def emit_qk(r0, z):
        norm_rope(z[:, 0:ATTN_WIDTH], qg_ref[...], q_ref, r0)
        norm_rope(z[:, ATTN_WIDTH:], kg_ref[...], k_ref, r0)

    def emit_vf(r0, z):
        v_ref[0, r0:r0 + sub, :] = z[:, 0:ATTN_WIDTH].astype(BF16)
        w = FOURIER_WIDTH
        h0 = r0 // 2
        for g in range(N_FGROUPS):
            c0 = FGROUP_DIM * g
            f_scr[g] = z[:, ATTN_WIDTH + c0:ATTN_WIDTH + c0 + FGROUP_DIM]
            for parity in range(2):
                fp = f_scr[g, pl.ds(parity, sub // 2, stride=2), :].astype(BF16)
                yg = jnp.dot(fp, cdft_ref[g], preferred_element_type=F32)
                base = 2 * w * parity
                y_ref[0, h0:h0 + sub // 2, base + c0:base + c0 + FGROUP_DIM] = (
                    yg[:, :FGROUP_DIM].astype(BF16))
                y_ref[0, h0:h0 + sub // 2, base + w + c0:base + w + c0 + FGROUP_DIM] = (
                    yg[:, FGROUP_DIM:].astype(BF16))

    starts = list(range(0, tm, sub))
    h = norm_mod(starts[0])
    z_qk = project_qk(h)
    pending_vf = None
    for t, r0 in enumerate(starts):
        if pending_vf is not None:
            emit_vf(*pending_vf)
        pending_vf = (r0, project_vf(h))
        emit_qk(r0, z_qk)
        if t + 1 < len(starts):
            h = norm_mod(starts[t + 1])
            z_qk = project_qk(h)
    emit_vf(*pending_vf)


def _project(x, mod_all, g1, w_in, qg, kg, seg, cdft, rope_tabs, *, layer, w_layer, mod_row, tm, name):
    b, l, d = x.shape
    rope = rope_tabs is not None
    row_of = (lambda i, t: i) if mod_row is None else (lambda i, t: mod_row)
    tok = lambda width: pl.BlockSpec((1, tm, width), lambda i, t: (i, t, 0))
    in_specs = [tok(d),
                _mod_spec(layer, row_of),
                _resident((1, d), layer),
                _resident((d, IN_COLS), w_layer),
                _resident((1, MXU_DIM), layer),
                _resident((1, MXU_DIM), layer),
                _resident((MXU_DIM, MXU_DIM)),
                _resident((N_FGROUPS, FGROUP_DIM, 2 * FGROUP_DIM), layer)]
    args = [x, mod_all, g1, w_in, qg, kg, seg, cdft]
    if rope:
        in_specs += [pl.BlockSpec((tm, LANES), lambda i, t: (t, 0))] * 2
        args += list(rope_tabs)
    out = lambda width: jax.ShapeDtypeStruct((b, l, width), BF16)
    return pl.pallas_call(
        functools.partial(_proj_kernel, rope=rope),
        grid=(b, l // tm),
        in_specs=in_specs,
        out_specs=[tok(ATTN_WIDTH), tok(ATTN_WIDTH), tok(ATTN_WIDTH),
                   pl.BlockSpec((1, tm // 2, 4 * FOURIER_WIDTH), lambda i, t: (i, t, 0))],
        out_shape=[out(ATTN_WIDTH), out(ATTN_WIDTH), out(ATTN_WIDTH),
                   jax.ShapeDtypeStruct((b, l // 2, 4 * FOURIER_WIDTH), BF16)],
        scratch_shapes=[pltpu.VMEM((N_FGROUPS, min(tm, MXU_DIM), FGROUP_DIM), F32)],
        compiler_params=_params("arbitrary", "arbitrary"),
        name=name,
    )(*args)


def _proj_kv_kernel(x_ref, mod_ref, g1_ref, wk_ref, wv_ref, kg_ref, seg_ref, k_ref, v_ref, *, sub):
    gain_scale = g1_ref[...] * (1.0 + mod_ref[1:2, :])
    for r0 in range(0, x_ref.shape[1], sub):
        x = x_ref[0, r0:r0 + sub, :]
        ms = jnp.mean(x * x, axis=-1, keepdims=True)
        h = (x * lax.rsqrt(ms + EPS) * gain_scale + mod_ref[0:1, :]).astype(BF16)
        zk = jnp.dot(h, wk_ref[...], preferred_element_type=F32)
        v_ref[0, r0:r0 + sub, :] = jnp.dot(h, wv_ref[...], preferred_element_type=F32).astype(BF16)
        for cb in range(ATTN_WIDTH // MXU_DIM):
            cols = pl.ds(MXU_DIM * cb, MXU_DIM)
            tc = zk[:, MXU_DIM * cb:MXU_DIM * (cb + 1)]
            seg_ms = jnp.dot((tc * tc).astype(BF16), seg_ref[...], preferred_element_type=F32)
            k_ref[0, r0:r0 + sub, cols] = (tc * lax.rsqrt(seg_ms + EPS) * kg_ref[...]).astype(BF16)


def _project_kv(x, mod_all, g1, w_in, kg, seg, *, layer, w_layer, mod_row, tm, name):
    b, l, d = x.shape
    tok = lambda width: pl.BlockSpec((1, tm, width), lambda i, t: (i, t, 0))
    w_cols = lambda blk: pl.BlockSpec((None, d, ATTN_WIDTH), lambda i, t: (w_layer, 0, blk),
                                      pipeline_mode=pl.Buffered(1))
    out = jax.ShapeDtypeStruct((b, l, ATTN_WIDTH), BF16)
    return pl.pallas_call(
        functools.partial(_proj_kv_kernel, sub=min(tm, MXU_DIM)),
        grid=(b, l // tm),
        in_specs=[tok(d), _mod_spec(layer, lambda i, t: mod_row), _resident((1, d), layer),
                  w_cols(1), w_cols(2),
                  _resident((1, MXU_DIM), layer), _resident((MXU_DIM, MXU_DIM))],
        out_specs=[tok(ATTN_WIDTH), tok(ATTN_WIDTH)],
        out_shape=[out, out],
        compiler_params=_params("arbitrary", "arbitrary"),
        name=name,
    )(x, mod_all, g1, w_in, w_in, kg, seg)


def _attn_kernel(*refs, n_seg, n_cast, lambda_init, sub_rows):
    bound_ref, q_ref = refs[0], refs[1]
    k_refs = refs[2:2 + n_seg]
    v_refs = refs[2 + n_seg:2 + 2 * n_seg]
    lam_ref, sg_ref = refs[2 + 2 * n_seg:4 + 2 * n_seg]
    cast_in = refs[4 + 2 * n_seg:4 + 2 * n_seg + n_cast]
    o_ref = refs[4 + 2 * n_seg + n_cast]
    cast_out = refs[5 + 2 * n_seg + n_cast:]


    def attend(bounded):
        wide_cache.clear()
        for src, dst in zip(cast_in, cast_out):
            dst[...] = src[...].astype(BF16)
        lv = lam_ref[...]
        lam = (jnp.exp(jnp.sum(lv[0:1] * lv[1:2], axis=-1, keepdims=True))
               - jnp.exp(jnp.sum(lv[2:3] * lv[3:4], axis=-1, keepdims=True)) + lambda_init)
        lane = lax.broadcasted_iota(jnp.int32, (1, LANES), 1)
        first = (lane < HEAD_DIM).astype(BF16)
        items = [(pl.ds(LANES * h, LANES), pl.ds(r0, sub_rows))
                 for h in range(q_ref.shape[2] // LANES) for r0 in range(0, q_ref.shape[1], sub_rows)]
        probs = scores(items[0], first, bounded)
        for t, item in enumerate(items):
            nxt = scores(items[t + 1], first, bounded) if t + 1 < len(items) else None
            finish(item, probs, lam)
            probs = nxt

    def scores(item, first, bounded):
        cols, rows = item
        q = q_ref[0, rows, cols]

        def one_map(qm):
            ss = [lax.dot_general(qm, k[0, :, cols], _NT, preferred_element_type=F32) for k in k_refs]
            if bounded:
                ps = [jnp.exp2(s) for s in ss]
            else:
                m = functools.reduce(jnp.maximum, [jnp.max(s, axis=-1, keepdims=True) for s in ss])
                ps = [jnp.exp2(s - m) for s in ss]
            l = functools.reduce(lambda a, c: a + c, [jnp.sum(p, axis=-1, keepdims=True) for p in ps])
            return [p.astype(BF16) for p in ps], l

        return one_map(q * first), one_map(q * (1.0 - first).astype(BF16))

    wide_cache = {}

    def wide_v(cols):
        if cols.start not in wide_cache:
            wide_cache[cols.start] = [
                jnp.concatenate([v[0, :, cols]] * (MXU_DIM // LANES), axis=1) for v in v_refs]
        return wide_cache[cols.start]

    def finish(item, probs, lam):
        cols, rows = item
        (p1, l1), (p2, l2) = probs
        c = (lam * l1 / l2).astype(BF16)
        o = functools.reduce(lambda a, b: a + b, [
            jnp.dot(pa - c * pb, vw, preferred_element_type=F32)[:, :LANES]
            for pa, pb, vw in zip(p1, p2, wide_v(cols))])
        o = o * (1.0 / l1)
        ms = jnp.mean(o * o, axis=-1, keepdims=True)
        o = o * lax.rsqrt(ms + EPS) * sg_ref[...] * (1.0 - lambda_init)
        o_ref[0, rows, cols] = o.astype(BF16)

    safe = bound_ref[0, 0] <= SAFE_SCORE_BOUND
    pl.when(safe)(lambda: attend(True))
    pl.when(jnp.logical_not(safe))(lambda: attend(False))


def _attention(bound, q, ks, vs, lam_rows, subln_g, casts=(), *, layer, lambda_init, tq, heads, name):
    b, lq, _ = q.shape
    n_seg = len(ks)
    width = heads * LANES
    grid = (b, N_HEADS // heads, lq // tq)
    qo_spec = pl.BlockSpec((1, tq, width), lambda i, h, t: (i, t, h))
    kv_specs = [pl.BlockSpec((1, k.shape[1], width), lambda i, h, t: (i, 0, h)) for k in ks]

    n_steps = grid[0] * grid[1] * grid[2]
    step_of = lambda i, h, t: (i * grid[1] + h) * grid[2] + t
    cast_in_specs, cast_out_specs, cast_shapes = [], [], []
    for w, w_layer in casts:
        _, r, c = w.shape
        per = next(p for p in range(1, n_steps + 1)
                   if n_steps % p == 0 and r % (n_steps // p) == 0
                   and (r // (n_steps // p)) % BF16_SUBLANES == 0)
        rows = r // (n_steps // per)
        cast_in_specs.append(pl.BlockSpec(
            (None, rows, c), lambda i, h, t, w_layer=w_layer, per=per: (w_layer, step_of(i, h, t) // per, 0)))
        cast_out_specs.append(pl.BlockSpec(
            (None, rows, c), lambda i, h, t, per=per: (0, step_of(i, h, t) // per, 0)))
        cast_shapes.append(jax.ShapeDtypeStruct((1, r, c), BF16))

    out = pl.pallas_call(
        functools.partial(_attn_kernel, n_seg=n_seg, n_cast=len(casts), lambda_init=lambda_init,
                          sub_rows=min(tq, MXU_DIM)),
        grid=grid,
        in_specs=[pl.BlockSpec(memory_space=pltpu.SMEM), qo_spec] + kv_specs + kv_specs
                 + [_resident((4, HEAD_DIM), layer), _resident((1, V_HEAD_DIM), layer)] + cast_in_specs,
        out_specs=[qo_spec] + cast_out_specs,
        out_shape=[jax.ShapeDtypeStruct((b, lq, ATTN_WIDTH), BF16)] + cast_shapes,
        compiler_params=_params("arbitrary", "arbitrary", "arbitrary"),
        name=name,
    )(bound, q, *ks, *vs, lam_rows, subln_g, *[w for w, _ in casts])
    return out[0], out[1:]


def _fold_kernel(cdft_ref, wf_ref, o_ref):
    for g in range(N_FGROUPS):
        wg = wf_ref[g].astype(BF16)
        for part in range(2):
            cols = pl.ds(FGROUP_DIM * part, FGROUP_DIM)
            o_ref[g, :, cols] = jnp.dot(cdft_ref[:, cols], wg, preferred_element_type=F32).astype(BF16)


def _fold_group_map(cdft, w_fourier):
    depth = w_fourier.shape[0]
    return pl.pallas_call(
        _fold_kernel,
        grid=(depth,),
        in_specs=[_resident((FGROUP_DIM, 2 * FGROUP_DIM)),
                  pl.BlockSpec((None, N_FGROUPS, FGROUP_DIM, FGROUP_DIM), lambda i: (i, 0, 0, 0))],
        out_specs=pl.BlockSpec((None, N_FGROUPS, FGROUP_DIM, 2 * FGROUP_DIM), lambda i: (i, 0, 0, 0)),
        out_shape=jax.ShapeDtypeStruct((depth, N_FGROUPS, FGROUP_DIM, 2 * FGROUP_DIM), BF16),
        compiler_params=_params("arbitrary"),
        name="fold_group_map",
    )(cdft, w_fourier)


def _fourier_kernel(y_ref, ce_ref, se_ref, co_ref, so_ref, o_ref):
    w = FOURIER_WIDTH
    e = (jnp.dot(ce_ref[...], y_ref[0, :, 0:w], preferred_element_type=F32)
         + jnp.dot(se_ref[...], y_ref[0, :, w:2 * w], preferred_element_type=F32))
    o = (jnp.dot(co_ref[...], y_ref[0, :, 2 * w:3 * w], preferred_element_type=F32)
         + jnp.dot(so_ref[...], y_ref[0, :, 3 * w:4 * w], preferred_element_type=F32))
    o_ref[0, 0] = (e + o).astype(BF16)
    o_ref[0, 1] = (e - o).astype(BF16)


def _fourier(y2, pos_mats, *, name):
    b, half, _ = y2.shape
    out = pl.pallas_call(
        _fourier_kernel,
        grid=(b,),
        in_specs=[pl.BlockSpec((1, half, 4 * FOURIER_WIDTH), lambda i: (i, 0, 0))]
                 + [_resident((half, half))] * 4,
        out_specs=pl.BlockSpec((1, 2, half, FOURIER_WIDTH), lambda i: (i, 0, 0, 0)),
        out_shape=jax.ShapeDtypeStruct((b, 2, half, FOURIER_WIDTH), BF16),
        compiler_params=_params("arbitrary"),
        name=name,
    )(y2, *pos_mats)
    return out.reshape(b, 2 * half, FOURIER_WIDTH)


def _ffn_kernel(x_ref, a_ref, f_ref, mod_ref, g2_ref, wo_ref, wg_ref, wu_ref, wd_ref, o_ref, *, sub):
    tm = x_ref.shape[1]
    gain_scale = g2_ref[...] * (1.0 + mod_ref[4:5, :])

    def mix_norm(r0):
        rows = pl.ds(r0, sub)
        mix = (jnp.dot(a_ref[0, rows, :], wo_ref[0:ATTN_WIDTH, :], preferred_element_type=F32)
               + jnp.dot(f_ref[0, rows, :], wo_ref[ATTN_WIDTH:, :], preferred_element_type=F32))
        x1 = x_ref[0, rows, :] + mod_ref[2:3, :] * mix
        ms = jnp.mean(x1 * x1, axis=-1, keepdims=True)
        h = (x1 * lax.rsqrt(ms + EPS) * gain_scale + mod_ref[3:4, :]).astype(BF16)
        return x1, h

    def swiglu(r0, x1, h):
        gate = jnp.dot(h, wg_ref[...], preferred_element_type=F32)
        up = jnp.dot(h, wu_ref[...], preferred_element_type=F32)
        act = (gate / (1.0 + jnp.exp(-gate)) * up).astype(BF16)
        y = jnp.dot(act, wd_ref[...], preferred_element_type=F32)
        o_ref[0, pl.ds(r0, sub), :] = x1 + mod_ref[5:6, :] * y

    starts = list(range(0, tm, sub))
    cur = mix_norm(starts[0])
    for t, r0 in enumerate(starts):
        nxt = mix_norm(starts[t + 1]) if t + 1 < len(starts) else None
        swiglu(r0, *cur)
        cur = nxt


def _out_ffn(x, attn, four, mod_all, g2, w_out, w_gate, w_up, w_down, *, layer, w_layer, mod_row, tm, name):
    b, l, d = x.shape
    d_ff = w_gate.shape[-1]
    row_of = (lambda i, t: i) if mod_row is None else (lambda i, t: mod_row)
    tok = lambda width: pl.BlockSpec((1, tm, width), lambda i, t: (i, t, 0))
    return pl.pallas_call(
        functools.partial(_ffn_kernel, sub=min(tm, MXU_DIM)),
        grid=(b, l // tm),
        in_specs=[tok(d), tok(ATTN_WIDTH), tok(FOURIER_WIDTH),
                  _mod_spec(layer, row_of),
                  _resident((1, d), layer),
                  _resident((d, d), w_layer),
                  _resident((d, d_ff), w_layer),
                  _resident((d, d_ff), w_layer),
                  _resident((d_ff, d), w_layer)],
        out_specs=tok(d),
        out_shape=jax.ShapeDtypeStruct((b, l, d), F32),
        compiler_params=_params("arbitrary", "arbitrary"),
        name=name,
    )(x, attn, four, mod_all, g2, w_out, w_gate, w_up, w_down)


def kernel(x, c, ctx, c_ctx, w_ada, b_ada, norm1_g, norm2_g, w_in, q_norm_g, k_norm_g, lambda_q1, lambda_k1, lambda_q2, lambda_k2, subln_g, w_fourier, w_out, w_gate, w_up, w_down):
    depth = w_ada.shape[0]
    b, l, d = x.shape
    lc = ctx.shape[1]
    assert b + 1 <= MOD_ROWS and lc % 2 == 0
    tm_ctx = math.gcd(b * lc, TOKEN_TILE // 2)

    rope_tabs = _rope_tables(l)
    cdft = _fold_group_map(_channel_dft(), w_fourier)
    pos_lat = _position_dft(l)
    pos_ctx = _position_dft(lc)
    seg = _segment_mean()

    cvec = jnp.zeros((MOD_ROWS, d), F32).at[:b].set(c).at[b].set(c_ctx)
    mod_all = _adaln_mod(cvec, w_ada, b_ada).reshape(depth, MOD_ROWS, 6, d)

    w_in_b = w_in.astype(BF16)
    g1, g2 = norm1_g.reshape(depth, 1, d), norm2_g.reshape(depth, 1, d)
    tile_gain = lambda g: jnp.tile(g, (1, MXU_DIM // HEAD_DIM)).reshape(depth, 1, MXU_DIM)
    qg, kg = tile_gain(q_norm_g * (HEAD_DIM ** -0.5 * LOG2_E)), tile_gain(k_norm_g)
    lam_rows = jnp.stack([lambda_q1, lambda_k1, lambda_q2, lambda_k2], axis=1)
    sg = subln_g.reshape(depth, 1, V_HEAD_DIM)
    score_bound = (HEAD_DIM ** 0.5 * LOG2_E) * jnp.max(jnp.abs(q_norm_g), axis=1) * jnp.max(jnp.abs(k_norm_g), axis=1)

    for i in range(depth):
        last = i == depth - 1
        lambda_init = 0.8 - 0.6 * math.exp(-0.3 * i)
        bound = score_bound[i].reshape(1, 1)
        proj = functools.partial(_project, mod_all=mod_all, g1=g1, w_in=w_in_b, qg=qg, kg=kg,
                                 seg=seg, cdft=cdft, layer=i, w_layer=i)
        attend = functools.partial(_attention, bound, lam_rows=lam_rows, subln_g=sg, layer=i,
                                   lambda_init=lambda_init)

        flat = lambda a: a.reshape(1, b * lc, a.shape[-1])
        if last:
            kc, vc = _project_kv(flat(ctx), mod_all, g1, w_in_b, kg, seg, layer=i, w_layer=i,
                                 mod_row=b, tm=tm_ctx, name="proj_ctx_kv")
            kc, vc = (a.reshape(b, lc, ATTN_WIDTH) for a in (kc, vc))
        else:
            qc, kc, vc, yc = proj(flat(ctx), rope_tabs=None, mod_row=b, tm=tm_ctx, name="proj_ctx")
            qc, kc, vc = (a.reshape(b, lc, ATTN_WIDTH) for a in (qc, kc, vc))
            yc = yc.reshape(b, lc // 2, 4 * FOURIER_WIDTH)
        qx, kx, vx, yx = proj(x, rope_tabs=rope_tabs, mod_row=None, tm=TOKEN_TILE, name="proj_lat")

        casts = [(w, i) for w in (w_out, w_gate, w_up, w_down)]
        attn_x, rounded = attend(qx, [kc, kx], [vc, vx], casts=casts, tq=l, heads=1, name="attn_lat")
        ffn = functools.partial(_out_ffn, mod_all=mod_all, g2=g2, w_out=rounded[0], w_gate=rounded[1],
                                w_up=rounded[2], w_down=rounded[3], layer=i, w_layer=0)
        four_x = _fourier(yx, pos_lat, name="fourier_lat")
        x_new = ffn(x, attn_x, four_x, mod_row=None, tm=TOKEN_TILE, name="out_ffn_lat")
        if not last:
            attn_c, _ = attend(qc, [kc], [vc], tq=lc, heads=N_HEADS, name="attn_ctx")
            four_c = _fourier(yc, pos_ctx, name="fourier_ctx")
            ctx = ffn(flat(ctx), flat(attn_c), flat(four_c), mod_row=b, tm=tm_ctx,
                      name="out_ffn_ctx").reshape(b, lc, d)
        x = x_new
    return x
```

```python
import functools
import math

import numpy as np
import jax
import jax.numpy as jnp
from jax import lax
from jax.experimental import pallas as pl
from jax.experimental.pallas import tpu as pltpu

D_MODEL = 1024
GRID_W = 64
ATTN_WIDTH = 512
FOURIER_WIDTH = 512
HEAD_DIM = 64
N_HEADS = 4
V_HEAD_DIM = 128
N_FGROUPS = 4
FGROUP_DIM = 128
IN_COLS = 2048
ROPE_AXIS_DIM = 32
ROPE_THETA = 10000.0
EPS = 1e-6

LANES = 128
MXU_DIM = 256
BF16_SUBLANES = 16
MOD_ROWS = 16
VMEM_LIMIT_BYTES = 56 * 1024 * 1024
SAFE_SCORE_BOUND = 40.0
TOKEN_TILE = 1024
LOG2_E = 1.4426950408889634

F32 = jnp.float32
BF16 = jnp.bfloat16
_NT = (((1,), (1,)), ((), ()))


def _params(*sem):
    return pltpu.CompilerParams(dimension_semantics=sem, vmem_limit_bytes=VMEM_LIMIT_BYTES)


def _resident(shape, layer=None):
    zeros = (0,) * len(shape)
    if layer is None:
        return pl.BlockSpec(shape, lambda *_: zeros, pipeline_mode=pl.Buffered(1))
    return pl.BlockSpec((None,) + tuple(shape), lambda *_: (layer,) + zeros, pipeline_mode=pl.Buffered(1))


def _mod_spec(layer, row_of):
    return pl.BlockSpec((None, None, 6, D_MODEL), lambda *g: (layer, row_of(*g), 0, 0))


def _rope_tables(n_tokens):
    rows = n_tokens // GRID_W
    row = np.repeat(np.arange(rows, dtype=np.float64), GRID_W)
    col = np.tile(np.arange(GRID_W, dtype=np.float64), rows)
    inv = np.float32(ROPE_THETA) ** (-np.arange(0, ROPE_AXIS_DIM, 2, dtype=np.float32) / np.float32(ROPE_AXIS_DIM))
    inv = inv.astype(np.float64)
    half = ROPE_AXIS_DIM // 2
    d = np.arange(HEAD_DIM)
    pos = np.where(d[None, :] < ROPE_AXIS_DIM, row[:, None], col[:, None])
    ang = pos * inv[d % half][None, :]
    low = (d % ROPE_AXIS_DIM) < half
    cos = np.cos(ang)
    sin = np.where(low[None, :], -np.sin(ang), np.sin(ang))
    tile2 = lambda a: jnp.asarray(np.tile(a, (1, LANES // HEAD_DIM)), F32)
    return tile2(cos), tile2(sin)


def _channel_dft():
    n = FGROUP_DIM
    k = (np.arange(n)[:, None] * np.arange(n)[None, :]) % n
    ang = 2.0 * np.pi * k / n
    m = np.concatenate([np.cos(ang), np.sin(ang)], axis=1) / np.sqrt(n)
    return jnp.asarray(m, F32).astype(BF16)


def _position_dft(n_tokens):
    half = n_tokens // 2
    lo = np.arange(half)[:, None]
    m = np.arange(half)[None, :]
    mats = []
    for s in (0, 1):
        k = ((2 * m + s) * lo) % n_tokens
        ang = 2.0 * np.pi * k / n_tokens
        mats.append(jnp.asarray(np.cos(ang) / np.sqrt(n_tokens), F32).astype(BF16))
        mats.append(jnp.asarray(-np.sin(ang) / np.sqrt(n_tokens), F32).astype(BF16))
    return mats


def _segment_mean():
    seg = np.arange(MXU_DIM) // HEAD_DIM
    return jnp.asarray((seg[:, None] == seg[None, :]) / HEAD_DIM, BF16)


def _mod_kernel(c_ref, w_ref, b_ref, o_ref):
    cv = c_ref[...]
    s = cv / (1.0 + jnp.exp(-cv))
    o_ref[0] = jnp.dot(s.astype(BF16), w_ref[0].astype(BF16), preferred_element_type=F32) + b_ref[0]


def _adaln_mod(cvec, w_ada, b_ada):
    depth, d, n = w_ada.shape
    tn = 1536
    return pl.pallas_call(
        _mod_kernel,
        grid=(depth, n // tn),
        in_specs=[pl.BlockSpec((MOD_ROWS, d), lambda i, j: (0, 0)),
                  pl.BlockSpec((1, d, tn), lambda i, j: (i, 0, j)),
                  pl.BlockSpec((1, 1, tn), lambda i, j: (i, 0, j))],
        out_specs=pl.BlockSpec((1, MOD_ROWS, tn), lambda i, j: (i, 0, j)),
        out_shape=jax.ShapeDtypeStruct((depth, MOD_ROWS, n), F32),
        compiler_params=_params("arbitrary", "arbitrary"),
        name="adaln_mod",
    )(cvec, w_ada, b_ada.reshape(depth, 1, n))


def _proj_kernel(*refs, rope):
    if rope:
        (x_ref, mod_ref, g1_ref, w_ref, qg_ref, kg_ref, seg_ref, cdft_ref,
         cos_ref, sin_ref, q_ref, k_ref, v_ref, y_ref, f_scr) = refs
    else:
        (x_ref, mod_ref, g1_ref, w_ref, qg_ref, kg_ref, seg_ref, cdft_ref,
         q_ref, k_ref, v_ref, y_ref, f_scr) = refs
    tm = x_ref.shape[1]
    sub = f_scr.shape[1]
    swap_idx = lax.broadcasted_iota(jnp.int32, (sub, LANES), 1) ^ (ROPE_AXIS_DIM // 2)

    qk_cols = 2 * ATTN_WIDTH

    gain_scale = g1_ref[...] * (1.0 + mod_ref[1:2, :])

    def norm_mod(r0):
        x = x_ref[0, r0:r0 + sub, :]
        ms = jnp.mean(x * x, axis=-1, keepdims=True)
        return (x * lax.rsqrt(ms + EPS) * gain_scale + mod_ref[0:1, :]).astype(BF16)

    def project_qk(h):
        return jnp.dot(h, w_ref[:, 0:qk_cols], preferred_element_type=F32)

    def project_vf(h):
        return jnp.dot(h, w_ref[:, qk_cols:], preferred_element_type=F32)

    def norm_rope(t, gain, out_ref, r0):
        for cb in range(ATTN_WIDTH // MXU_DIM):
            tc = t[:, MXU_DIM * cb:MXU_DIM * (cb + 1)]
            ms = jnp.dot((tc * tc).astype(BF16), seg_ref[...], preferred_element_type=F32)
            tn = tc * lax.rsqrt(ms + EPS) * gain
            for hb in range(MXU_DIM // LANES):
                u = tn[:, LANES * hb:LANES * (hb + 1)]
                if rope:
                    partner = jnp.take_along_axis(u, swap_idx, axis=1)
                    u = u * cos_ref[r0:r0 + sub, :] + partner * sin_ref[r0:r0 + sub, :]
                c0 = MXU_DIM * cb + LANES * hb
                out_ref[0, r0:r0 + sub, c0:c0 + LANES] = u.astype(BF16)

    def emit_qk(r0, z):
        norm_rope(z[:, 0:ATTN_WIDTH], qg_ref[...], q_ref, r0)
        norm_rope(z[:, ATTN_WIDTH:], kg_ref[...], k_ref, r0)

    def emit_vf(r0, z):
        v_ref[0, r0:r0 + sub, :] = z[:, 0:ATTN_WIDTH].astype(BF16)
        w = FOURIER_WIDTH
        h0 = r0 // 2
        for g in range(N_FGROUPS):
            c0 = FGROUP_DIM * g
            f_scr[g] = z[:, ATTN_WIDTH + c0:ATTN_WIDTH + c0 + FGROUP_DIM]
            for parity in range(2):
                fp = f_scr[g, pl.ds(parity, sub // 2, stride=2), :].astype(BF16)
                yg = jnp.dot(fp, cdft_ref[g], preferred_element_type=F32)
                base = 2 * w * parity
                y_ref[0, h0:h0 + sub // 2, base + c0:base + c0 + FGROUP_DIM] = (
                    yg[:, :FGROUP_DIM].astype(BF16))
                y_ref[0, h0:h0 + sub // 2, base + w + c0:base + w + c0 + FGROUP_DIM] = (
                    yg[:, FGROUP_DIM:].astype(BF16))

    starts = list(range(0, tm, sub))
    h = norm_mod(starts[0])
    z_qk = project_qk(h)
    pending_vf = None
    for t, r0 in enumerate(starts):
        if pending_vf is not None:
            emit_vf(*pending_vf)
        pending_vf = (r0, project_vf(h))
        emit_qk(r0, z_qk)
        if t + 1 < len(starts):
            h = norm_mod(starts[t + 1])
            z_qk = project_qk(h)
    emit_vf(*pending_vf)


def _project(x, mod_all, g1, w_in, qg, kg, seg, cdft, rope_tabs, *, layer, w_layer, mod_row, tm, name):
    b, l, d = x.shape
    rope = rope_tabs is not None
    row_of = (lambda i, t: i) if mod_row is None else (lambda i, t: mod_row)
    tok = lambda width: pl.BlockSpec((1, tm, width), lambda i, t: (i, t, 0))
    in_specs = [tok(d),
                _mod_spec(layer, row_of),
                _resident((1, d), layer),
                _resident((d, IN_COLS), w_layer),
                _resident((1, MXU_DIM), layer),
                _resident((1, MXU_DIM), layer),
                _resident((MXU_DIM, MXU_DIM)),
                _resident((N_FGROUPS, FGROUP_DIM, 2 * FGROUP_DIM), layer)]
    args = [x, mod_all, g1, w_in, qg, kg, seg, cdft]
    if rope:
        in_specs += [pl.BlockSpec((tm, LANES), lambda i, t: (t, 0))] * 2
        args += list(rope_tabs)
    out = lambda width: jax.ShapeDtypeStruct((b, l, width), BF16)
    return pl.pallas_call(
        functools.partial(_proj_kernel, rope=rope),
        grid=(b, l // tm),
        in_specs=in_specs,
        out_specs=[tok(ATTN_WIDTH), tok(ATTN_WIDTH), tok(ATTN_WIDTH),
                   pl.BlockSpec((1, tm // 2, 4 * FOURIER_WIDTH), lambda i, t: (i, t, 0))],
        out_shape=[out(ATTN_WIDTH), out(ATTN_WIDTH), out(ATTN_WIDTH),
                   jax.ShapeDtypeStruct((b, l // 2, 4 * FOURIER_WIDTH), BF16)],
        scratch_shapes=[pltpu.VMEM((N_FGROUPS, min(tm, MXU_DIM), FGROUP_DIM), F32)],
        compiler_params=_params("arbitrary", "arbitrary"),
        name=name,
    )(*args)


def _proj_kv_kernel(x_ref, mod_ref, g1_ref, wk_ref, wv_ref, kg_ref, seg_ref, k_ref, v_ref, *, sub):
    gain_scale = g1_ref[...] * (1.0 + mod_ref[1:2, :])
    for r0 in range(0, x_ref.shape[1], sub):
        x = x_ref[0, r0:r0 + sub, :]
        ms = jnp.mean(x * x, axis=-1, keepdims=True)
        h = (x * lax.rsqrt(ms + EPS) * gain_scale + mod_ref[0:1, :]).astype(BF16)
        zk = jnp.dot(h, wk_ref[...], preferred_element_type=F32)
        v_ref[0, r0:r0 + sub, :] = jnp.dot(h, wv_ref[...], preferred_element_type=F32).astype(BF16)
        for cb in range(ATTN_WIDTH // MXU_DIM):
            cols = pl.ds(MXU_DIM * cb, MXU_DIM)
            tc = zk[:, MXU_DIM * cb:MXU_DIM * (cb + 1)]
            seg_ms = jnp.dot((tc * tc).astype(BF16), seg_ref[...], preferred_element_type=F32)
            k_ref[0, r0:r0 + sub, cols] = (tc * lax.rsqrt(seg_ms + EPS) * kg_ref[...]).astype(BF16)


def _project_kv(x, mod_all, g1, w_in, kg, seg, *, layer, w_layer, mod_row, tm, name):
    b, l, d = x.shape
    tok = lambda width: pl.BlockSpec((1, tm, width), lambda i, t: (i, t, 0))
    w_cols = lambda blk: pl.BlockSpec((None, d, ATTN_WIDTH), lambda i, t: (w_layer, 0, blk),
                                      pipeline_mode=pl.Buffered(1))
    out = jax.ShapeDtypeStruct((b, l, ATTN_WIDTH), BF16)
    return pl.pallas_call(
        functools.partial(_proj_kv_kernel, sub=min(tm, MXU_DIM)),
        grid=(b, l // tm),
        in_specs=[tok(d), _mod_spec(layer, lambda i, t: mod_row), _resident((1, d), layer),
                  w_cols(1), w_cols(2),
                  _resident((1, MXU_DIM), layer), _resident((MXU_DIM, MXU_DIM))],
        out_specs=[tok(ATTN_WIDTH), tok(ATTN_WIDTH)],
        out_shape=[out, out],
        compiler_params=_params("arbitrary", "arbitrary"),
        name=name,
    )(x, mod_all, g1, w_in, w_in, kg, seg)


def _attn_kernel(*refs, n_seg, n_cast, lambda_init, sub_rows):
    bound_ref, q_ref = refs[0], refs[1]
    k_refs = refs[2:2 + n_seg]
    v_refs = refs[2 + n_seg:2 + 2 * n_seg]
    lam_ref, sg_ref = refs[2 + 2 * n_seg:4 + 2 * n_seg]
    cast_in = refs[4 + 2 * n_seg:4 + 2 * n_seg + n_cast]
    o_ref = refs[4 + 2 * n_seg + n_cast]
    cast_out = refs[5 + 2 * n_seg + n_cast:]


    def attend(bounded):
        wide_cache.clear()
        for src, dst in zip(cast_in, cast_out):
            dst[...] = src[...].astype(BF16)
        lv = lam_ref[...]
        lam = (jnp.exp(jnp.sum(lv[0:1] * lv[1:2], axis=-1, keepdims=True))
               - jnp.exp(jnp.sum(lv[2:3] * lv[3:4], axis=-1, keepdims=True)) + lambda_init)
        lane = lax.broadcasted_iota(jnp.int32, (1, LANES), 1)
        first = (lane < HEAD_DIM).astype(BF16)
        items = [(pl.ds(LANES * h, LANES), pl.ds(r0, sub_rows))
                 for h in range(q_ref.shape[2] // LANES) for r0 in range(0, q_ref.shape[1], sub_rows)]
        probs = scores(items[0], first, bounded)
        for t, item in enumerate(items):
            nxt = scores(items[t + 1], first, bounded) if t + 1 < len(items) else None
            finish(item, probs, lam)
            probs = nxt

    def scores(item, first, bounded):
        cols, rows = item
        q = q_ref[0, rows, cols]

        def one_map(qm):
            ss = [lax.dot_general(qm, k[0, :, cols], _NT, preferred_element_type=F32) for k in k_refs]
            if bounded:
                ps = [jnp.exp2(s) for s in ss]
            else:
                m = functools.reduce(jnp.maximum, [jnp.max(s, axis=-1, keepdims=True) for s in ss])
                ps = [jnp.exp2(s - m) for s in ss]
            l = functools.reduce(lambda a, c: a + c, [jnp.sum(p, axis=-1, keepdims=True) for p in ps])
            return [p.astype(BF16) for p in ps], l

        return one_map(q * first), one_map(q * (1.0 - first).astype(BF16))

    wide_cache = {}

    def wide_v(cols):
        if cols.start not in wide_cache:
            wide_cache[cols.start] = [
                jnp.concatenate([v[0, :, cols]] * (MXU_DIM // LANES), axis=1) for v in v_refs]
        return wide_cache[cols.start]

    def finish(item, probs, lam):
        cols, rows = item
        (p1, l1), (p2, l2) = probs
        c = (lam * l1 / l2).astype(BF16)
        o = functools.reduce(lambda a, b: a + b, [
            jnp.dot(pa - c * pb, vw, preferred_element_type=F32)[:, :LANES]
            for pa, pb, vw in zip(p1, p2, wide_v(cols))])
        o = o * (1.0 / l1)
        ms = jnp.mean(o * o, axis=-1, keepdims=True)
        o = o * lax.rsqrt(ms + EPS) * sg_ref[...] * (1.0 - lambda_init)
        o_ref[0, rows, cols] = o.astype(BF16)

    safe = bound_ref[0, 0] <= SAFE_SCORE_BOUND
    pl.when(safe)(lambda: attend(True))
    pl.when(jnp.logical_not(safe))(lambda: attend(False))


def _attention(bound, q, ks, vs, lam_rows, subln_g, casts=(), *, layer, lambda_init, tq, heads, name):
    b, lq, _ = q.shape
    n_seg = len(ks)
    width = heads * LANES
    grid = (b, N_HEADS // heads, lq // tq)
    qo_spec = pl.BlockSpec((1, tq, width), lambda i, h, t: (i, t, h))
    kv_specs = [pl.BlockSpec((1, k.shape[1], width), lambda i, h, t: (i, 0, h)) for k in ks]

    n_steps = grid[0] * grid[1] * grid[2]
    step_of = lambda i, h, t: (i * grid[1] + h) * grid[2] + t
    cast_in_specs, cast_out_specs, cast_shapes = [], [], []
    for w, w_layer in casts:
        _, r, c = w.shape
        per = next(p for p in range(1, n_steps + 1)
                   if n_steps % p == 0 and r % (n_steps // p) == 0
                   and (r // (n_steps // p)) % BF16_SUBLANES == 0)
        rows = r // (n_steps // per)
        cast_in_specs.append(pl.BlockSpec(
            (None, rows, c), lambda i, h, t, w_layer=w_layer, per=per: (w_layer, step_of(i, h, t) // per, 0)))
        cast_out_specs.append(pl.BlockSpec(
            (None, rows, c), lambda i, h, t, per=per: (0, step_of(i, h, t) // per, 0)))
        cast_shapes.append(jax.ShapeDtypeStruct((1, r, c), BF16))

    out = pl.pallas_call(
        functools.partial(_attn_kernel, n_seg=n_seg, n_cast=len(casts), lambda_init=lambda_init,
                          sub_rows=min(tq, MXU_DIM)),
        grid=grid,
        in_specs=[pl.BlockSpec(memory_space=pltpu.SMEM), qo_spec] + kv_specs + kv_specs
                 + [_resident((4, HEAD_DIM), layer), _resident((1, V_HEAD_DIM), layer)] + cast_in_specs,
        out_specs=[qo_spec] + cast_out_specs,
        out_shape=[jax.ShapeDtypeStruct((b, lq, ATTN_WIDTH), BF16)] + cast_shapes,
        compiler_params=_params("arbitrary", "arbitrary", "arbitrary"),
        name=name,
    )(bound, q, *ks, *vs, lam_rows, subln_g, *[w for w, _ in casts])
    return out[0], out[1:]


def _fold_kernel(cdft_ref, wf_ref, o_ref):
    for g in range(N_FGROUPS):
        wg = wf_ref[g].astype(BF16)
        for part in range(2):
            cols = pl.ds(FGROUP_DIM * part, FGROUP_DIM)
            o_ref[g, :, cols] = jnp.dot(cdft_ref[:, cols], wg, preferred_element_type=F32).astype(BF16)


def _fold_group_map(cdft, w_fourier):
    depth = w_fourier.shape[0]
    return pl.pallas_call(
        _fold_kernel,
        grid=(depth,),
        in_specs=[_resident((FGROUP_DIM, 2 * FGROUP_DIM)),
                  pl.BlockSpec((None, N_FGROUPS, FGROUP_DIM, FGROUP_DIM), lambda i: (i, 0, 0, 0))],
        out_specs=pl.BlockSpec((None, N_FGROUPS, FGROUP_DIM, 2 * FGROUP_DIM), lambda i: (i, 0, 0, 0)),
        out_shape=jax.ShapeDtypeStruct((depth, N_FGROUPS, FGROUP_DIM, 2 * FGROUP_DIM), BF16),
        compiler_params=_params("arbitrary"),
        name="fold_group_map",
    )(cdft, w_fourier)


def _fourier_kernel(y_ref, ce_ref, se_ref, co_ref, so_ref, o_ref):
    w = FOURIER_WIDTH
    e = (jnp.dot(ce_ref[...], y_ref[0, :, 0:w], preferred_element_type=F32)
         + jnp.dot(se_ref[...], y_ref[0, :, w:2 * w], preferred_element_type=F32))
    o = (jnp.dot(co_ref[...], y_ref[0, :, 2 * w:3 * w], preferred_element_type=F32)
         + jnp.dot(so_ref[...], y_ref[0, :, 3 * w:4 * w], preferred_element_type=F32))
    o_ref[0, 0] = (e + o).astype(BF16)
    o_ref[0, 1] = (e - o).astype(BF16)


def _fourier(y2, pos_mats, *, name):
    b, half, _ = y2.shape
    out = pl.pallas_call(
        _fourier_kernel,
        grid=(b,),
        in_specs=[pl.BlockSpec((1, half, 4 * FOURIER_WIDTH), lambda i: (i, 0, 0))]
                 + [_resident((half, half))] * 4,
        out_specs=pl.BlockSpec((1, 2, half, FOURIER_WIDTH), lambda i: (i, 0, 0, 0)),
        out_shape=jax.ShapeDtypeStruct((b, 2, half, FOURIER_WIDTH), BF16),
        compiler_params=_params("arbitrary"),
        name=name,
    )(y2, *pos_mats)
    return out.reshape(b, 2 * half, FOURIER_WIDTH)


def _ffn_kernel(x_ref, a_ref, f_ref, mod_ref, g2_ref, wo_ref, wg_ref, wu_ref, wd_ref, o_ref, *, sub):
    tm = x_ref.shape[1]
    gain_scale = g2_ref[...] * (1.0 + mod_ref[4:5, :])

    def mix_norm(r0):
        rows = pl.ds(r0, sub)
        mix = (jnp.dot(a_ref[0, rows, :], wo_ref[0:ATTN_WIDTH, :], preferred_element_type=F32)
               + jnp.dot(f_ref[0, rows, :], wo_ref[ATTN_WIDTH:, :], preferred_element_type=F32))
        x1 = x_ref[0, rows, :] + mod_ref[2:3, :] * mix
        ms = jnp.mean(x1 * x1, axis=-1, keepdims=True)
        h = (x1 * lax.rsqrt(ms + EPS) * gain_scale + mod_ref[3:4, :]).astype(BF16)
        return x1, h

    def swiglu(r0, x1, h):
        gate = jnp.dot(h, wg_ref[...], preferred_element_type=F32)
        up = jnp.dot(h, wu_ref[...], preferred_element_type=F32)
        act = (gate / (1.0 + jnp.exp(-gate)) * up).astype(BF16)
        y = jnp.dot(act, wd_ref[...], preferred_element_type=F32)
        o_ref[0, pl.ds(r0, sub), :] = x1 + mod_ref[5:6, :] * y

    starts = list(range(0, tm, sub))
    cur = mix_norm(starts[0])
    for t, r0 in enumerate(starts):
        nxt = mix_norm(starts[t + 1]) if t + 1 < len(starts) else None
        swiglu(r0, *cur)
        cur = nxt


def _out_ffn(x, attn, four, mod_all, g2, w_out, w_gate, w_up, w_down, *, layer, w_layer, mod_row, tm, name):
    b, l, d = x.shape
    d_ff = w_gate.shape[-1]
    row_of = (lambda i, t: i) if mod_row is None else (lambda i, t: mod_row)
    tok = lambda width: pl.BlockSpec((1, tm, width), lambda i, t: (i, t, 0))
    return pl.pallas_call(
        functools.partial(_ffn_kernel, sub=min(tm, MXU_DIM)),
        grid=(b, l // tm),
        in_specs=[tok(d), tok(ATTN_WIDTH), tok(FOURIER_WIDTH),
                  _mod_spec(layer, row_of),
                  _resident((1, d), layer),
                  _resident((d, d), w_layer),
                  _resident((d, d_ff), w_layer),
                  _resident((d, d_ff), w_layer),
                  _resident((d_ff, d), w_layer)],
        out_specs=tok(d),
        out_shape=jax.ShapeDtypeStruct((b, l, d), F32),
        compiler_params=_params("arbitrary", "arbitrary"),
        name=name,
    )(x, attn, four, mod_all, g2, w_out, w_gate, w_up, w_down)


def kernel(x, c, ctx, c_ctx, w_ada, b_ada, norm1_g, norm2_g, w_in, q_norm_g, k_norm_g, lambda_q1, lambda_k1, lambda_q2, lambda_k2, subln_g, w_fourier, w_out, w_gate, w_up, w_down):
    depth = w_ada.shape[0]
    b, l, d = x.shape
    lc = ctx.shape[1]
    assert b + 1 <= MOD_ROWS and lc % 2 == 0
    tm_ctx = math.gcd(b * lc, TOKEN_TILE // 2)

    rope_tabs = _rope_tables(l)
    cdft = _fold_group_map(_channel_dft(), w_fourier)
    pos_lat = _position_dft(l)
    pos_ctx = _position_dft(lc)
    seg = _segment_mean()

    cvec = jnp.zeros((MOD_ROWS, d), F32).at[:b].set(c).at[b].set(c_ctx)
    mod_all = _adaln_mod(cvec, w_ada, b_ada).reshape(depth, MOD_ROWS, 6, d)

    w_in_b = w_in.astype(BF16)
    g1, g2 = norm1_g.reshape(depth, 1, d), norm2_g.reshape(depth, 1, d)
    tile_gain = lambda g: jnp.tile(g, (1, MXU_DIM // HEAD_DIM)).reshape(depth, 1, MXU_DIM)
    qg, kg = tile_gain(q_norm_g * (HEAD_DIM ** -0.5 * LOG2_E)), tile_gain(k_norm_g)
    lam_rows = jnp.stack([lambda_q1, lambda_k1, lambda_q2, lambda_k2], axis=1)
    sg = subln_g.reshape(depth, 1, V_HEAD_DIM)
    score_bound = (HEAD_DIM ** 0.5 * LOG2_E) * jnp.max(jnp.abs(q_norm_g), axis=1) * jnp.max(jnp.abs(k_norm_g), axis=1)

    for i in range(depth):
        last = i == depth - 1
        lambda_init = 0.8 - 0.6 * math.exp(-0.3 * i)
        bound = score_bound[i].reshape(1, 1)
        proj = functools.partial(_project, mod_all=mod_all, g1=g1, w_in=w_in_b, qg=qg, kg=kg,
                                 seg=seg, cdft=cdft, layer=i, w_layer=i)
        attend = functools.partial(_attention, bound, lam_rows=lam_rows, subln_g=sg, layer=i,
                                   lambda_init=lambda_init)

        flat = lambda a: a.reshape(1, b * lc, a.shape[-1])
        if last:
            kc, vc = _project_kv(flat(ctx), mod_all, g1, w_in_b, kg, seg, layer=i, w_layer=i,
                                 mod_row=b, tm=tm_ctx, name="proj_ctx_kv")
            kc, vc = (a.reshape(b, lc, ATTN_WIDTH) for a in (kc, vc))
        else:
            qc, kc, vc, yc = proj(flat(ctx), rope_tabs=None, mod_row=b, tm=tm_ctx, name="proj_ctx")
            qc, kc, vc = (a.reshape(b, lc, ATTN_WIDTH) for a in (qc, kc, vc))
            yc = yc.reshape(b, lc // 2, 4 * FOURIER_WIDTH)
        qx, kx, vx, yx = proj(x, rope_tabs=rope_tabs, mod_row=None, tm=l, name="proj_lat")

        casts = [(w, i) for w in (w_out, w_gate, w_up, w_down)]
        attn_x, rounded = attend(qx, [kc, kx], [vc, vx], casts=casts, tq=l, heads=1, name="attn_lat")
        ffn = functools.partial(_out_ffn, mod_all=mod_all, g2=g2, w_out=rounded[0], w_gate=rounded[1],
                                w_up=rounded[2], w_down=rounded[3], layer=i, w_layer=0)
        four_x = _fourier(yx, pos_lat, name="fourier_lat")
        x_new = ffn(x, attn_x, four_x, mod_row=None, tm=TOKEN_TILE, name="out_ffn_lat")
        if not last:
            attn_c, _ = attend(qc, [kc], [vc], tq=lc, heads=N_HEADS, name="attn_ctx")
            four_c = _fourier(yc, pos_ctx, name="fourier_ctx")
            ctx = ffn(flat(ctx), flat(attn_c), flat(four_c), mod_row=b, tm=tm_ctx,
                      name="out_ffn_ctx").reshape(b, lc, d)
        x = x_new
    return x
```

```python
import functools
import math

import numpy as np
import jax
import jax.numpy as jnp
from jax import lax
from jax.experimental import pallas as pl
from jax.experimental.pallas import tpu as pltpu

D_MODEL = 1024
GRID_W = 64
ATTN_WIDTH = 512
FOURIER_WIDTH = 512
HEAD_DIM = 64
N_HEADS = 4
V_HEAD_DIM = 128
N_FGROUPS = 4
FGROUP_DIM = 128
IN_COLS = 2048
ROPE_AXIS_DIM = 32
ROPE_THETA = 10000.0
EPS = 1e-6

LANES = 128
MXU_DIM = 256
BF16_SUBLANES = 16
MOD_ROWS = 16
VMEM_LIMIT_BYTES = 56 * 1024 * 1024
SAFE_SCORE_BOUND = 40.0
TOKEN_TILE = 1024
LOG2_E = 1.4426950408889634

F32 = jnp.float32
BF16 = jnp.bfloat16
_NT = (((1,), (1,)), ((), ()))


def _params(*sem):
    return pltpu.CompilerParams(dimension_semantics=sem, vmem_limit_bytes=VMEM_LIMIT_BYTES)


def _resident(shape, layer=None):
    zeros = (0,) * len(shape)
    if layer is None:
        return pl.BlockSpec(shape, lambda *_: zeros, pipeline_mode=pl.Buffered(1))
    return pl.BlockSpec((None,) + tuple(shape), lambda *_: (layer,) + zeros, pipeline_mode=pl.Buffered(1))


def _mod_spec(layer, row_of):
    return pl.BlockSpec((None, None, 6, D_MODEL), lambda *g: (layer, row_of(*g), 0, 0))


def _rope_tables(n_tokens):
    rows = n_tokens // GRID_W
    row = np.repeat(np.arange(rows, dtype=np.float64), GRID_W)
    col = np.tile(np.arange(GRID_W, dtype=np.float64), rows)
    inv = np.float32(ROPE_THETA) ** (-np.arange(0, ROPE_AXIS_DIM, 2, dtype=np.float32) / np.float32(ROPE_AXIS_DIM))
    inv = inv.astype(np.float64)
    half = ROPE_AXIS_DIM // 2
    d = np.arange(HEAD_DIM)
    pos = np.where(d[None, :] < ROPE_AXIS_DIM, row[:, None], col[:, None])
    ang = pos * inv[d % half][None, :]
    low = (d % ROPE_AXIS_DIM) < half
    cos = np.cos(ang)
    sin = np.where(low[None, :], -np.sin(ang), np.sin(ang))
    tile2 = lambda a: jnp.asarray(np.tile(a, (1, LANES // HEAD_DIM)), F32)
    return tile2(cos), tile2(sin)


def _channel_dft():
    n = FGROUP_DIM
    k = (np.arange(n)[:, None] * np.arange(n)[None, :]) % n
    ang = 2.0 * np.pi * k / n
    m = np.concatenate([np.cos(ang), np.sin(ang)], axis=1) / np.sqrt(n)
    return jnp.asarray(m, F32).astype(BF16)


def _position_dft(n_tokens):
    half = n_tokens // 2
    lo = np.arange(half)[:, None]
    m = np.arange(half)[None, :]
    mats = []
    for s in (0, 1):
        k = ((2 * m + s) * lo) % n_tokens
        ang = 2.0 * np.pi * k / n_tokens
        mats.append(jnp.asarray(np.cos(ang) / np.sqrt(n_tokens), F32).astype(BF16))
        mats.append(jnp.asarray(-np.sin(ang) / np.sqrt(n_tokens), F32).astype(BF16))
    return mats


def _segment_mean():
    seg = np.arange(MXU_DIM) // HEAD_DIM
    return jnp.asarray((seg[:, None] == seg[None, :]) / HEAD_DIM, BF16)


def _mod_kernel(c_ref, w_ref, b_ref, o_ref):
    cv = c_ref[...]
    s = cv / (1.0 + jnp.exp(-cv))
    o_ref[0] = jnp.dot(s.astype(BF16), w_ref[0].astype(BF16), preferred_element_type=F32) + b_ref[0]


def _adaln_mod(cvec, w_ada, b_ada):
    depth, d, n = w_ada.shape
    tn = 1536
    return pl.pallas_call(
        _mod_kernel,
        grid=(depth, n // tn),
        in_specs=[pl.BlockSpec((MOD_ROWS, d), lambda i, j: (0, 0)),
                  pl.BlockSpec((1, d, tn), lambda i, j: (i, 0, j)),
                  pl.BlockSpec((1, 1, tn), lambda i, j: (i, 0, j))],
        out_specs=pl.BlockSpec((1, MOD_ROWS, tn), lambda i, j: (i, 0, j)),
        out_shape=jax.ShapeDtypeStruct((depth, MOD_ROWS, n), F32),
        compiler_params=_params("arbitrary", "arbitrary"),
        name="adaln_mod",
    )(cvec, w_ada, b_ada.reshape(depth, 1, n))


def _proj_kernel(*refs, rope):
    if rope:
        (x_ref, mod_ref, g1_ref, w_ref, qg_ref, kg_ref, seg_ref, cdft_ref,
         cos_ref, sin_ref, q_ref, k_ref, v_ref, y_ref, f_scr) = refs
    else:
        (x_ref, mod_ref, g1_ref, w_ref, qg_ref, kg_ref, seg_ref, cdft_ref,
         q_ref, k_ref, v_ref, y_ref, f_scr) = refs
    sub = f_scr.shape[1]
    tiles = [(bi, r0) for bi in range(x_ref.shape[0]) for r0 in range(0, x_ref.shape[1], sub)]
    swap_idx = lax.broadcasted_iota(jnp.int32, (sub, LANES), 1) ^ (ROPE_AXIS_DIM // 2)

    qk_cols = 2 * ATTN_WIDTH

    gain_scale = g1_ref[...] * (1.0 + mod_ref[1:2, :])

    def norm_mod(tile):
        bi, r0 = tile
        x = x_ref[bi, r0:r0 + sub, :]
        ms = jnp.mean(x * x, axis=-1, keepdims=True)
        return (x * lax.rsqrt(ms + EPS) * gain_scale + mod_ref[0:1, :]).astype(BF16)

    def project_qk(h):
        return jnp.dot(h, w_ref[:, 0:qk_cols], preferred_element_type=F32)

    def project_vf(h):
        return jnp.dot(h, w_ref[:, qk_cols:], preferred_element_type=F32)

    def norm_rope(t, gain, out_ref, tile):
        bi, r0 = tile
        for cb in range(ATTN_WIDTH // MXU_DIM):
            tc = t[:, MXU_DIM * cb:MXU_DIM * (cb + 1)]
            ms = jnp.dot((tc * tc).astype(BF16), seg_ref[...], preferred_element_type=F32)
            tn = tc * lax.rsqrt(ms + EPS) * gain
            for hb in range(MXU_DIM // LANES):
                u = tn[:, LANES * hb:LANES * (hb + 1)]
                if rope:
                    partner = jnp.take_along_axis(u, swap_idx, axis=1)
                    u = u * cos_ref[r0:r0 + sub, :] + partner * sin_ref[r0:r0 + sub, :]
                c0 = MXU_DIM * cb + LANES * hb
                out_ref[bi, r0:r0 + sub, c0:c0 + LANES] = u.astype(BF16)

    def emit_qk(tile, z):
        norm_rope(z[:, 0:ATTN_WIDTH], qg_ref[...], q_ref, tile)
        norm_rope(z[:, ATTN_WIDTH:], kg_ref[...], k_ref, tile)

    def emit_vf(tile, z):
        bi, r0 = tile
        v_ref[bi, r0:r0 + sub, :] = z[:, 0:ATTN_WIDTH].astype(BF16)
        w = FOURIER_WIDTH
        h0 = r0 // 2
        for g in range(N_FGROUPS):
            c0 = FGROUP_DIM * g
            f_scr[g] = z[:, ATTN_WIDTH + c0:ATTN_WIDTH + c0 + FGROUP_DIM]
            for parity in range(2):
                fp = f_scr[g, pl.ds(parity, sub // 2, stride=2), :].astype(BF16)
                yg = jnp.dot(fp, cdft_ref[g], preferred_element_type=F32)
                base = 2 * w * parity
                y_ref[bi, h0:h0 + sub // 2, base + c0:base + c0 + FGROUP_DIM] = (
                    yg[:, :FGROUP_DIM].astype(BF16))
                y_ref[bi, h0:h0 + sub // 2, base + w + c0:base + w + c0 + FGROUP_DIM] = (
                    yg[:, FGROUP_DIM:].astype(BF16))

    h = norm_mod(tiles[0])
    z_qk = project_qk(h)
    pending_vf = None
    for t, tile in enumerate(tiles):
        if pending_vf is not None:
            emit_vf(*pending_vf)
        pending_vf = (tile, project_vf(h))
        emit_qk(tile, z_qk)
        if t + 1 < len(tiles):
            h = norm_mod(tiles[t + 1])
            z_qk = project_qk(h)
    emit_vf(*pending_vf)


def _project(x, mod_all, g1, w_in, qg, kg, seg, cdft, rope_tabs, *, layer, w_layer, mod_row, nb=1, tm, name):
    b, l, d = x.shape
    rope = rope_tabs is not None
    assert nb == 1 or mod_row is not None
    row_of = (lambda i, t: i) if mod_row is None else (lambda i, t: mod_row)
    tok = lambda width: pl.BlockSpec((nb, tm, width), lambda i, t: (i, t, 0))
    in_specs = [tok(d),
                _mod_spec(layer, row_of),
                _resident((1, d), layer),
                _resident((d, IN_COLS), w_layer),
                _resident((1, MXU_DIM), layer),
                _resident((1, MXU_DIM), layer),
                _resident((MXU_DIM, MXU_DIM)),
                _resident((N_FGROUPS, FGROUP_DIM, 2 * FGROUP_DIM), layer)]
    args = [x, mod_all, g1, w_in, qg, kg, seg, cdft]
    if rope:
        in_specs += [pl.BlockSpec((tm, LANES), lambda i, t: (t, 0))] * 2
        args += list(rope_tabs)
    out = lambda width: jax.ShapeDtypeStruct((b, l, width), BF16)
    return pl.pallas_call(
        functools.partial(_proj_kernel, rope=rope),
        grid=(b // nb, l // tm),
        in_specs=in_specs,
        out_specs=[tok(ATTN_WIDTH), tok(ATTN_WIDTH), tok(ATTN_WIDTH),
                   pl.BlockSpec((nb, tm // 2, 4 * FOURIER_WIDTH), lambda i, t: (i, t, 0))],
        out_shape=[out(ATTN_WIDTH), out(ATTN_WIDTH), out(ATTN_WIDTH),
                   jax.ShapeDtypeStruct((b, l // 2, 4 * FOURIER_WIDTH), BF16)],
        scratch_shapes=[pltpu.VMEM((N_FGROUPS, min(tm, MXU_DIM), FGROUP_DIM), F32)],
        compiler_params=_params("arbitrary", "arbitrary"),
        name=name,
    )(*args)


def _proj_kv_kernel(x_ref, mod_ref, g1_ref, wk_ref, wv_ref, kg_ref, seg_ref, k_ref, v_ref, *, sub):
    gain_scale = g1_ref[...] * (1.0 + mod_ref[1:2, :])
    tiles = [(bi, r0) for bi in range(x_ref.shape[0]) for r0 in range(0, x_ref.shape[1], sub)]
    for bi, r0 in tiles:
        x = x_ref[bi, r0:r0 + sub, :]
        ms = jnp.mean(x * x, axis=-1, keepdims=True)
        h = (x * lax.rsqrt(ms + EPS) * gain_scale + mod_ref[0:1, :]).astype(BF16)
        zk = jnp.dot(h, wk_ref[...], preferred_element_type=F32)
        v_ref[bi, r0:r0 + sub, :] = jnp.dot(h, wv_ref[...], preferred_element_type=F32).astype(BF16)
        for cb in range(ATTN_WIDTH // MXU_DIM):
            cols = pl.ds(MXU_DIM * cb, MXU_DIM)
            tc = zk[:, MXU_DIM * cb:MXU_DIM * (cb + 1)]
            seg_ms = jnp.dot((tc * tc).astype(BF16), seg_ref[...], preferred_element_type=F32)
            k_ref[bi, r0:r0 + sub, cols] = (tc * lax.rsqrt(seg_ms + EPS) * kg_ref[...]).astype(BF16)


def _project_kv(x, mod_all, g1, w_in, kg, seg, *, layer, w_layer, mod_row, nb, tm, name):
    b, l, d = x.shape
    tok = lambda width: pl.BlockSpec((nb, tm, width), lambda i, t: (i, t, 0))
    w_cols = lambda blk: pl.BlockSpec((None, d, ATTN_WIDTH), lambda i, t: (w_layer, 0, blk),
                                      pipeline_mode=pl.Buffered(1))
    out = jax.ShapeDtypeStruct((b, l, ATTN_WIDTH), BF16)
    return pl.pallas_call(
        functools.partial(_proj_kv_kernel, sub=min(tm, MXU_DIM)),
        grid=(b // nb, l // tm),
        in_specs=[tok(d), _mod_spec(layer, lambda i, t: mod_row), _resident((1, d), layer),
                  w_cols(1), w_cols(2),
                  _resident((1, MXU_DIM), layer), _resident((MXU_DIM, MXU_DIM))],
        out_specs=[tok(ATTN_WIDTH), tok(ATTN_WIDTH)],
        out_shape=[out, out],
        compiler_params=_params("arbitrary", "arbitrary"),
        name=name,
    )(x, mod_all, g1, w_in, w_in, kg, seg)


def _attn_kernel(*refs, n_seg, n_cast, lambda_init, sub_rows):
    bound_ref, q_ref = refs[0], refs[1]
    k_refs = refs[2:2 + n_seg]
    v_refs = refs[2 + n_seg:2 + 2 * n_seg]
    lam_ref, sg_ref = refs[2 + 2 * n_seg:4 + 2 * n_seg]
    cast_in = refs[4 + 2 * n_seg:4 + 2 * n_seg + n_cast]
    o_ref = refs[4 + 2 * n_seg + n_cast]
    cast_out = refs[5 + 2 * n_seg + n_cast:]


    def attend(bounded):
        wide_cache.clear()
        for src, dst in zip(cast_in, cast_out):
            dst[...] = src[...].astype(BF16)
        lv = lam_ref[...]
        lam = (jnp.exp(jnp.sum(lv[0:1] * lv[1:2], axis=-1, keepdims=True))
               - jnp.exp(jnp.sum(lv[2:3] * lv[3:4], axis=-1, keepdims=True)) + lambda_init)
        lane = lax.broadcasted_iota(jnp.int32, (1, LANES), 1)
        first = (lane < HEAD_DIM).astype(BF16)
        items = [(pl.ds(LANES * h, LANES), pl.ds(r0, sub_rows))
                 for h in range(q_ref.shape[2] // LANES) for r0 in range(0, q_ref.shape[1], sub_rows)]
        probs = scores(items[0], first, bounded)
        for t, item in enumerate(items):
            nxt = scores(items[t + 1], first, bounded) if t + 1 < len(items) else None
            finish(item, probs, lam)
            probs = nxt

    def scores(item, first, bounded):
        cols, rows = item
        q = q_ref[0, rows, cols]

        def one_map(qm):
            ss = [lax.dot_general(qm, k[0, :, cols], _NT, preferred_element_type=F32) for k in k_refs]
            if bounded:
                ps = [jnp.exp2(s) for s in ss]
            else:
                m = functools.reduce(jnp.maximum, [jnp.max(s, axis=-1, keepdims=True) for s in ss])
                ps = [jnp.exp2(s - m) for s in ss]
            l = functools.reduce(lambda a, c: a + c, [jnp.sum(p, axis=-1, keepdims=True) for p in ps])
            return [p.astype(BF16) for p in ps], l

        return one_map(q * first), one_map(q * (1.0 - first).astype(BF16))

    wide_cache = {}

    def wide_v(cols):
        if cols.start not in wide_cache:
            wide_cache[cols.start] = [
                jnp.concatenate([v[0, :, cols]] * (MXU_DIM // LANES), axis=1) for v in v_refs]
        return wide_cache[cols.start]

    def finish(item, probs, lam):
        cols, rows = item
        (p1, l1), (p2, l2) = probs
        c = (lam * l1 / l2).astype(BF16)
        o = functools.reduce(lambda a, b: a + b, [
            jnp.dot(pa - c * pb, vw, preferred_element_type=F32)[:, :LANES]
            for pa, pb, vw in zip(p1, p2, wide_v(cols))])
        o = o * (1.0 / l1)
        ms = jnp.mean(o * o, axis=-1, keepdims=True)
        o = o * lax.rsqrt(ms + EPS) * sg_ref[...] * (1.0 - lambda_init)
        o_ref[0, rows, cols] = o.astype(BF16)

    safe = bound_ref[0, 0] <= SAFE_SCORE_BOUND
    pl.when(safe)(lambda: attend(True))
    pl.when(jnp.logical_not(safe))(lambda: attend(False))


def _attention(bound, q, ks, vs, lam_rows, subln_g, casts=(), *, layer, lambda_init, tq, heads, name):
    b, lq, _ = q.shape
    n_seg = len(ks)
    width = heads * LANES
    grid = (b, N_HEADS // heads, lq // tq)
    qo_spec = pl.BlockSpec((1, tq, width), lambda i, h, t: (i, t, h))
    kv_specs = [pl.BlockSpec((1, k.shape[1], width), lambda i, h, t: (i, 0, h)) for k in ks]

    n_steps = grid[0] * grid[1] * grid[2]
    step_of = lambda i, h, t: (i * grid[1] + h) * grid[2] + t
    cast_in_specs, cast_out_specs, cast_shapes = [], [], []
    for w, w_layer in casts:
        _, r, c = w.shape
        per = next(p for p in range(1, n_steps + 1)
                   if n_steps % p == 0 and r % (n_steps // p) == 0
                   and (r // (n_steps // p)) % BF16_SUBLANES == 0)
        rows = r // (n_steps // per)
        cast_in_specs.append(pl.BlockSpec(
            (None, rows, c), lambda i, h, t, w_layer=w_layer, per=per: (w_layer, step_of(i, h, t) // per, 0)))
        cast_out_specs.append(pl.BlockSpec(
            (None, rows, c), lambda i, h, t, per=per: (0, step_of(i, h, t) // per, 0)))
        cast_shapes.append(jax.ShapeDtypeStruct((1, r, c), BF16))

    out = pl.pallas_call(
        functools.partial(_attn_kernel, n_seg=n_seg, n_cast=len(casts), lambda_init=lambda_init,
                          sub_rows=min(tq, MXU_DIM)),
        grid=grid,
        in_specs=[pl.BlockSpec(memory_space=pltpu.SMEM), qo_spec] + kv_specs + kv_specs
                 + [_resident((4, HEAD_DIM), layer), _resident((1, V_HEAD_DIM), layer)] + cast_in_specs,
        out_specs=[qo_spec] + cast_out_specs,
        out_shape=[jax.ShapeDtypeStruct((b, lq, ATTN_WIDTH), BF16)] + cast_shapes,
        compiler_params=_params("arbitrary", "arbitrary", "arbitrary"),
        name=name,
    )(bound, q, *ks, *vs, lam_rows, subln_g, *[w for w, _ in casts])
    return out[0], out[1:]


def _fold_kernel(cdft_ref, wf_ref, o_ref):
    for g in range(N_FGROUPS):
        wg = wf_ref[g].astype(BF16)
        for part in range(2):
            cols = pl.ds(FGROUP_DIM * part, FGROUP_DIM)
            o_ref[g, :, cols] = jnp.dot(cdft_ref[:, cols], wg, preferred_element_type=F32).astype(BF16)


def _fold_group_map(cdft, w_fourier):
    depth = w_fourier.shape[0]
    return pl.pallas_call(
        _fold_kernel,
        grid=(depth,),
        in_specs=[_resident((FGROUP_DIM, 2 * FGROUP_DIM)),
                  pl.BlockSpec((None, N_FGROUPS, FGROUP_DIM, FGROUP_DIM), lambda i: (i, 0, 0, 0))],
        out_specs=pl.BlockSpec((None, N_FGROUPS, FGROUP_DIM, 2 * FGROUP_DIM), lambda i: (i, 0, 0, 0)),
        out_shape=jax.ShapeDtypeStruct((depth, N_FGROUPS, FGROUP_DIM, 2 * FGROUP_DIM), BF16),
        compiler_params=_params("arbitrary"),
        name="fold_group_map",
    )(cdft, w_fourier)


def _fourier_kernel(y_ref, ce_ref, se_ref, co_ref, so_ref, o_ref):
    w = FOURIER_WIDTH
    e = (jnp.dot(ce_ref[...], y_ref[0, :, 0:w], preferred_element_type=F32)
         + jnp.dot(se_ref[...], y_ref[0, :, w:2 * w], preferred_element_type=F32))
    o = (jnp.dot(co_ref[...], y_ref[0, :, 2 * w:3 * w], preferred_element_type=F32)
         + jnp.dot(so_ref[...], y_ref[0, :, 3 * w:4 * w], preferred_element_type=F32))
    o_ref[0, 0] = (e + o).astype(BF16)
    o_ref[0, 1] = (e - o).astype(BF16)


def _fourier(y2, pos_mats, *, name):
    b, half, _ = y2.shape
    out = pl.pallas_call(
        _fourier_kernel,
        grid=(b,),
        in_specs=[pl.BlockSpec((1, half, 4 * FOURIER_WIDTH), lambda i: (i, 0, 0))]
                 + [_resident((half, half))] * 4,
        out_specs=pl.BlockSpec((1, 2, half, FOURIER_WIDTH), lambda i: (i, 0, 0, 0)),
        out_shape=jax.ShapeDtypeStruct((b, 2, half, FOURIER_WIDTH), BF16),
        compiler_params=_params("arbitrary"),
        name=name,
    )(y2, *pos_mats)
    return out.reshape(b, 2 * half, FOURIER_WIDTH)


def _ffn_kernel(x_ref, a_ref, f_ref, mod_ref, g2_ref, wo_ref, wg_ref, wu_ref, wd_ref, o_ref, *, sub):
    tiles = [(bi, r0) for bi in range(x_ref.shape[0]) for r0 in range(0, x_ref.shape[1], sub)]
    gain_scale = g2_ref[...] * (1.0 + mod_ref[4:5, :])

    def mix_norm(tile):
        bi, rows = tile[0], pl.ds(tile[1], sub)
        mix = (jnp.dot(a_ref[bi, rows, :], wo_ref[0:ATTN_WIDTH, :], preferred_element_type=F32)
               + jnp.dot(f_ref[bi, rows, :], wo_ref[ATTN_WIDTH:, :], preferred_element_type=F32))
        x1 = x_ref[bi, rows, :] + mod_ref[2:3, :] * mix
        ms = jnp.mean(x1 * x1, axis=-1, keepdims=True)
        h = (x1 * lax.rsqrt(ms + EPS) * gain_scale + mod_ref[3:4, :]).astype(BF16)
        return x1, h

    def swiglu(tile, x1, h):
        gate = jnp.dot(h, wg_ref[...], preferred_element_type=F32)
        up = jnp.dot(h, wu_ref[...], preferred_element_type=F32)
        act = (gate / (1.0 + jnp.exp(-gate)) * up).astype(BF16)
        y = jnp.dot(act, wd_ref[...], preferred_element_type=F32)
        o_ref[tile[0], pl.ds(tile[1], sub), :] = x1 + mod_ref[5:6, :] * y

    cur = mix_norm(tiles[0])
    for t, tile in enumerate(tiles):
        nxt = mix_norm(tiles[t + 1]) if t + 1 < len(tiles) else None
        swiglu(tile, *cur)
        cur = nxt


def _out_ffn(x, attn, four, mod_all, g2, w_out, w_gate, w_up, w_down, *, layer, w_layer, mod_row, nb=1, tm, name):
    b, l, d = x.shape
    d_ff = w_gate.shape[-1]
    assert nb == 1 or mod_row is not None
    row_of = (lambda i, t: i) if mod_row is None else (lambda i, t: mod_row)
    tok = lambda width: pl.BlockSpec((nb, tm, width), lambda i, t: (i, t, 0))
    return pl.pallas_call(
        functools.partial(_ffn_kernel, sub=min(tm, MXU_DIM)),
        grid=(b // nb, l // tm),
        in_specs=[tok(d), tok(ATTN_WIDTH), tok(FOURIER_WIDTH),
                  _mod_spec(layer, row_of),
                  _resident((1, d), layer),
                  _resident((d, d), w_layer),
                  _resident((d, d_ff), w_layer),
                  _resident((d, d_ff), w_layer),
                  _resident((d_ff, d), w_layer)],
        out_specs=tok(d),
        out_shape=jax.ShapeDtypeStruct((b, l, d), F32),
        compiler_params=_params("arbitrary", "arbitrary"),
        name=name,
    )(x, attn, four, mod_all, g2, w_out, w_gate, w_up, w_down)


def kernel(x, c, ctx, c_ctx, w_ada, b_ada, norm1_g, norm2_g, w_in, q_norm_g, k_norm_g, lambda_q1, lambda_k1, lambda_q2, lambda_k2, subln_g, w_fourier, w_out, w_gate, w_up, w_down):
    depth = w_ada.shape[0]
    b, l, d = x.shape
    lc = ctx.shape[1]
    assert b + 1 <= MOD_ROWS and lc % 2 == 0
    nb_ctx = math.gcd(b, max(1, TOKEN_TILE // 2 // lc))

    rope_tabs = _rope_tables(l)
    cdft = _fold_group_map(_channel_dft(), w_fourier)
    pos_lat = _position_dft(l)
    pos_ctx = _position_dft(lc)
    seg = _segment_mean()

    cvec = jnp.zeros((MOD_ROWS, d), F32).at[:b].set(c).at[b].set(c_ctx)
    mod_all = _adaln_mod(cvec, w_ada, b_ada).reshape(depth, MOD_ROWS, 6, d)

    w_in_b = w_in.astype(BF16)
    g1, g2 = norm1_g.reshape(depth, 1, d), norm2_g.reshape(depth, 1, d)
    tile_gain = lambda g: jnp.tile(g, (1, MXU_DIM // HEAD_DIM)).reshape(depth, 1, MXU_DIM)
    qg, kg = tile_gain(q_norm_g * (HEAD_DIM ** -0.5 * LOG2_E)), tile_gain(k_norm_g)
    lam_rows = jnp.stack([lambda_q1, lambda_k1, lambda_q2, lambda_k2], axis=1)
    sg = subln_g.reshape(depth, 1, V_HEAD_DIM)
    score_bound = (HEAD_DIM ** 0.5 * LOG2_E) * jnp.max(jnp.abs(q_norm_g), axis=1) * jnp.max(jnp.abs(k_norm_g), axis=1)

    for i in range(depth):
        last = i == depth - 1
        lambda_init = 0.8 - 0.6 * math.exp(-0.3 * i)
        bound = score_bound[i].reshape(1, 1)
        proj = functools.partial(_project, mod_all=mod_all, g1=g1, w_in=w_in_b, qg=qg, kg=kg,
                                 seg=seg, cdft=cdft, layer=i, w_layer=i)
        attend = functools.partial(_attention, bound, lam_rows=lam_rows, subln_g=sg, layer=i,
                                   lambda_init=lambda_init)

        if last:
            kc, vc = _project_kv(ctx, mod_all, g1, w_in_b, kg, seg, layer=i, w_layer=i,
                                 mod_row=b, nb=nb_ctx, tm=lc, name="proj_ctx_kv")
        else:
            qc, kc, vc, yc = proj(ctx, rope_tabs=None, mod_row=b, nb=nb_ctx, tm=lc, name="proj_ctx")
        qx, kx, vx, yx = proj(x, rope_tabs=rope_tabs, mod_row=None, tm=l, name="proj_lat")

        casts = [(w, i) for w in (w_out, w_gate, w_up, w_down)]
        attn_x, rounded = attend(qx, [kc, kx], [vc, vx], casts=casts, tq=l, heads=1, name="attn_lat")
        ffn = functools.partial(_out_ffn, mod_all=mod_all, g2=g2, w_out=rounded[0], w_gate=rounded[1],
                                w_up=rounded[2], w_down=rounded[3], layer=i, w_layer=0)
        four_x = _fourier(yx, pos_lat, name="fourier_lat")
        x_new = ffn(x, attn_x, four_x, mod_row=None, tm=TOKEN_TILE, name="out_ffn_lat")
        if not last:
            attn_c, _ = attend(qc, [kc], [vc], tq=lc, heads=N_HEADS, name="attn_ctx")
            four_c = _fourier(yc, pos_ctx, name="fourier_ctx")
            ctx = ffn(ctx, attn_c, four_c, mod_row=b, nb=nb_ctx, tm=lc, name="out_ffn_ctx")
        x = x_new
    return x
```

```python
import functools
import math

import numpy as np
import jax
import jax.numpy as jnp
from jax import lax
from jax.experimental import pallas as pl
from jax.experimental.pallas import tpu as pltpu

D_MODEL = 1024
GRID_W = 64
ATTN_WIDTH = 512
FOURIER_WIDTH = 512
HEAD_DIM = 64
N_HEADS = 4
V_HEAD_DIM = 128
N_FGROUPS = 4
FGROUP_DIM = 128
IN_COLS = 2048
ROPE_AXIS_DIM = 32
ROPE_THETA = 10000.0
EPS = 1e-6

LANES = 128
MXU_DIM = 256
BF16_SUBLANES = 16
MOD_ROWS = 16
VMEM_LIMIT_BYTES = 56 * 1024 * 1024
SAFE_SCORE_BOUND = 40.0
TOKEN_TILE = 1024
LOG2_E = 1.4426950408889634

F32 = jnp.float32
BF16 = jnp.bfloat16
_NT = (((1,), (1,)), ((), ()))


def _params(*sem):
    return pltpu.CompilerParams(dimension_semantics=sem, vmem_limit_bytes=VMEM_LIMIT_BYTES)


def _resident(shape, layer=None):
    zeros = (0,) * len(shape)
    if layer is None:
        return pl.BlockSpec(shape, lambda *_: zeros, pipeline_mode=pl.Buffered(1))
    return pl.BlockSpec((None,) + tuple(shape), lambda *_: (layer,) + zeros, pipeline_mode=pl.Buffered(1))


def _mod_spec(layer, row_of):
    return pl.BlockSpec((None, None, 6, D_MODEL), lambda *g: (layer, row_of(*g), 0, 0))


def _rope_tables(n_tokens):
    rows = n_tokens // GRID_W
    row = np.repeat(np.arange(rows, dtype=np.float64), GRID_W)
    col = np.tile(np.arange(GRID_W, dtype=np.float64), rows)
    inv = np.float32(ROPE_THETA) ** (-np.arange(0, ROPE_AXIS_DIM, 2, dtype=np.float32) / np.float32(ROPE_AXIS_DIM))
    inv = inv.astype(np.float64)
    half = ROPE_AXIS_DIM // 2
    d = np.arange(HEAD_DIM)
    pos = np.where(d[None, :] < ROPE_AXIS_DIM, row[:, None], col[:, None])
    ang = pos * inv[d % half][None, :]
    low = (d % ROPE_AXIS_DIM) < half
    cos = np.cos(ang)
    sin = np.where(low[None, :], -np.sin(ang), np.sin(ang))
    tile2 = lambda a: jnp.asarray(np.tile(a, (1, LANES // HEAD_DIM)), F32)
    return tile2(cos), tile2(sin)


def _channel_dft():
    n = FGROUP_DIM
    k = (np.arange(n)[:, None] * np.arange(n)[None, :]) % n
    ang = 2.0 * np.pi * k / n
    m = np.concatenate([np.cos(ang), np.sin(ang)], axis=1) / np.sqrt(n)
    return jnp.asarray(m, F32).astype(BF16)


def _position_dft(n_tokens):
    half = n_tokens // 2
    lo = np.arange(half)[:, None]
    m = np.arange(half)[None, :]
    mats = []
    for s in (0, 1):
        k = ((2 * m + s) * lo) % n_tokens
        ang = 2.0 * np.pi * k / n_tokens
        mats.append(jnp.asarray(np.cos(ang) / np.sqrt(n_tokens), F32).astype(BF16))
        mats.append(jnp.asarray(-np.sin(ang) / np.sqrt(n_tokens), F32).astype(BF16))
    return mats


def _segment_mean():
    seg = np.arange(MXU_DIM) // HEAD_DIM
    return jnp.asarray((seg[:, None] == seg[None, :]) / HEAD_DIM, BF16)


def _mod_kernel(c_ref, w_ref, b_ref, o_ref):
    cv = c_ref[...]
    s = cv / (1.0 + jnp.exp(-cv))
    o_ref[0] = jnp.dot(s.astype(BF16), w_ref[0].astype(BF16), preferred_element_type=F32) + b_ref[0]


def _adaln_mod(cvec, w_ada, b_ada):
    depth, d, n = w_ada.shape
    tn = 1536
    return pl.pallas_call(
        _mod_kernel,
        grid=(depth, n // tn),
        in_specs=[pl.BlockSpec((MOD_ROWS, d), lambda i, j: (0, 0)),
                  pl.BlockSpec((1, d, tn), lambda i, j: (i, 0, j)),
                  pl.BlockSpec((1, 1, tn), lambda i, j: (i, 0, j))],
        out_specs=pl.BlockSpec((1, MOD_ROWS, tn), lambda i, j: (i, 0, j)),
        out_shape=jax.ShapeDtypeStruct((depth, MOD_ROWS, n), F32),
        compiler_params=_params("arbitrary", "arbitrary"),
        name="adaln_mod",
    )(cvec, w_ada, b_ada.reshape(depth, 1, n))


def _proj_kernel(*refs, rope):
    if rope:
        (x_ref, mod_ref, g1_ref, w_ref, qg_ref, kg_ref, seg_ref, cdft_ref,
         cos_ref, sin_ref, q_ref, k_ref, v_ref, y_ref, f_scr) = refs
    else:
        (x_ref, mod_ref, g1_ref, w_ref, qg_ref, kg_ref, seg_ref, cdft_ref,
         q_ref, k_ref, v_ref, y_ref, f_scr) = refs
    tm = x_ref.shape[1]
    sub = f_scr.shape[1]
    swap_idx = lax.broadcasted_iota(jnp.int32, (sub, LANES), 1) ^ (ROPE_AXIS_DIM // 2)

    qk_cols = 2 * ATTN_WIDTH

    gain_scale = g1_ref[...] * (1.0 + mod_ref[1:2, :])

    def norm_mod(r0):
        x = x_ref[0, r0:r0 + sub, :]
        ms = jnp.mean(x * x, axis=-1, keepdims=True)
        return (x * lax.rsqrt(ms + EPS) * gain_scale + mod_ref[0:1, :]).astype(BF16)

    def project_qk(h):
        return jnp.dot(h, w_ref[:, 0:qk_cols], preferred_element_type=F32)

    def project_vf(h):
        return jnp.dot(h, w_ref[:, qk_cols:], preferred_element_type=F32)

    def norm_rope(t, gain, out_ref, r0):
        for cb in range(ATTN_WIDTH // MXU_DIM):
            tc = t[:, MXU_DIM * cb:MXU_DIM * (cb + 1)]
            ms = jnp.dot((tc * tc).astype(BF16), seg_ref[...], preferred_element_type=F32)
            tn = tc * lax.rsqrt(ms + EPS) * gain
            for hb in range(MXU_DIM // LANES):
                u = tn[:, LANES * hb:LANES * (hb + 1)]
                if rope:
                    partner = jnp.take_along_axis(u, swap_idx, axis=1)
                    u = u * cos_ref[r0:r0 + sub, :] + partner * sin_ref[r0:r0 + sub, :]
                c0 = MXU_DIM * cb + LANES * hb
                out_ref[0, r0:r0 + sub, c0:c0 + LANES] = u.astype(BF16)

    def emit_qk(r0, z):
        norm_rope(z[:, 0:ATTN_WIDTH], qg_ref[...], q_ref, r0)
        norm_rope(z[:, ATTN_WIDTH:], kg_ref[...], k_ref, r0)

    def emit_vf(r0, z):
        v_ref[0, r0:r0 + sub, :] = z[:, 0:ATTN_WIDTH].astype(BF16)
        w = FOURIER_WIDTH
        h0 = r0 // 2
        for g in range(N_FGROUPS):
            c0 = FGROUP_DIM * g
            f_scr[g] = z[:, ATTN_WIDTH + c0:ATTN_WIDTH + c0 + FGROUP_DIM]
            for parity in range(2):
                fp = f_scr[g, pl.ds(parity, sub // 2, stride=2), :].astype(BF16)
                yg = jnp.dot(fp, cdft_ref[g], preferred_element_type=F32)
                base = 2 * w * parity
                y_ref[0, h0:h0 + sub // 2, base + c0:base + c0 + FGROUP_DIM] = (
                    yg[:, :FGROUP_DIM].astype(BF16))
                y_ref[0, h0:h0 + sub // 2, base + w + c0:base + w + c0 + FGROUP_DIM] = (
                    yg[:, FGROUP_DIM:].astype(BF16))

    starts = list(range(0, tm, sub))
    h = norm_mod(starts[0])
    z_qk = project_qk(h)
    pending_vf = None
    for t, r0 in enumerate(starts):
        if pending_vf is not None:
            emit_vf(*pending_vf)
        pending_vf = (r0, project_vf(h))
        emit_qk(r0, z_qk)
        if t + 1 < len(starts):
            h = norm_mod(starts[t + 1])
            z_qk = project_qk(h)
    emit_vf(*pending_vf)


def _project(x, mod_all, g1, w_in, qg, kg, seg, cdft, rope_tabs, *, layer, w_layer, mod_row, tm, name):
    b, l, d = x.shape
    rope = rope_tabs is not None
    row_of = (lambda i, t: i) if mod_row is None else (lambda i, t: mod_row)
    tok = lambda width: pl.BlockSpec((1, tm, width), lambda i, t: (i, t, 0))
    in_specs = [tok(d),
                _mod_spec(layer, row_of),
                _resident((1, d), layer),
                _resident((d, IN_COLS), w_layer),
                _resident((1, MXU_DIM), layer),
                _resident((1, MXU_DIM), layer),
                _resident((MXU_DIM, MXU_DIM)),
                _resident((N_FGROUPS, FGROUP_DIM, 2 * FGROUP_DIM), layer)]
    args = [x, mod_all, g1, w_in, qg, kg, seg, cdft]
    if rope:
        in_specs += [pl.BlockSpec((tm, LANES), lambda i, t: (t, 0))] * 2
        args += list(rope_tabs)
    out = lambda width: jax.ShapeDtypeStruct((b, l, width), BF16)
    return pl.pallas_call(
        functools.partial(_proj_kernel, rope=rope),
        grid=(b, l // tm),
        in_specs=in_specs,
        out_specs=[tok(ATTN_WIDTH), tok(ATTN_WIDTH), tok(ATTN_WIDTH),
                   pl.BlockSpec((1, tm // 2, 4 * FOURIER_WIDTH), lambda i, t: (i, t, 0))],
        out_shape=[out(ATTN_WIDTH), out(ATTN_WIDTH), out(ATTN_WIDTH),
                   jax.ShapeDtypeStruct((b, l // 2, 4 * FOURIER_WIDTH), BF16)],
        scratch_shapes=[pltpu.VMEM((N_FGROUPS, min(tm, MXU_DIM), FGROUP_DIM), F32)],
        compiler_params=_params("arbitrary", "arbitrary"),
        name=name,
    )(*args)


def _proj_kv_kernel(x_ref, mod_ref, g1_ref, wk_ref, wv_ref, kg_ref, seg_ref, k_ref, v_ref, *, sub):
    gain_scale = g1_ref[...] * (1.0 + mod_ref[1:2, :])
    for r0 in range(0, x_ref.shape[1], sub):
        x = x_ref[0, r0:r0 + sub, :]
        ms = jnp.mean(x * x, axis=-1, keepdims=True)
        h = (x * lax.rsqrt(ms + EPS) * gain_scale + mod_ref[0:1, :]).astype(BF16)
        zk = jnp.dot(h, wk_ref[...], preferred_element_type=F32)
        v_ref[0, r0:r0 + sub, :] = jnp.dot(h, wv_ref[...], preferred_element_type=F32).astype(BF16)
        for cb in range(ATTN_WIDTH // MXU_DIM):
            cols = pl.ds(MXU_DIM * cb, MXU_DIM)
            tc = zk[:, MXU_DIM * cb:MXU_DIM * (cb + 1)]
            seg_ms = jnp.dot((tc * tc).astype(BF16), seg_ref[...], preferred_element_type=F32)
            k_ref[0, r0:r0 + sub, cols] = (tc * lax.rsqrt(seg_ms + EPS) * kg_ref[...]).astype(BF16)


def _project_kv(x, mod_all, g1, w_in, kg, seg, *, layer, w_layer, mod_row, tm, name):
    b, l, d = x.shape
    tok = lambda width: pl.BlockSpec((1, tm, width), lambda i, t: (i, t, 0))
    w_cols = lambda blk: pl.BlockSpec((None, d, ATTN_WIDTH), lambda i, t: (w_layer, 0, blk),
                                      pipeline_mode=pl.Buffered(1))
    out = jax.ShapeDtypeStruct((b, l, ATTN_WIDTH), BF16)
    return pl.pallas_call(
        functools.partial(_proj_kv_kernel, sub=min(tm, MXU_DIM)),
        grid=(b, l // tm),
        in_specs=[tok(d), _mod_spec(layer, lambda i, t: mod_row), _resident((1, d), layer),
                  w_cols(1), w_cols(2),
                  _resident((1, MXU_DIM), layer), _resident((MXU_DIM, MXU_DIM))],
        out_specs=[tok(ATTN_WIDTH), tok(ATTN_WIDTH)],
        out_shape=[out, out],
        compiler_params=_params("arbitrary", "arbitrary"),
        name=name,
    )(x, mod_all, g1, w_in, w_in, kg, seg)


def _attn_kernel(*refs, n_seg, n_cast, lambda_init, sub_rows):
    bound_ref, q_ref = refs[0], refs[1]
    k_refs = refs[2:2 + n_seg]
    v_refs = refs[2 + n_seg:2 + 2 * n_seg]
    lam_ref, sg_ref = refs[2 + 2 * n_seg:4 + 2 * n_seg]
    cast_in = refs[4 + 2 * n_seg:4 + 2 * n_seg + n_cast]
    o_ref = refs[4 + 2 * n_seg + n_cast]
    cast_out = refs[5 + 2 * n_seg + n_cast:]


    def attend(bounded):
        wide_cache.clear()
        for src, dst in zip(cast_in, cast_out):
            dst[...] = src[...].astype(BF16)
        lv = lam_ref[...]
        lam = (jnp.exp(jnp.sum(lv[0:1] * lv[1:2], axis=-1, keepdims=True))
               - jnp.exp(jnp.sum(lv[2:3] * lv[3:4], axis=-1, keepdims=True)) + lambda_init)
        lane = lax.broadcasted_iota(jnp.int32, (1, LANES), 1)
        first = (lane < HEAD_DIM).astype(BF16)
        items = [(pl.ds(LANES * h, LANES), pl.ds(r0, sub_rows))
                 for h in range(q_ref.shape[2] // LANES) for r0 in range(0, q_ref.shape[1], sub_rows)]
        probs = scores(items[0], first, bounded)
        for t, item in enumerate(items):
            nxt = scores(items[t + 1], first, bounded) if t + 1 < len(items) else None
            finish(item, probs, lam)
            probs = nxt

    def scores(item, first, bounded):
        cols, rows = item
        q = q_ref[0, rows, cols]

        def one_map(qm):
            ss = [lax.dot_general(qm, k[0, :, cols], _NT, preferred_element_type=F32) for k in k_refs]
            if bounded:
                ps = [jnp.exp2(s) for s in ss]
            else:
                m = functools.reduce(jnp.maximum, [jnp.max(s, axis=-1, keepdims=True) for s in ss])
                ps = [jnp.exp2(s - m) for s in ss]
            l = functools.reduce(lambda a, c: a + c, [jnp.sum(p, axis=-1, keepdims=True) for p in ps])
            return [p.astype(BF16) for p in ps], l

        return one_map(q * first), one_map(q * (1.0 - first).astype(BF16))

    wide_cache = {}

    def wide_v(cols):
        if cols.start not in wide_cache:
            wide_cache[cols.start] = [
                jnp.concatenate([v[0, :, cols]] * (MXU_DIM // LANES), axis=1) for v in v_refs]
        return wide_cache[cols.start]

    def finish(item, probs, lam):
        cols, rows = item
        (p1, l1), (p2, l2) = probs
        c = (lam * l1 / l2).astype(BF16)
        o = functools.reduce(lambda a, b: a + b, [
            jnp.dot(pa - c * pb, vw, preferred_element_type=F32)[:, :LANES]
            for pa, pb, vw in zip(p1, p2, wide_v(cols))])
        o = o * (1.0 / l1)
        ms = jnp.mean(o * o, axis=-1, keepdims=True)
        o = o * lax.rsqrt(ms + EPS) * sg_ref[...] * (1.0 - lambda_init)
        o_ref[0, rows, cols] = o.astype(BF16)

    safe = bound_ref[0, 0] <= SAFE_SCORE_BOUND
    pl.when(safe)(lambda: attend(True))
    pl.when(jnp.logical_not(safe))(lambda: attend(False))


def _attention(bound, q, ks, vs, lam_rows, subln_g, casts=(), *, layer, lambda_init, tq, heads, name):
    b, lq, _ = q.shape
    n_seg = len(ks)
    width = heads * LANES
    grid = (b, N_HEADS // heads, lq // tq)
    qo_spec = pl.BlockSpec((1, tq, width), lambda i, h, t: (i, t, h))
    kv_specs = [pl.BlockSpec((1, k.shape[1], width), lambda i, h, t: (i, 0, h)) for k in ks]

    n_steps = grid[0] * grid[1] * grid[2]
    step_of = lambda i, h, t: (i * grid[1] + h) * grid[2] + t
    cast_in_specs, cast_out_specs, cast_shapes = [], [], []
    for w, w_layer in casts:
        _, r, c = w.shape
        per = next(p for p in range(1, n_steps + 1)
                   if n_steps % p == 0 and r % (n_steps // p) == 0
                   and (r // (n_steps // p)) % BF16_SUBLANES == 0)
        rows = r // (n_steps // per)
        cast_in_specs.append(pl.BlockSpec(
            (None, rows, c), lambda i, h, t, w_layer=w_layer, per=per: (w_layer, step_of(i, h, t) // per, 0)))
        cast_out_specs.append(pl.BlockSpec(
            (None, rows, c), lambda i, h, t, per=per: (0, step_of(i, h, t) // per, 0)))
        cast_shapes.append(jax.ShapeDtypeStruct((1, r, c), BF16))

    out = pl.pallas_call(
        functools.partial(_attn_kernel, n_seg=n_seg, n_cast=len(casts), lambda_init=lambda_init,
                          sub_rows=min(tq, MXU_DIM)),
        grid=grid,
        in_specs=[pl.BlockSpec(memory_space=pltpu.SMEM), qo_spec] + kv_specs + kv_specs
                 + [_resident((4, HEAD_DIM), layer), _resident((1, V_HEAD_DIM), layer)] + cast_in_specs,
        out_specs=[qo_spec] + cast_out_specs,
        out_shape=[jax.ShapeDtypeStruct((b, lq, ATTN_WIDTH), BF16)] + cast_shapes,
        compiler_params=_params("arbitrary", "arbitrary", "arbitrary"),
        name=name,
    )(bound, q, *ks, *vs, lam_rows, subln_g, *[w for w, _ in casts])
    return out[0], out[1:]


def _fold_kernel(cdft_ref, wf_ref, o_ref):
    for g in range(N_FGROUPS):
        wg = wf_ref[g].astype(BF16)
        for part in range(2):
            cols = pl.ds(FGROUP_DIM * part, FGROUP_DIM)
            o_ref[g, :, cols] = jnp.dot(cdft_ref[:, cols], wg, preferred_element_type=F32).astype(BF16)


def _fold_group_map(cdft, w_fourier):
    depth = w_fourier.shape[0]
    return pl.pallas_call(
        _fold_kernel,
        grid=(depth,),
        in_specs=[_resident((FGROUP_DIM, 2 * FGROUP_DIM)),
                  pl.BlockSpec((None, N_FGROUPS, FGROUP_DIM, FGROUP_DIM), lambda i: (i, 0, 0, 0))],
        out_specs=pl.BlockSpec((None, N_FGROUPS, FGROUP_DIM, 2 * FGROUP_DIM), lambda i: (i, 0, 0, 0)),
        out_shape=jax.ShapeDtypeStruct((depth, N_FGROUPS, FGROUP_DIM, 2 * FGROUP_DIM), BF16),
        compiler_params=_params("arbitrary"),
        name="fold_group_map",
    )(cdft, w_fourier)


def _fourier_kernel(y_ref, ce_ref, se_ref, co_ref, so_ref, o_ref):
    w = FOURIER_WIDTH
    e = (jnp.dot(ce_ref[...], y_ref[0, :, 0:w], preferred_element_type=F32)
         + jnp.dot(se_ref[...], y_ref[0, :, w:2 * w], preferred_element_type=F32))
    o = (jnp.dot(co_ref[...], y_ref[0, :, 2 * w:3 * w], preferred_element_type=F32)
         + jnp.dot(so_ref[...], y_ref[0, :, 3 * w:4 * w], preferred_element_type=F32))
    o_ref[0, 0] = (e + o).astype(BF16)
    o_ref[0, 1] = (e - o).astype(BF16)


def _fourier(y2, pos_mats, *, name):
    b, half, _ = y2.shape
    out = pl.pallas_call(
        _fourier_kernel,
        grid=(b,),
        in_specs=[pl.BlockSpec((1, half, 4 * FOURIER_WIDTH), lambda i: (i, 0, 0))]
                 + [_resident((half, half))] * 4,
        out_specs=pl.BlockSpec((1, 2, half, FOURIER_WIDTH), lambda i: (i, 0, 0, 0)),
        out_shape=jax.ShapeDtypeStruct((b, 2, half, FOURIER_WIDTH), BF16),
        compiler_params=_params("arbitrary"),
        name=name,
    )(y2, *pos_mats)
    return out.reshape(b, 2 * half, FOURIER_WIDTH)


def _ffn_kernel(x_ref, a_ref, f_ref, mod_ref, g2_ref, wo_ref, wg_ref, wu_ref, wd_ref, o_ref, *, sub):
    tm = x_ref.shape[1]
    gain_scale = g2_ref[...] * (1.0 + mod_ref[4:5, :])

    def mix_norm(r0):
        rows = pl.ds(r0, sub)
        mix = (jnp.dot(a_ref[0, rows, :], wo_ref[0:ATTN_WIDTH, :], preferred_element_type=F32)
               + jnp.dot(f_ref[0, rows, :], wo_ref[ATTN_WIDTH:, :], preferred_element_type=F32))
        x1 = x_ref[0, rows, :] + mod_ref[2:3, :] * mix
        ms = jnp.mean(x1 * x1, axis=-1, keepdims=True)
        h = (x1 * lax.rsqrt(ms + EPS) * gain_scale + mod_ref[3:4, :]).astype(BF16)
        return x1, h

    def swiglu(r0, x1, h):
        gate = jnp.dot(h, wg_ref[...], preferred_element_type=F32)
        up = jnp.dot(h, wu_ref[...], preferred_element_type=F32)
        act = (gate / (1.0 + jnp.exp(-gate)) * up).astype(BF16)
        y = jnp.dot(act, wd_ref[...], preferred_element_type=F32)
        o_ref[0, pl.ds(r0, sub), :] = x1 + mod_ref[5:6, :] * y

    starts = list(range(0, tm, sub))
    cur = mix_norm(starts[0])
    for t, r0 in enumerate(starts):
        nxt = mix_norm(starts[t + 1]) if t + 1 < len(starts) else None
        swiglu(r0, *cur)
        cur = nxt


def _out_ffn(x, attn, four, mod_all, g2, w_out, w_gate, w_up, w_down, *, layer, w_layer, mod_row, tm, name):
    b, l, d = x.shape
    d_ff = w_gate.shape[-1]
    row_of = (lambda i, t: i) if mod_row is None else (lambda i, t: mod_row)
    tok = lambda width: pl.BlockSpec((1, tm, width), lambda i, t: (i, t, 0))
    return pl.pallas_call(
        functools.partial(_ffn_kernel, sub=min(tm, MXU_DIM)),
        grid=(b, l // tm),
        in_specs=[tok(d), tok(ATTN_WIDTH), tok(FOURIER_WIDTH),
                  _mod_spec(layer, row_of),
                  _resident((1, d), layer),
                  _resident((d, d), w_layer),
                  _resident((d, d_ff), w_layer),
                  _resident((d, d_ff), w_layer),
                  _resident((d_ff, d), w_layer)],
        out_specs=tok(d),
        out_shape=jax.ShapeDtypeStruct((b, l, d), F32),
        compiler_params=_params("arbitrary", "arbitrary"),
        name=name,
    )(x, attn, four, mod_all, g2, w_out, w_gate, w_up, w_down)


def kernel(x, c, ctx, c_ctx, w_ada, b_ada, norm1_g, norm2_g, w_in, q_norm_g, k_norm_g, lambda_q1, lambda_k1, lambda_q2, lambda_k2, subln_g, w_fourier, w_out, w_gate, w_up, w_down):
    depth = w_ada.shape[0]
    b, l, d = x.shape
    lc = ctx.shape[1]
    assert b + 1 <= MOD_ROWS and lc % 2 == 0
    tm_ctx = math.gcd(b * lc, TOKEN_TILE // 2)

    rope_tabs = _rope_tables(l)
    cdft = _fold_group_map(_channel_dft(), w_fourier)
    pos_lat = _position_dft(l)
    pos_ctx = _position_dft(lc)
    seg = _segment_mean()

    cvec = jnp.concatenate([c, c_ctx[None, :], jnp.zeros((MOD_ROWS - b - 1, d), F32)], axis=0)
    mod_all = _adaln_mod(cvec, w_ada, b_ada).reshape(depth, MOD_ROWS, 6, d)

    w_in_b = w_in.astype(BF16)
    g1, g2 = norm1_g.reshape(depth, 1, d), norm2_g.reshape(depth, 1, d)
    tile_gain = lambda g: jnp.tile(g, (1, MXU_DIM // HEAD_DIM)).reshape(depth, 1, MXU_DIM)
    qg, kg = tile_gain(q_norm_g * (HEAD_DIM ** -0.5 * LOG2_E)), tile_gain(k_norm_g)
    lam_rows = jnp.stack([lambda_q1, lambda_k1, lambda_q2, lambda_k2], axis=1)
    sg = subln_g.reshape(depth, 1, V_HEAD_DIM)
    score_bound = (HEAD_DIM ** 0.5 * LOG2_E) * jnp.max(jnp.abs(q_norm_g), axis=1) * jnp.max(jnp.abs(k_norm_g), axis=1)

    for i in range(depth):
        last = i == depth - 1
        lambda_init = 0.8 - 0.6 * math.exp(-0.3 * i)
        bound = score_bound[i].reshape(1, 1)
        proj = functools.partial(_project, mod_all=mod_all, g1=g1, w_in=w_in_b, qg=qg, kg=kg,
                                 seg=seg, cdft=cdft, layer=i, w_layer=i)
        attend = functools.partial(_attention, bound, lam_rows=lam_rows, subln_g=sg, layer=i,
                                   lambda_init=lambda_init)

        flat = lambda a: a.reshape(1, b * lc, a.shape[-1])
        if last:
            kc, vc = _project_kv(flat(ctx), mod_all, g1, w_in_b, kg, seg, layer=i, w_layer=i,
                                 mod_row=b, tm=tm_ctx, name="proj_ctx_kv")
            kc, vc = (a.reshape(b, lc, ATTN_WIDTH) for a in (kc, vc))
        else:
            qc, kc, vc, yc = proj(flat(ctx), rope_tabs=None, mod_row=b, tm=tm_ctx, name="proj_ctx")
            qc, kc, vc = (a.reshape(b, lc, ATTN_WIDTH) for a in (qc, kc, vc))
            yc = yc.reshape(b, lc // 2, 4 * FOURIER_WIDTH)
        qx, kx, vx, yx = proj(x, rope_tabs=rope_tabs, mod_row=None, tm=l, name="proj_lat")

        casts = [(w, i) for w in (w_out, w_gate, w_up, w_down)]
        attn_x, rounded = attend(qx, [kc, kx], [vc, vx], casts=casts, tq=l, heads=1, name="attn_lat")
        ffn = functools.partial(_out_ffn, mod_all=mod_all, g2=g2, w_out=rounded[0], w_gate=rounded[1],
                                w_up=rounded[2], w_down=rounded[3], layer=i, w_layer=0)
        four_x = _fourier(yx, pos_lat, name="fourier_lat")
        x_new = ffn(x, attn_x, four_x, mod_row=None, tm=TOKEN_TILE, name="out_ffn_lat")
        if not last:
            attn_c, _ = attend(qc, [kc], [vc], tq=lc, heads=N_HEADS, name="attn_ctx")
            four_c = _fourier(yc, pos_ctx, name="fourier_ctx")
            ctx = ffn(flat(ctx), flat(attn_c), flat(four_c), mod_row=b, tm=tm_ctx,
                      name="out_ffn_ctx").reshape(b, lc, d)
        x = x_new
    return x
```

```python
import functools
import math

import numpy as np
import jax
import jax.numpy as jnp
from jax import lax
from jax.experimental import pallas as pl
from jax.experimental.pallas import tpu as pltpu

D_MODEL = 1024
GRID_W = 64
ATTN_WIDTH = 512
FOURIER_WIDTH = 512
HEAD_DIM = 64
N_HEADS = 4
V_HEAD_DIM = 128
N_FGROUPS = 4
FGROUP_DIM = 128
IN_COLS = 2048
ROPE_AXIS_DIM = 32
ROPE_THETA = 10000.0
EPS = 1e-6

LANES = 128
MXU_DIM = 256
BF16_SUBLANES = 16
MOD_ROWS = 16
VMEM_LIMIT_BYTES = 56 * 1024 * 1024
SAFE_SCORE_BOUND = 40.0
TOKEN_TILE = 1024
LOG2_E = 1.4426950408889634

F32 = jnp.float32
BF16 = jnp.bfloat16
_NT = (((1,), (1,)), ((), ()))


def _params(*sem):
    return pltpu.CompilerParams(dimension_semantics=sem, vmem_limit_bytes=VMEM_LIMIT_BYTES)


def _resident(shape, layer=None):
    zeros = (0,) * len(shape)
    if layer is None:
        return pl.BlockSpec(shape, lambda *_: zeros, pipeline_mode=pl.Buffered(1))
    return pl.BlockSpec((None,) + tuple(shape), lambda *_: (layer,) + zeros, pipeline_mode=pl.Buffered(1))


def _mod_spec(layer, row_of):
    return pl.BlockSpec((None, None, 6, D_MODEL), lambda *g: (layer, row_of(*g), 0, 0))


def _rope_tables(n_tokens):
    rows = n_tokens // GRID_W
    row = np.repeat(np.arange(rows, dtype=np.float64), GRID_W)
    col = np.tile(np.arange(GRID_W, dtype=np.float64), rows)
    inv = np.float32(ROPE_THETA) ** (-np.arange(0, ROPE_AXIS_DIM, 2, dtype=np.float32) / np.float32(ROPE_AXIS_DIM))
    inv = inv.astype(np.float64)
    half = ROPE_AXIS_DIM // 2
    d = np.arange(HEAD_DIM)
    pos = np.where(d[None, :] < ROPE_AXIS_DIM, row[:, None], col[:, None])
    ang = pos * inv[d % half][None, :]
    low = (d % ROPE_AXIS_DIM) < half
    cos = np.cos(ang)
    sin = np.where(low[None, :], -np.sin(ang), np.sin(ang))
    tile2 = lambda a: jnp.asarray(np.tile(a, (1, LANES // HEAD_DIM)), F32)
    return tile2(cos), tile2(sin)


def _channel_dft():
    n = FGROUP_DIM
    k = (np.arange(n)[:, None] * np.arange(n)[None, :]) % n
    ang = 2.0 * np.pi * k / n
    m = np.concatenate([np.cos(ang), np.sin(ang)], axis=1) / np.sqrt(n)
    return jnp.asarray(m, F32).astype(BF16)


def _position_dft(n_tokens):
    half = n_tokens // 2
    lo = np.arange(half)[:, None]
    m = np.arange(half)[None, :]
    mats = []
    for s in (0, 1):
        k = ((2 * m + s) * lo) % n_tokens
        ang = 2.0 * np.pi * k / n_tokens
        mats.append(jnp.asarray(np.cos(ang) / np.sqrt(n_tokens), F32).astype(BF16))
        mats.append(jnp.asarray(-np.sin(ang) / np.sqrt(n_tokens), F32).astype(BF16))
    return mats


def _segment_mean():
    seg = np.arange(MXU_DIM) // HEAD_DIM
    return jnp.asarray((seg[:, None] == seg[None, :]) / HEAD_DIM, BF16)


MOD_RING = 3


def _mod_kernel(c_ref, w_hbm, b_ref, o_ref, wbuf, sem, *, n_col_blocks, n_steps):
    step = pl.program_id(0) * n_col_blocks + pl.program_id(1)
    tn = wbuf.shape[2]

    def block_copy(k):
        col0 = pl.multiple_of((k % n_col_blocks) * tn, LANES)
        return pltpu.make_async_copy(w_hbm.at[k // n_col_blocks, :, pl.ds(col0, tn)],
                                     wbuf.at[k % MOD_RING], sem.at[k % MOD_RING])

    @pl.when(step == 0)
    def _():
        for k in range(MOD_RING - 1):
            block_copy(step + k).start()

    @pl.when(step + (MOD_RING - 1) < n_steps)
    def _():
        block_copy(step + (MOD_RING - 1)).start()

    block_copy(step).wait()
    cv = c_ref[...]
    s = cv / (1.0 + jnp.exp(-cv))
    w = wbuf[step % MOD_RING].astype(BF16)
    o_ref[0] = jnp.dot(s.astype(BF16), w, preferred_element_type=F32) + b_ref[0]


def _adaln_mod(cvec, w_ada, b_ada):
    depth, d, n = w_ada.shape
    tn = 1536
    n_steps = depth * (n // tn)
    assert n_steps >= MOD_RING - 1
    return pl.pallas_call(
        functools.partial(_mod_kernel, n_col_blocks=n // tn, n_steps=n_steps),
        grid=(depth, n // tn),
        in_specs=[pl.BlockSpec((MOD_ROWS, d), lambda i, j: (0, 0)),
                  pl.BlockSpec(memory_space=pl.ANY),
                  pl.BlockSpec((1, 1, tn), lambda i, j: (i, 0, j))],
        out_specs=pl.BlockSpec((1, MOD_ROWS, tn), lambda i, j: (i, 0, j)),
        out_shape=jax.ShapeDtypeStruct((depth, MOD_ROWS, n), F32),
        scratch_shapes=[pltpu.VMEM((MOD_RING, d, tn), F32), pltpu.SemaphoreType.DMA((MOD_RING,))],
        compiler_params=_params("arbitrary", "arbitrary"),
        name="adaln_mod",
    )(cvec, w_ada, b_ada.reshape(depth, 1, n))


def _proj_kernel(*refs, rope):
    if rope:
        (x_ref, mod_ref, g1_ref, w_ref, qg_ref, kg_ref, seg_ref, cdft_ref,
         cos_ref, sin_ref, q_ref, k_ref, v_ref, y_ref, f_scr) = refs
    else:
        (x_ref, mod_ref, g1_ref, w_ref, qg_ref, kg_ref, seg_ref, cdft_ref,
         q_ref, k_ref, v_ref, y_ref, f_scr) = refs
    tm = x_ref.shape[1]
    sub = f_scr.shape[1]
    swap_idx = lax.broadcasted_iota(jnp.int32, (sub, LANES), 1) ^ (ROPE_AXIS_DIM // 2)

    qk_cols = 2 * ATTN_WIDTH

    gain_scale = g1_ref[...] * (1.0 + mod_ref[1:2, :])

    def norm_mod(r0):
        x = x_ref[0, r0:r0 + sub, :]
        ms = jnp.mean(x * x, axis=-1, keepdims=True)
        return (x * lax.rsqrt(ms + EPS) * gain_scale + mod_ref[0:1, :]).astype(BF16)

    def project_qk(h):
        return jnp.dot(h, w_ref[:, 0:qk_cols], preferred_element_type=F32)

    def project_vf(h):
        return jnp.dot(h, w_ref[:, qk_cols:], preferred_element_type=F32)

    def norm_rope(t, gain, out_ref, r0):
        for cb in range(ATTN_WIDTH // MXU_DIM):
            tc = t[:, MXU_DIM * cb:MXU_DIM * (cb + 1)]
            ms = jnp.dot((tc * tc).astype(BF16), seg_ref[...], preferred_element_type=F32)
            tn = tc * lax.rsqrt(ms + EPS) * gain
            for hb in range(MXU_DIM // LANES):
                u = tn[:, LANES * hb:LANES * (hb + 1)]
                if rope:
                    partner = jnp.take_along_axis(u, swap_idx, axis=1)
                    u = u * cos_ref[r0:r0 + sub, :] + partner * sin_ref[r0:r0 + sub, :]
                c0 = MXU_DIM * cb + LANES * hb
                out_ref[0, r0:r0 + sub, c0:c0 + LANES] = u.astype(BF16)

    def emit_qk(r0, z):
        norm_rope(z[:, 0:ATTN_WIDTH], qg_ref[...], q_ref, r0)
        norm_rope(z[:, ATTN_WIDTH:], kg_ref[...], k_ref, r0)

    def emit_vf(r0, z):
        v_ref[0, r0:r0 + sub, :] = z[:, 0:ATTN_WIDTH].astype(BF16)
        w = FOURIER_WIDTH
        h0 = r0 // 2
        for g in range(N_FGROUPS):
            c0 = FGROUP_DIM * g
            f_scr[g] = z[:, ATTN_WIDTH + c0:ATTN_WIDTH + c0 + FGROUP_DIM]
            for parity in range(2):
                fp = f_scr[g, pl.ds(parity, sub // 2, stride=2), :].astype(BF16)
                yg = jnp.dot(fp, cdft_ref[g], preferred_element_type=F32)
                base = 2 * w * parity
                y_ref[0, h0:h0 + sub // 2, base + c0:base + c0 + FGROUP_DIM] = (
                    yg[:, :FGROUP_DIM].astype(BF16))
                y_ref[0, h0:h0 + sub // 2, base + w + c0:base + w + c0 + FGROUP_DIM] = (
                    yg[:, FGROUP_DIM:].astype(BF16))

    starts = list(range(0, tm, sub))
    h = norm_mod(starts[0])
    z_qk = project_qk(h)
    pending_vf = None
    for t, r0 in enumerate(starts):
        if pending_vf is not None:
            emit_vf(*pending_vf)
        pending_vf = (r0, project_vf(h))
        emit_qk(r0, z_qk)
        if t + 1 < len(starts):
            h = norm_mod(starts[t + 1])
            z_qk = project_qk(h)
    emit_vf(*pending_vf)


def _project(x, mod_all, g1, w_in, qg, kg, seg, cdft, rope_tabs, *, layer, w_layer, mod_row, tm, name):
    b, l, d = x.shape
    rope = rope_tabs is not None
    row_of = (lambda i, t: i) if mod_row is None else (lambda i, t: mod_row)
    tok = lambda width: pl.BlockSpec((1, tm, width), lambda i, t: (i, t, 0))
    in_specs = [tok(d),
                _mod_spec(layer, row_of),
                _resident((1, d), layer),
                _resident((d, IN_COLS), w_layer),
                _resident((1, MXU_DIM), layer),
                _resident((1, MXU_DIM), layer),
                _resident((MXU_DIM, MXU_DIM)),
                _resident((N_FGROUPS, FGROUP_DIM, 2 * FGROUP_DIM), layer)]
    args = [x, mod_all, g1, w_in, qg, kg, seg, cdft]
    if rope:
        in_specs += [pl.BlockSpec((tm, LANES), lambda i, t: (t, 0))] * 2
        args += list(rope_tabs)
    out = lambda width: jax.ShapeDtypeStruct((b, l, width), BF16)
    return pl.pallas_call(
        functools.partial(_proj_kernel, rope=rope),
        grid=(b, l // tm),
        in_specs=in_specs,
        out_specs=[tok(ATTN_WIDTH), tok(ATTN_WIDTH), tok(ATTN_WIDTH),
                   pl.BlockSpec((1, tm // 2, 4 * FOURIER_WIDTH), lambda i, t: (i, t, 0))],
        out_shape=[out(ATTN_WIDTH), out(ATTN_WIDTH), out(ATTN_WIDTH),
                   jax.ShapeDtypeStruct((b, l // 2, 4 * FOURIER_WIDTH), BF16)],
        scratch_shapes=[pltpu.VMEM((N_FGROUPS, min(tm, MXU_DIM), FGROUP_DIM), F32)],
        compiler_params=_params("arbitrary", "arbitrary"),
        name=name,
    )(*args)


def _proj_kv_kernel(x_ref, mod_ref, g1_ref, wk_ref, wv_ref, kg_ref, seg_ref, k_ref, v_ref, *, sub):
    gain_scale = g1_ref[...] * (1.0 + mod_ref[1:2, :])
    for r0 in range(0, x_ref.shape[1], sub):
        x = x_ref[0, r0:r0 + sub, :]
        ms = jnp.mean(x * x, axis=-1, keepdims=True)
        h = (x * lax.rsqrt(ms + EPS) * gain_scale + mod_ref[0:1, :]).astype(BF16)
        zk = jnp.dot(h, wk_ref[...], preferred_element_type=F32)
        v_ref[0, r0:r0 + sub, :] = jnp.dot(h, wv_ref[...], preferred_element_type=F32).astype(BF16)
        for cb in range(ATTN_WIDTH // MXU_DIM):
            cols = pl.ds(MXU_DIM * cb, MXU_DIM)
            tc = zk[:, MXU_DIM * cb:MXU_DIM * (cb + 1)]
            seg_ms = jnp.dot((tc * tc).astype(BF16), seg_ref[...], preferred_element_type=F32)
            k_ref[0, r0:r0 + sub, cols] = (tc * lax.rsqrt(seg_ms + EPS) * kg_ref[...]).astype(BF16)


def _project_kv(x, mod_all, g1, w_in, kg, seg, *, layer, w_layer, mod_row, tm, name):
    b, l, d = x.shape
    tok = lambda width: pl.BlockSpec((1, tm, width), lambda i, t: (i, t, 0))
    w_cols = lambda blk: pl.BlockSpec((None, d, ATTN_WIDTH), lambda i, t: (w_layer, 0, blk),
                                      pipeline_mode=pl.Buffered(1))
    out = jax.ShapeDtypeStruct((b, l, ATTN_WIDTH), BF16)
    return pl.pallas_call(
        functools.partial(_proj_kv_kernel, sub=min(tm, MXU_DIM)),
        grid=(b, l // tm),
        in_specs=[tok(d), _mod_spec(layer, lambda i, t: mod_row), _resident((1, d), layer),
                  w_cols(1), w_cols(2),
                  _resident((1, MXU_DIM), layer), _resident((MXU_DIM, MXU_DIM))],
        out_specs=[tok(ATTN_WIDTH), tok(ATTN_WIDTH)],
        out_shape=[out, out],
        compiler_params=_params("arbitrary", "arbitrary"),
        name=name,
    )(x, mod_all, g1, w_in, w_in, kg, seg)


def _attn_kernel(*refs, n_seg, n_cast, lambda_init, sub_rows):
    bound_ref, q_ref = refs[0], refs[1]
    k_refs = refs[2:2 + n_seg]
    v_refs = refs[2 + n_seg:2 + 2 * n_seg]
    lam_ref, sg_ref = refs[2 + 2 * n_seg:4 + 2 * n_seg]
    cast_in = refs[4 + 2 * n_seg:4 + 2 * n_seg + n_cast]
    o_ref = refs[4 + 2 * n_seg + n_cast]
    cast_out = refs[5 + 2 * n_seg + n_cast:]


    def attend(bounded):
        wide_cache.clear()
        for src, dst in zip(cast_in, cast_out):
            dst[...] = src[...].astype(BF16)
        lv = lam_ref[...]
        lam = (jnp.exp(jnp.sum(lv[0:1] * lv[1:2], axis=-1, keepdims=True))
               - jnp.exp(jnp.sum(lv[2:3] * lv[3:4], axis=-1, keepdims=True)) + lambda_init)
        lane = lax.broadcasted_iota(jnp.int32, (1, LANES), 1)
        first = (lane < HEAD_DIM).astype(BF16)
        items = [(pl.ds(LANES * h, LANES), pl.ds(r0, sub_rows))
                 for h in range(q_ref.shape[2] // LANES) for r0 in range(0, q_ref.shape[1], sub_rows)]
        probs = scores(items[0], first, bounded)
        for t, item in enumerate(items):
            nxt = scores(items[t + 1], first, bounded) if t + 1 < len(items) else None
            finish(item, probs, lam)
            probs = nxt

    def scores(item, first, bounded):
        cols, rows = item
        q = q_ref[0, rows, cols]

        def one_map(qm):
            ss = [lax.dot_general(qm, k[0, :, cols], _NT, preferred_element_type=F32) for k in k_refs]
            if bounded:
                ps = [jnp.exp2(s) for s in ss]
            else:
                m = functools.reduce(jnp.maximum, [jnp.max(s, axis=-1, keepdims=True) for s in ss])
                ps = [jnp.exp2(s - m) for s in ss]
            l = functools.reduce(lambda a, c: a + c, [jnp.sum(p, axis=-1, keepdims=True) for p in ps])
            return [p.astype(BF16) for p in ps], l

        return one_map(q * first), one_map(q * (1.0 - first).astype(BF16))

    wide_cache = {}

    def wide_v(cols):
        if cols.start not in wide_cache:
            wide_cache[cols.start] = [
                jnp.concatenate([v[0, :, cols]] * (MXU_DIM // LANES), axis=1) for v in v_refs]
        return wide_cache[cols.start]

    def finish(item, probs, lam):
        cols, rows = item
        (p1, l1), (p2, l2) = probs
        c = (lam * l1 / l2).astype(BF16)
        o = functools.reduce(lambda a, b: a + b, [
            jnp.dot(pa - c * pb, vw, preferred_element_type=F32)[:, :LANES]
            for pa, pb, vw in zip(p1, p2, wide_v(cols))])
        o = o * (1.0 / l1)
        ms = jnp.mean(o * o, axis=-1, keepdims=True)
        o = o * lax.rsqrt(ms + EPS) * sg_ref[...] * (1.0 - lambda_init)
        o_ref[0, rows, cols] = o.astype(BF16)

    safe = bound_ref[0, 0] <= SAFE_SCORE_BOUND
    pl.when(safe)(lambda: attend(True))
    pl.when(jnp.logical_not(safe))(lambda: attend(False))


def _attention(bound, q, ks, vs, lam_rows, subln_g, casts=(), *, layer, lambda_init, tq, heads, name):
    b, lq, _ = q.shape
    n_seg = len(ks)
    width = heads * LANES
    grid = (b, N_HEADS // heads, lq // tq)
    qo_spec = pl.BlockSpec((1, tq, width), lambda i, h, t: (i, t, h))
    kv_specs = [pl.BlockSpec((1, k.shape[1], width), lambda i, h, t: (i, 0, h)) for k in ks]

    n_steps = grid[0] * grid[1] * grid[2]
    step_of = lambda i, h, t: (i * grid[1] + h) * grid[2] + t
    cast_in_specs, cast_out_specs, cast_shapes = [], [], []
    for w, w_layer in casts:
        _, r, c = w.shape
        per = next(p for p in range(1, n_steps + 1)
                   if n_steps % p == 0 and r % (n_steps // p) == 0
                   and (r // (n_steps // p)) % BF16_SUBLANES == 0)
        rows = r // (n_steps // per)
        cast_in_specs.append(pl.BlockSpec(
            (None, rows, c), lambda i, h, t, w_layer=w_layer, per=per: (w_layer, step_of(i, h, t) // per, 0)))
        cast_out_specs.append(pl.BlockSpec(
            (None, rows, c), lambda i, h, t, per=per: (0, step_of(i, h, t) // per, 0)))
        cast_shapes.append(jax.ShapeDtypeStruct((1, r, c), BF16))

    out = pl.pallas_call(
        functools.partial(_attn_kernel, n_seg=n_seg, n_cast=len(casts), lambda_init=lambda_init,
                          sub_rows=min(tq, MXU_DIM)),
        grid=grid,
        in_specs=[pl.BlockSpec(memory_space=pltpu.SMEM), qo_spec] + kv_specs + kv_specs
                 + [_resident((4, HEAD_DIM), layer), _resident((1, V_HEAD_DIM), layer)] + cast_in_specs,
        out_specs=[qo_spec] + cast_out_specs,
        out_shape=[jax.ShapeDtypeStruct((b, lq, ATTN_WIDTH), BF16)] + cast_shapes,
        compiler_params=_params("arbitrary", "arbitrary", "arbitrary"),
        name=name,
    )(bound, q, *ks, *vs, lam_rows, subln_g, *[w for w, _ in casts])
    return out[0], out[1:]


def _fold_kernel(cdft_ref, wf_ref, o_ref):
    for g in range(N_FGROUPS):
        wg = wf_ref[g].astype(BF16)
        for part in range(2):
            cols = pl.ds(FGROUP_DIM * part, FGROUP_DIM)
            o_ref[g, :, cols] = jnp.dot(cdft_ref[:, cols], wg, preferred_element_type=F32).astype(BF16)


def _fold_group_map(cdft, w_fourier):
    depth = w_fourier.shape[0]
    return pl.pallas_call(
        _fold_kernel,
        grid=(depth,),
        in_specs=[_resident((FGROUP_DIM, 2 * FGROUP_DIM)),
                  pl.BlockSpec((None, N_FGROUPS, FGROUP_DIM, FGROUP_DIM), lambda i: (i, 0, 0, 0))],
        out_specs=pl.BlockSpec((None, N_FGROUPS, FGROUP_DIM, 2 * FGROUP_DIM), lambda i: (i, 0, 0, 0)),
        out_shape=jax.ShapeDtypeStruct((depth, N_FGROUPS, FGROUP_DIM, 2 * FGROUP_DIM), BF16),
        compiler_params=_params("arbitrary"),
        name="fold_group_map",
    )(cdft, w_fourier)


def _fourier_kernel(y_ref, ce_ref, se_ref, co_ref, so_ref, o_ref):
    w = FOURIER_WIDTH
    e = (jnp.dot(ce_ref[...], y_ref[0, :, 0:w], preferred_element_type=F32)
         + jnp.dot(se_ref[...], y_ref[0, :, w:2 * w], preferred_element_type=F32))
    o = (jnp.dot(co_ref[...], y_ref[0, :, 2 * w:3 * w], preferred_element_type=F32)
         + jnp.dot(so_ref[...], y_ref[0, :, 3 * w:4 * w], preferred_element_type=F32))
    o_ref[0, 0] = (e + o).astype(BF16)
    o_ref[0, 1] = (e - o).astype(BF16)


def _fourier(y2, pos_mats, *, name):
    b, half, _ = y2.shape
    out = pl.pallas_call(
        _fourier_kernel,
        grid=(b,),
        in_specs=[pl.BlockSpec((1, half, 4 * FOURIER_WIDTH), lambda i: (i, 0, 0))]
                 + [_resident((half, half))] * 4,
        out_specs=pl.BlockSpec((1, 2, half, FOURIER_WIDTH), lambda i: (i, 0, 0, 0)),
        out_shape=jax.ShapeDtypeStruct((b, 2, half, FOURIER_WIDTH), BF16),
        compiler_params=_params("arbitrary"),
        name=name,
    )(y2, *pos_mats)
    return out.reshape(b, 2 * half, FOURIER_WIDTH)


def _ffn_kernel(x_ref, a_ref, f_ref, mod_ref, g2_ref, wo_ref, wg_ref, wu_ref, wd_ref, o_ref, *, sub):
    tm = x_ref.shape[1]
    gain_scale = g2_ref[...] * (1.0 + mod_ref[4:5, :])

    def mix_norm(r0):
        rows = pl.ds(r0, sub)
        mix = (jnp.dot(a_ref[0, rows, :], wo_ref[0:ATTN_WIDTH, :], preferred_element_type=F32)
               + jnp.dot(f_ref[0, rows, :], wo_ref[ATTN_WIDTH:, :], preferred_element_type=F32))
        x1 = x_ref[0, rows, :] + mod_ref[2:3, :] * mix
        ms = jnp.mean(x1 * x1, axis=-1, keepdims=True)
        h = (x1 * lax.rsqrt(ms + EPS) * gain_scale + mod_ref[3:4, :]).astype(BF16)
        return x1, h

    def swiglu(r0, x1, h):
        gate = jnp.dot(h, wg_ref[...], preferred_element_type=F32)
        up = jnp.dot(h, wu_ref[...], preferred_element_type=F32)
        act = (gate / (1.0 + jnp.exp(-gate)) * up).astype(BF16)
        y = jnp.dot(act, wd_ref[...], preferred_element_type=F32)
        o_ref[0, pl.ds(r0, sub), :] = x1 + mod_ref[5:6, :] * y

    starts = list(range(0, tm, sub))
    cur = mix_norm(starts[0])
    for t, r0 in enumerate(starts):
        nxt = mix_norm(starts[t + 1]) if t + 1 < len(starts) else None
        swiglu(r0, *cur)
        cur = nxt


def _out_ffn(x, attn, four, mod_all, g2, w_out, w_gate, w_up, w_down, *, layer, w_layer, mod_row, tm, name):
    b, l, d = x.shape
    d_ff = w_gate.shape[-1]
    row_of = (lambda i, t: i) if mod_row is None else (lambda i, t: mod_row)
    tok = lambda width: pl.BlockSpec((1, tm, width), lambda i, t: (i, t, 0))
    return pl.pallas_call(
        functools.partial(_ffn_kernel, sub=min(tm, MXU_DIM)),
        grid=(b, l // tm),
        in_specs=[tok(d), tok(ATTN_WIDTH), tok(FOURIER_WIDTH),
                  _mod_spec(layer, row_of),
                  _resident((1, d), layer),
                  _resident((d, d), w_layer),
                  _resident((d, d_ff), w_layer),
                  _resident((d, d_ff), w_layer),
                  _resident((d_ff, d), w_layer)],
        out_specs=tok(d),
        out_shape=jax.ShapeDtypeStruct((b, l, d), F32),
        compiler_params=_params("arbitrary", "arbitrary"),
        name=name,
    )(x, attn, four, mod_all, g2, w_out, w_gate, w_up, w_down)


def kernel(x, c, ctx, c_ctx, w_ada, b_ada, norm1_g, norm2_g, w_in, q_norm_g, k_norm_g, lambda_q1, lambda_k1, lambda_q2, lambda_k2, subln_g, w_fourier, w_out, w_gate, w_up, w_down):
    depth = w_ada.shape[0]
    b, l, d = x.shape
    lc = ctx.shape[1]
    assert b + 1 <= MOD_ROWS and lc % 2 == 0
    tm_ctx = math.gcd(b * lc, TOKEN_TILE // 2)

    rope_tabs = _rope_tables(l)
    cdft = _fold_group_map(_channel_dft(), w_fourier)
    pos_lat = _position_dft(l)
    pos_ctx = _position_dft(lc)
    seg = _segment_mean()

    cvec = jnp.zeros((MOD_ROWS, d), F32).at[:b].set(c).at[b].set(c_ctx)
    mod_all = _adaln_mod(cvec, w_ada, b_ada).reshape(depth, MOD_ROWS, 6, d)

    w_in_b = w_in.astype(BF16)
    g1, g2 = norm1_g.reshape(depth, 1, d), norm2_g.reshape(depth, 1, d)
    tile_gain = lambda g: jnp.tile(g, (1, MXU_DIM // HEAD_DIM)).reshape(depth, 1, MXU_DIM)
    qg, kg = tile_gain(q_norm_g * (HEAD_DIM ** -0.5 * LOG2_E)), tile_gain(k_norm_g)
    lam_rows = jnp.stack([lambda_q1, lambda_k1, lambda_q2, lambda_k2], axis=1)
    sg = subln_g.reshape(depth, 1, V_HEAD_DIM)
    score_bound = (HEAD_DIM ** 0.5 * LOG2_E) * jnp.max(jnp.abs(q_norm_g), axis=1) * jnp.max(jnp.abs(k_norm_g), axis=1)

    for i in range(depth):
        last = i == depth - 1
        lambda_init = 0.8 - 0.6 * math.exp(-0.3 * i)
        bound = score_bound[i].reshape(1, 1)
        proj = functools.partial(_project, mod_all=mod_all, g1=g1, w_in=w_in_b, qg=qg, kg=kg,
                                 seg=seg, cdft=cdft, layer=i, w_layer=i)
        attend = functools.partial(_attention, bound, lam_rows=lam_rows, subln_g=sg, layer=i,
                                   lambda_init=lambda_init)

        flat = lambda a: a.reshape(1, b * lc, a.shape[-1])
        if last:
            kc, vc = _project_kv(flat(ctx), mod_all, g1, w_in_b, kg, seg, layer=i, w_layer=i,
                                 mod_row=b, tm=tm_ctx, name="proj_ctx_kv")
            kc, vc = (a.reshape(b, lc, ATTN_WIDTH) for a in (kc, vc))
        else:
            qc, kc, vc, yc = proj(flat(ctx), rope_tabs=None, mod_row=b, tm=tm_ctx, name="proj_ctx")
            qc, kc, vc = (a.reshape(b, lc, ATTN_WIDTH) for a in (qc, kc, vc))
            yc = yc.reshape(b, lc // 2, 4 * FOURIER_WIDTH)
        qx, kx, vx, yx = proj(x, rope_tabs=rope_tabs, mod_row=None, tm=l, name="proj_lat")

        casts = [(w, i) for w in (w_out, w_gate, w_up, w_down)]
        attn_x, rounded = attend(qx, [kc, kx], [vc, vx], casts=casts, tq=l, heads=1, name="attn_lat")
        ffn = functools.partial(_out_ffn, mod_all=mod_all, g2=g2, w_out=rounded[0], w_gate=rounded[1],
                                w_up=rounded[2], w_down=rounded[3], layer=i, w_layer=0)
        four_x = _fourier(yx, pos_lat, name="fourier_lat")
        x_new = ffn(x, attn_x, four_x, mod_row=None, tm=TOKEN_TILE, name="out_ffn_lat")
        if not last:
            attn_c, _ = attend(qc, [kc], [vc], tq=lc, heads=N_HEADS, name="attn_ctx")
            four_c = _fourier(yc, pos_ctx, name="fourier_ctx")
            ctx = ffn(flat(ctx), flat(attn_c), flat(four_c), mod_row=b, tm=tm_ctx,
                      name="out_ffn_ctx").reshape(b, lc, d)
        x = x_new
    return x
```
